```python
import jax
import jax.numpy as jnp
from jax import lax
import numpy as np

D_MODEL = 2048
BATCH = 4
SEQ = 2048
DEPTH = 4
DEC_BATCH = 8
DEC_SEQ = 4
PAST_LEN = 16384
PAGE_SIZE = 128

PLE_DIM = 256
N_MIXERS = 3
D_FF = 4 * D_MODEL
NORM_EPS = 1e-6

SB_HEADS = 16
SB_HEAD_DIM = D_MODEL // SB_HEADS
SB_SCALE = SB_HEAD_DIM ** -0.5
Q_BLOCK = 128
SB_BIAS_INIT = -7.0

ML_HEADS = 8
ML_QK_DIM = D_MODEL // (2 * ML_HEADS)
ML_V_DIM = D_MODEL // ML_HEADS
ML_K_SCALE = ML_QK_DIM ** -0.5
ML_CHUNK = 128
GATE_SOFTCAP = 15.0
ML_FGATE_BIAS = 3.0

HG_HEADS = 16
HG_HEAD_DIM = D_MODEL // HG_HEADS
HG_CHUNK = 32

ATTN_LAYERS = tuple(i for i in range(DEPTH) if i % N_MIXERS == 0)
MLSTM_LAYERS = tuple(i for i in range(DEPTH) if i % N_MIXERS == 1)
HGRN_LAYERS = tuple(i for i in range(DEPTH) if i % N_MIXERS == 2)
N_ATTN = len(ATTN_LAYERS)
N_MLSTM = len(MLSTM_LAYERS)
N_HGRN = len(HGRN_LAYERS)
ML_IN_WIDTH = 2 * ML_HEADS * ML_QK_DIM + 2 * D_MODEL + 2 * ML_HEADS

kernel_name = 'hybrid_sb_mlstm_hgrn2_step'


def rms_norm(x, g):
    xf = x.astype(jnp.float32)
    xf = xf * lax.rsqrt(jnp.mean(jnp.square(xf), axis=-1, keepdims=True) + NORM_EPS)
    return (xf * g.astype(jnp.float32)).astype(x.dtype)


def _chunk_len(t, c):
    return c if t % c == 0 else t


def _to_chunks(a, n_chunks):
    b, t = a.shape[:2]
    return jnp.swapaxes(a.reshape((b, n_chunks, t // n_chunks) + a.shape[2:]), 0, 1)


def _from_chunks(a):
    a = jnp.swapaxes(a, 0, 1)
    return a.reshape((a.shape[0], a.shape[1] * a.shape[2]) + a.shape[3:])


def stick_breaking_weights(z, mask):
    log_keep = jnp.where(mask, jax.nn.log_sigmoid(-z), 0.0)
    tail = lax.cumsum(log_keep, axis=z.ndim - 1, reverse=True) - log_keep
    return jnp.where(mask, jnp.exp(jax.nn.log_sigmoid(z) + tail), 0.0)


def sb_qkv(u, w_qkv, g_q, g_k):
    b, t, _ = u.shape
    qkv = (u @ w_qkv).reshape(b, t, 3, SB_HEADS, SB_HEAD_DIM)
    return rms_norm(qkv[:, :, 0], g_q), rms_norm(qkv[:, :, 1], g_k), qkv[:, :, 2]


def sb_attend_prompt(q, k, v, bias):
    t = q.shape[1]
    n_blocks = t // Q_BLOCK
    k_pos = jnp.arange(t)
    vf = v.astype(jnp.float32)
    bias_f = bias.astype(jnp.float32)[None, :, None, None]

    def block(args):
        qb, start = args
        q_pos = start + jnp.arange(Q_BLOCK)
        z = jnp.einsum('bqhd,bkhd->bhqk', qb, k, preferred_element_type=jnp.float32) * SB_SCALE + bias_f
        w = stick_breaking_weights(z, k_pos[None, :] < q_pos[:, None])
        return jnp.einsum('bhqk,bkhd->bqhd', w, vf).astype(q.dtype)

    out = lax.map(block, (_to_chunks(q, n_blocks), jnp.arange(n_blocks) * Q_BLOCK))
    return _from_chunks(out)


def sb_attend_sample(q, k, v, past_k, past_v, bias):
    t = q.shape[1]
    past = past_k.shape[1]
    bias_f = bias.astype(jnp.float32)[None, :, None, None]
    z = jnp.concatenate([
        jnp.einsum('bqhd,bkhd->bhqk', q, past_k, preferred_element_type=jnp.float32),
        jnp.einsum('bqhd,bkhd->bhqk', q, k, preferred_element_type=jnp.float32)], axis=-1) * SB_SCALE + bias_f
    k_pos = jnp.arange(past + t)
    q_pos = past + jnp.arange(t)
    w = stick_breaking_weights(z, k_pos[None, :] < q_pos[:, None])
    out = (jnp.einsum('bhqk,bkhd->bqhd', w[..., :past], past_v.astype(jnp.float32))
           + jnp.einsum('bhqk,bkhd->bqhd', w[..., past:], v.astype(jnp.float32)))
    return out.astype(q.dtype)


def mlstm_chunk(carry, inp):
    c, n, m = carry
    q, k, v, log_i, log_f = inp
    L = q.shape[1]
    causal = jnp.tril(jnp.ones((L, L), dtype=bool))
    bcum = jnp.cumsum(log_f, axis=1)
    dmat = bcum[:, :, None, :] - bcum[:, None, :, :] + log_i[:, None, :, :]
    dmat = jnp.where(causal[None, :, :, None], dmat, -jnp.inf)
    inter = bcum + m[:, None, :]
    m_t = jnp.maximum(inter, jnp.max(dmat, axis=2))
    a = jnp.exp(dmat - m_t[:, :, None, :]) * jnp.einsum('bthd,bshd->btsh', q, k)
    sc = jnp.exp(inter - m_t)
    num = jnp.einsum('btsh,bshv->bthv', a, v) + sc[..., None] * jnp.einsum('bthd,bhdv->bthv', q, c)
    den = jnp.sum(a, axis=2) + sc * jnp.einsum('bthd,bhd->bth', q, n)
    h = num / jnp.maximum(jnp.abs(den), jnp.exp(-m_t))[..., None]
    m_new = m_t[:, -1]
    wl = jnp.exp(bcum[:, -1:, :] - bcum + log_i - m_new[:, None, :])
    sl = jnp.exp(bcum[:, -1] + m - m_new)
    c_new = sl[..., None, None] * c + jnp.einsum('bsh,bshd,bshv->bhdv', wl, k, v)
    n_new = sl[..., None] * n + jnp.einsum('bsh,bshd->bhd', wl, k)
    return (c_new, n_new, m_new), h


def mlstm_mixer(u, c0, n0, m0, w_in, b_gate, g_out, w_out):
    b, t, _ = u.shape
    qk_w = ML_HEADS * ML_QK_DIM
    proj = u @ w_in
    q = proj[..., :qk_w].reshape(b, t, ML_HEADS, ML_QK_DIM).astype(jnp.float32)
    k = proj[..., qk_w:2 * qk_w].reshape(b, t, ML_HEADS, ML_QK_DIM).astype(jnp.float32) * ML_K_SCALE
    v = proj[..., 2 * qk_w:2 * qk_w + D_MODEL].reshape(b, t, ML_HEADS, ML_V_DIM).astype(jnp.float32)
    o = proj[..., 2 * qk_w + D_MODEL:2 * qk_w + 2 * D_MODEL]
    pre = proj[..., 2 * qk_w + 2 * D_MODEL:].astype(jnp.float32) + b_gate.astype(jnp.float32)
    pre = GATE_SOFTCAP * jnp.tanh(pre / GATE_SOFTCAP)
    log_i = pre[..., :ML_HEADS]
    log_f = jax.nn.log_sigmoid(pre[..., ML_HEADS:])
    n_chunks = t // _chunk_len(t, ML_CHUNK)
    xs = (_to_chunks(q, n_chunks), _to_chunks(k, n_chunks), _to_chunks(v, n_chunks),
          _to_chunks(log_i, n_chunks), _to_chunks(log_f, n_chunks))
    carry0 = (c0.astype(jnp.float32), n0.astype(jnp.float32), m0.astype(jnp.float32))
    (c1, n1, m1), h = lax.scan(mlstm_chunk, carry0, xs)
    h = rms_norm(_from_chunks(h), g_out).reshape(b, t, D_MODEL)
    y = (jax.nn.sigmoid(o.astype(jnp.float32)) * h).astype(u.dtype) @ w_out
    return y, c1, n1, m1


def hgrn_chunk(s, inp):
    q, k, i, log_f = inp
    L = q.shape[1]
    causal = jnp.tril(jnp.ones((L, L), dtype=bool))
    bcum = jnp.cumsum(log_f, axis=1)
    diff = bcum[:, :, None] - bcum[:, None, :]
    decay = jnp.exp(jnp.where(causal[None, :, :, None, None], diff, -jnp.inf))
    a = jnp.einsum('bthd,bshd,btshd->btsh', q, k, decay)
    o = jnp.einsum('btsh,bshv->bthv', a, i) + jnp.einsum('bthd,bhdv->bthv', q * jnp.exp(bcum), s)
    blast = bcum[:, -1]
    s_new = jnp.exp(blast)[..., None] * s + jnp.einsum('bshd,bshv->bhdv', k * jnp.exp(blast[:, None] - bcum), i)
    return s_new, o


def hgrn_mixer(u, s0, w_in, lower_bound, g_out, w_out):
    b, t, _ = u.shape
    proj = u @ w_in
    shp = (b, t, HG_HEADS, HG_HEAD_DIM)
    q = proj[..., :D_MODEL].reshape(shp).astype(jnp.float32)
    fp = proj[..., D_MODEL:2 * D_MODEL].reshape(shp).astype(jnp.float32)
    inp = proj[..., 2 * D_MODEL:3 * D_MODEL].reshape(shp).astype(jnp.float32)
    g = proj[..., 3 * D_MODEL:]
    lb = lower_bound.reshape(HG_HEADS, HG_HEAD_DIM)
    log_f = jnp.logaddexp(jnp.log(lb), jnp.log1p(-lb) + jax.nn.log_sigmoid(fp))
    k = (1.0 - lb) * jax.nn.sigmoid(-fp)
    n_chunks = t // _chunk_len(t, HG_CHUNK)
    xs = (_to_chunks(q, n_chunks), _to_chunks(k, n_chunks), _to_chunks(inp, n_chunks),
          _to_chunks(log_f, n_chunks))
    s1, o = lax.scan(hgrn_chunk, s0.astype(jnp.float32), xs)
    o = rms_norm(_from_chunks(o), g_out).reshape(b, t, D_MODEL)
    y = (jax.nn.sigmoid(g.astype(jnp.float32)) * o).astype(u.dtype) @ w_out
    return y, s1


def sq_relu_ffn(u, w_up, w_down):
    a = jax.nn.relu(u @ w_up)
    return (a * a) @ w_down


def per_layer_embedding(h, p, g_norm, w_gate, w_proj):
    gate = jax.nn.sigmoid((rms_norm(h, g_norm) @ w_gate).astype(jnp.float32))
    return h + (gate * (p @ w_proj).astype(jnp.float32)).astype(h.dtype)


def setup_inputs(seed: int = 0) -> dict:
    key = jax.random.key(seed)
    ks = jax.random.split(key, 32)
    f32 = jnp.float32

    def nrm(k, shape, scale=1.0):
        return jax.random.normal(k, shape, f32) * scale

    def gain(k, shape):
        return 1.0 + 0.02 * jax.random.normal(k, shape, f32)

    n_pages = PAST_LEN // PAGE_SIZE
    n_pool = (DEC_BATCH * n_pages * 5) // 4
    page_table = jax.random.permutation(ks[8], n_pool)[:DEC_BATCH * n_pages]
    page_table = page_table.reshape(DEC_BATCH, n_pages).astype(jnp.int32)
    ml_b_gate = jnp.concatenate([nrm(ks[20], (N_MLSTM, ML_HEADS), 0.1),
                                 ML_FGATE_BIAS + nrm(ks[21], (N_MLSTM, ML_HEADS), 0.5)], axis=-1)
    dsc = D_MODEL ** -0.5
    return {
        'x_prompt': nrm(ks[0], (BATCH, SEQ, D_MODEL)),
        'x_sample': nrm(ks[1], (DEC_BATCH, DEC_SEQ, D_MODEL)),
        'cache_k': nrm(ks[2], (N_ATTN, n_pool, PAGE_SIZE, SB_HEADS, SB_HEAD_DIM)),
        'cache_v': nrm(ks[3], (N_ATTN, n_pool, PAGE_SIZE, SB_HEADS, SB_HEAD_DIM)),
        'state_mlstm_c': nrm(ks[4], (N_MLSTM, DEC_BATCH, ML_HEADS, ML_QK_DIM, ML_V_DIM), 0.05),
        'state_mlstm_n': nrm(ks[5], (N_MLSTM, DEC_BATCH, ML_HEADS, ML_QK_DIM), 0.05),
        'state_mlstm_m': nrm(ks[6], (N_MLSTM, DEC_BATCH, ML_HEADS)),
        'state_hgrn': nrm(ks[7], (N_HGRN, DEC_BATCH, HG_HEADS, HG_HEAD_DIM, HG_HEAD_DIM), 0.3),
        'page_table': page_table,
        'p_prompt': nrm(ks[9], (DEPTH, BATCH, SEQ, PLE_DIM)),
        'p_sample': nrm(ks[10], (DEPTH, DEC_BATCH, DEC_SEQ, PLE_DIM)),
        'norm_mix': gain(ks[11], (DEPTH, D_MODEL)),
        'norm_ffn': gain(ks[12], (DEPTH, D_MODEL)),
        'norm_ple': gain(ks[13], (DEPTH, D_MODEL)),
        'sb_w_qkv': nrm(ks[14], (N_ATTN, D_MODEL, 3 * D_MODEL), dsc),
        'sb_q_norm': gain(ks[15], (N_ATTN, SB_HEAD_DIM)),
        'sb_k_norm': gain(ks[16], (N_ATTN, SB_HEAD_DIM)),
        'sb_logit_bias': SB_BIAS_INIT + nrm(ks[31], (N_ATTN, SB_HEADS), 0.3),
        'sb_w_out': nrm(ks[17], (N_ATTN, D_MODEL, D_MODEL), dsc),
        'ml_w_in': nrm(ks[18], (N_MLSTM, D_MODEL, ML_IN_WIDTH), dsc),
        'ml_b_gate': ml_b_gate,
        'ml_out_norm': gain(ks[19], (N_MLSTM, ML_V_DIM)),
        'ml_w_out': nrm(ks[22], (N_MLSTM, D_MODEL, D_MODEL), dsc),
        'hg_w_in': nrm(ks[23], (N_HGRN, D_MODEL, 4 * D_MODEL), dsc),
        'hg_lb_logits': nrm(ks[24], (DEPTH, D_MODEL), 0.5),
        'hg_out_norm': gain(ks[25], (N_HGRN, HG_HEAD_DIM)),
        'hg_w_out': nrm(ks[26], (N_HGRN, D_MODEL, D_MODEL), dsc),
        'ffn_w_up': nrm(ks[27], (DEPTH, D_MODEL, D_FF), dsc),
        'ffn_w_down': nrm(ks[28], (DEPTH, D_FF, D_MODEL), D_FF ** -0.5),
        'ple_w_proj': nrm(ks[29], (DEPTH, PLE_DIM, D_MODEL), PLE_DIM ** -0.5),
        'ple_w_gate': nrm(ks[30], (DEPTH, D_MODEL, D_MODEL), dsc),
    }


def reference(x_prompt, x_sample, cache_k, cache_v, state_mlstm_c, state_mlstm_n, state_mlstm_m,
              state_hgrn, page_table, p_prompt, p_sample, norm_mix, norm_ffn, norm_ple,
              sb_w_qkv, sb_q_norm, sb_k_norm, sb_logit_bias, sb_w_out, ml_w_in, ml_b_gate,
              ml_out_norm, ml_w_out, hg_w_in, hg_lb_logits, hg_out_norm, hg_w_out,
              ffn_w_up, ffn_w_down, ple_w_proj, ple_w_gate):
    n_seq, n_pages = page_table.shape
    past = n_pages * PAGE_SIZE
    b_p = x_prompt.shape[0]
    lb_soft = jax.nn.softmax(hg_lb_logits.astype(jnp.float32), axis=0)
    lower_bounds = jnp.cumsum(lb_soft, axis=0) - lb_soft

    hp, hs = x_prompt, x_sample
    kp_l, vp_l, ks_l, vs_l = [], [], [], []
    mcp_l, mnp_l, mmp_l, mcs_l, mns_l, mms_l = [], [], [], [], [], []
    hgp_l, hgs_l = [], []
    for layer in range(DEPTH):
        kind = layer % N_MIXERS
        up = rms_norm(hp, norm_mix[layer])
        us = rms_norm(hs, norm_mix[layer])
        if kind == 0:
            j = ATTN_LAYERS.index(layer)
            qp, kp, vp = sb_qkv(up, sb_w_qkv[j], sb_q_norm[j], sb_k_norm[j])
            qs, kss, vss = sb_qkv(us, sb_w_qkv[j], sb_q_norm[j], sb_k_norm[j])
            past_k = cache_k[j][page_table].reshape(n_seq, past, SB_HEADS, SB_HEAD_DIM)
            past_v = cache_v[j][page_table].reshape(n_seq, past, SB_HEADS, SB_HEAD_DIM)
            yp = sb_attend_prompt(qp, kp, vp, sb_logit_bias[j]).reshape(hp.shape) @ sb_w_out[j]
            ys = sb_attend_sample(qs, kss, vss, past_k, past_v, sb_logit_bias[j]).reshape(hs.shape) @ sb_w_out[j]
            kp_l.append(kp)
            vp_l.append(vp)
            ks_l.append(kss)
            vs_l.append(vss)
        elif kind == 1:
            j = MLSTM_LAYERS.index(layer)
            c0 = jnp.zeros((b_p, ML_HEADS, ML_QK_DIM, ML_V_DIM), jnp.float32)
            n0 = jnp.zeros((b_p, ML_HEADS, ML_QK_DIM), jnp.float32)
            m0 = jnp.zeros((b_p, ML_HEADS), jnp.float32)
            yp, cp, npv, mp = mlstm_mixer(up, c0, n0, m0, ml_w_in[j], ml_b_gate[j], ml_out_norm[j], ml_w_out[j])
            ys, cs, nsv, ms = mlstm_mixer(us, state_mlstm_c[j], state_mlstm_n[j], state_mlstm_m[j],
                                          ml_w_in[j], ml_b_gate[j], ml_out_norm[j], ml_w_out[j])
            mcp_l.append(cp)
            mnp_l.append(npv)
            mmp_l.append(mp)
            mcs_l.append(cs)
            mns_l.append(nsv)
            mms_l.append(ms)
        else:
            j = HGRN_LAYERS.index(layer)
            s0 = jnp.zeros((b_p, HG_HEADS, HG_HEAD_DIM, HG_HEAD_DIM), jnp.float32)
            yp, sp = hgrn_mixer(up, s0, hg_w_in[j], lower_bounds[layer], hg_out_norm[j], hg_w_out[j])
            ys, ss = hgrn_mixer(us, state_hgrn[j], hg_w_in[j], lower_bounds[layer], hg_out_norm[j], hg_w_out[j])
            hgp_l.append(sp)
            hgs_l.append(ss)
        hp = hp + yp.astype(hp.dtype)
        hs = hs + ys.astype(hs.dtype)
        hp = hp + sq_relu_ffn(rms_norm(hp, norm_ffn[layer]), ffn_w_up[layer], ffn_w_down[layer])
        hs = hs + sq_relu_ffn(rms_norm(hs, norm_ffn[layer]), ffn_w_up[layer], ffn_w_down[layer])
        hp = per_layer_embedding(hp, p_prompt[layer], norm_ple[layer], ple_w_gate[layer], ple_w_proj[layer])
        hs = per_layer_embedding(hs, p_sample[layer], norm_ple[layer], ple_w_gate[layer], ple_w_proj[layer])

    k_prompt = jnp.stack(kp_l)
    v_prompt = jnp.stack(vp_l)
    k_sample = jnp.stack(ks_l)
    v_sample = jnp.stack(vs_l)
    mlstm_c_prompt = jnp.stack(mcp_l)
    mlstm_n_prompt = jnp.stack(mnp_l)
    mlstm_m_prompt = jnp.stack(mmp_l)
    mlstm_c_sample = jnp.stack(mcs_l)
    mlstm_n_sample = jnp.stack(mns_l)
    mlstm_m_sample = jnp.stack(mms_l)
    hgrn_s_prompt = jnp.stack(hgp_l)
    hgrn_s_sample = jnp.stack(hgs_l)
    return (hp, hs, k_prompt, v_prompt, k_sample, v_sample,
            mlstm_c_prompt, mlstm_n_prompt, mlstm_m_prompt,
            mlstm_c_sample, mlstm_n_sample, mlstm_m_sample,
            hgrn_s_prompt, hgrn_s_sample)
```

```python
import functools

import numpy as np
import jax
import jax.numpy as jnp
from jax import lax
from jax.experimental import pallas as pl
from jax.experimental.pallas import tpu as pltpu

F32 = jnp.float32
BF16 = jnp.bfloat16

NORM_EPS = 1e-6
N_MIXERS = 3
GATE_SOFTCAP = 15.0
LANES = 128
SUBLANES = 8
VMEM_LIMIT_BYTES = 56 * 1024 * 1024
ROW_TILE = 1024
COL_TILE = 512
FFN_TILE = 256
SB_Q_TILE = 256
SB_PAGES_PER_STEP = 4
CHUNK = 128
NEG_BIG = -1e30


def _cparams(*sem):
    return pltpu.CompilerParams(dimension_semantics=sem, vmem_limit_bytes=VMEM_LIMIT_BYTES)


def _dot(a, b):
    return jnp.dot(a, b, preferred_element_type=F32)


def _dot_nt(a, b):
    return lax.dot_general(a, b, (((1,), (1,)), ((), ())), preferred_element_type=F32)


def _split_bf16(x):
    hi = x.astype(BF16)
    lo = (x - hi.astype(F32)).astype(BF16)
    return hi, lo


def _softplus(z):
    return jnp.maximum(z, 0.0) + jnp.log1p(jnp.exp(-jnp.abs(z)))


def _rms(x, gain):
    ms = jnp.mean(x * x, axis=-1, keepdims=True)
    return x * lax.rsqrt(ms + NORM_EPS) * gain


def _rmsnorm_kernel(x_ref, g_ref, o_ref):
    o_ref[...] = _rms(x_ref[...], g_ref[...]).astype(o_ref.dtype)


def rmsnorm_bf16(x, gains, layer):
    m, d = x.shape
    tm = min(m, 512)
    return pl.pallas_call(
        _rmsnorm_kernel,
        grid=(m // tm,),
        in_specs=[pl.BlockSpec((tm, d), lambda i: (i, 0)),
                  pl.BlockSpec((None, 1, d), lambda i: (layer, 0, 0))],
        out_specs=pl.BlockSpec((tm, d), lambda i: (i, 0)),
        out_shape=jax.ShapeDtypeStruct((m, d), BF16),
        compiler_params=_cparams("parallel"),
        name="rmsnorm",
    )(x, gains)


def _mm_kernel(x_ref, w_ref, gain_ref, o_ref, *, n_norm_blocks):
    acc = _dot(x_ref[...].astype(BF16), w_ref[...].astype(BF16))
    if n_norm_blocks == 0:
        o_ref[...] = acc
        return
    j = pl.program_id(1)

    @pl.when(j >= n_norm_blocks)
    def _():
        o_ref[...] = acc

    @pl.when(j < n_norm_blocks)
    def _():
        for g in range(acc.shape[1] // LANES):
            sl = slice(g * LANES, (g + 1) * LANES)
            o_ref[:, sl] = _rms(acc[:, sl], gain_ref[:, sl])


def matmul_headnorm(x, w_all, layer, n_cols, head_gain):
    m, k = x.shape
    tm = min(m, ROW_TILE)
    tn = COL_TILE
    return pl.pallas_call(
        functools.partial(_mm_kernel, n_norm_blocks=head_gain.shape[1] // tn),
        grid=(m // tm, n_cols // tn),
        in_specs=[pl.BlockSpec((tm, k), lambda i, j: (i, 0)),
                  pl.BlockSpec((None, k, tn), lambda i, j: (layer, 0, j)),
                  pl.BlockSpec((1, tn), lambda i, j: (0, jnp.minimum(j, head_gain.shape[1] // tn - 1)))],
        out_specs=pl.BlockSpec((tm, tn), lambda i, j: (i, j)),
        out_shape=jax.ShapeDtypeStruct((m, n_cols), F32),
        compiler_params=_cparams("parallel", "arbitrary"),
        name="matmul",
    )(x, w_all, head_gain)


def _mm_plain_kernel(x_ref, w_ref, o_ref):
    o_ref[...] = _dot(x_ref[...].astype(BF16), w_ref[...].astype(BF16))


def matmul_plain(x, w_all, layer, n_cols):
    m, k = x.shape
    tm = min(m, ROW_TILE)
    tn = COL_TILE
    return pl.pallas_call(
        _mm_plain_kernel,
        grid=(m // tm, n_cols // tn),
        in_specs=[pl.BlockSpec((tm, k), lambda i, j: (i, 0)),
                  pl.BlockSpec((None, k, tn), lambda i, j: (layer, 0, j))],
        out_specs=pl.BlockSpec((tm, tn), lambda i, j: (i, j)),
        out_shape=jax.ShapeDtypeStruct((m, n_cols), F32),
        compiler_params=_cparams("parallel", "arbitrary"),
        name="matmul_plain",
    )(x, w_all)


def _mm_res_kernel(x_ref, w_ref, r_ref, o_ref):
    o_ref[...] = r_ref[...] + _dot(x_ref[...].astype(BF16), w_ref[...].astype(BF16))


def matmul_residual(x, w_all, layer, res):
    m, k = x.shape
    n = res.shape[1]
    tm = min(m, ROW_TILE)
    tn = COL_TILE
    return pl.pallas_call(
        _mm_res_kernel,
        grid=(m // tm, n // tn),
        in_specs=[pl.BlockSpec((tm, k), lambda i, j: (i, 0)),
                  pl.BlockSpec((None, k, tn), lambda i, j: (layer, 0, j)),
                  pl.BlockSpec((tm, tn), lambda i, j: (i, j))],
        out_specs=pl.BlockSpec((tm, tn), lambda i, j: (i, j)),
        out_shape=jax.ShapeDtypeStruct((m, n), F32),
        compiler_params=_cparams("parallel", "arbitrary"),
        name="matmul_residual",
    )(x, w_all, res)


def _ffn_kernel(x_ref, g_ref, wu_ref, wd_ref, o_ref, u_scr):
    f = pl.program_id(1)

    @pl.when(f == 0)
    def _():
        x = x_ref[...]
        u_scr[...] = _rms(x, g_ref[...]).astype(BF16)
        o_ref[...] = x

    a = jnp.maximum(_dot(u_scr[...], wu_ref[...].astype(BF16)), 0.0)
    a = (a * a).astype(BF16)
    o_ref[...] += _dot(a, wd_ref[...].astype(BF16))


def ffn(h, gains, w_up, w_down, layer):
    m, d = h.shape
    d_ff = w_up.shape[2]
    tm = min(m, ROW_TILE)
    tf = FFN_TILE
    return pl.pallas_call(
        _ffn_kernel,
        grid=(m // tm, d_ff // tf),
        in_specs=[pl.BlockSpec((tm, d), lambda i, f: (i, 0), pipeline_mode=pl.Buffered(1)),
                  pl.BlockSpec((None, 1, d), lambda i, f: (layer, 0, 0)),
                  pl.BlockSpec((None, d, tf), lambda i, f: (layer, 0, f)),
                  pl.BlockSpec((None, tf, d), lambda i, f: (layer, f, 0))],
        out_specs=pl.BlockSpec((tm, d), lambda i, f: (i, 0)),
        out_shape=jax.ShapeDtypeStruct((m, d), F32),
        scratch_shapes=[pltpu.VMEM((tm, d), BF16)],
        compiler_params=_cparams("parallel", "arbitrary"),
        name="ffn",
    )(h, gains, w_up, w_down)


def _ple_kernel(u_ref, wg_ref, p_ref, wp_ref, h_ref, o_ref):
    gate = jax.nn.sigmoid(_dot(u_ref[...], wg_ref[...].astype(BF16)))
    pp = _dot(p_ref[...].astype(BF16), wp_ref[...].astype(BF16))
    o_ref[...] = h_ref[...] + gate * pp


def ple(h, u, p_all, w_gate, w_proj, layer):
    m, d = h.shape
    dp = p_all.shape[2]
    tm = min(m, ROW_TILE)
    tn = COL_TILE
    return pl.pallas_call(
        _ple_kernel,
        grid=(m // tm, d // tn),
        in_specs=[pl.BlockSpec((tm, d), lambda i, j: (i, 0)),
                  pl.BlockSpec((None, d, tn), lambda i, j: (layer, 0, j)),
                  pl.BlockSpec((None, tm, dp), lambda i, j: (layer, i, 0)),
                  pl.BlockSpec((None, dp, tn), lambda i, j: (layer, 0, j)),
                  pl.BlockSpec((tm, tn), lambda i, j: (i, j))],
        out_specs=pl.BlockSpec((tm, tn), lambda i, j: (i, j)),
        out_shape=jax.ShapeDtypeStruct((m, d), F32),
        compiler_params=_cparams("parallel", "arbitrary"),
        name="ple",
    )(u, w_gate, p_all, w_proj, h)


def _cumsum_weights():
    row = lax.broadcasted_iota(jnp.int32, (LANES, 2 * LANES), 0)
    col = lax.broadcasted_iota(jnp.int32, (LANES, 2 * LANES), 1)
    return jnp.where((col >= LANES) | (row > col), 1.0, 0.0).astype(BF16)


def _sb_block(z, mask, carry, v_bf16, uo):
    sp = _softplus(z)
    log_keep = -sp if mask is None else jnp.where(mask, -sp, 0.0)
    hi, lo = _split_bf16(log_keep)
    cu = _dot(hi, uo) + _dot(lo, uo)
    w = jnp.exp(z - sp + cu[:, :LANES] + carry)
    if mask is not None:
        w = jnp.where(mask, w, 0.0)
    return _dot(w.astype(BF16), v_bf16), cu[:, LANES:]


def _sb_prompt_kernel(bias_ref, q_ref, k_ref, v_ref, o_ref, acc_scr, carry_scr, *, scale, tq):
    h = pl.program_id(1)
    qi = pl.program_id(2)
    q = q_ref[...].astype(BF16)
    bias = bias_ref[h]
    uo = _cumsum_weights()
    acc_scr[...] = jnp.zeros_like(acc_scr)
    carry_scr[...] = jnp.zeros_like(carry_scr)
    q_pos = qi * tq + lax.broadcasted_iota(jnp.int32, (tq, LANES), 0)
    k_off = lax.broadcasted_iota(jnp.int32, (tq, LANES), 1)
    n_blocks = (qi + 1) * (tq // LANES)

    def body(it, _):
        ks = pl.multiple_of((n_blocks - 1 - it) * LANES, LANES)
        kb = k_ref[pl.ds(ks, LANES), :].astype(BF16)
        vb = v_ref[pl.ds(ks, LANES), :].astype(BF16)
        z = _dot_nt(q, kb) * scale + bias
        mask = (k_off + ks) < q_pos
        pv, tot = _sb_block(z, mask, carry_scr[...], vb, uo)
        acc_scr[...] += pv
        carry_scr[...] += tot
        return 0

    lax.fori_loop(0, n_blocks, body, 0)
    o_ref[...] = acc_scr[...].astype(o_ref.dtype)


def sb_attend_prompt(qkv, bias, n_heads):
    b, t, _ = qkv.shape
    dh = LANES
    tq = min(t, SB_Q_TILE)
    kern = functools.partial(_sb_prompt_kernel, scale=dh ** -0.5, tq=tq)
    return pl.pallas_call(
        kern,
        grid=(b, n_heads, t // tq),
        in_specs=[pl.BlockSpec(memory_space=pltpu.SMEM),
                  pl.BlockSpec((None, tq, dh), lambda bi, h, qi: (bi, qi, h)),
                  pl.BlockSpec((None, t, dh), lambda bi, h, qi: (bi, 0, n_heads + h)),
                  pl.BlockSpec((None, t, dh), lambda bi, h, qi: (bi, 0, 2 * n_heads + h))],
        out_specs=pl.BlockSpec((None, tq, dh), lambda bi, h, qi: (bi, qi, h)),
        out_shape=jax.ShapeDtypeStruct((b, t, n_heads * dh), BF16),
        scratch_shapes=[pltpu.VMEM((tq, dh), F32), pltpu.VMEM((tq, LANES), F32)],
        compiler_params=_cparams("parallel", "parallel", "arbitrary"),
        name="sb_attend_prompt",
    )(bias, qkv, qkv, qkv)


def _sb_sample_kernel(pt_ref, q_ref, bias_ref, kn_ref, vn_ref, *rest, scale, n_heads, n_new, pages_per_step):
    k_refs = rest[:pages_per_step]
    v_refs = rest[pages_per_step:2 * pages_per_step]
    o_ref, qbd_scr, acc_scr, carry_scr = rest[2 * pages_per_step:]
    s = pl.program_id(1)
    d = q_ref.shape[1]
    uo = _cumsum_weights()
    col = lax.broadcasted_iota(jnp.int32, (n_heads, d), 1)
    head_lo = lax.broadcasted_iota(jnp.int32, (n_heads, d), 0) * LANES
    own = (col >= head_lo) & (col < head_lo + LANES)

    def process(k_blk, v_blk, mask):
        z = _dot_nt(qbd_scr[...], k_blk.astype(BF16)) * scale + bias_ref[...]
        pv, tot = _sb_block(z, mask, carry_scr[...], v_blk.astype(BF16), uo)
        acc_scr[...] += pv
        carry_scr[...] += tot

    @pl.when(s == 0)
    def _():
        q = q_ref[...]
        for i in range(n_new):
            row = jnp.broadcast_to(q[i:i + 1, :], (n_heads, d))
            qbd_scr[i * n_heads:(i + 1) * n_heads, :] = jnp.where(own, row, 0.0).astype(BF16)
        acc_scr[...] = jnp.zeros_like(acc_scr)
        carry_scr[...] = jnp.zeros_like(carry_scr)
        rows = n_new * n_heads
        row = lax.broadcasted_iota(jnp.int32, (rows, LANES), 0)
        k_tok = lax.broadcasted_iota(jnp.int32, (rows, LANES), 1)
        process(kn_ref[...], vn_ref[...], (k_tok + 1) * n_heads <= row)

    for j in range(pages_per_step):
        process(k_refs[j][...], v_refs[j][...], None)

    @pl.when(s == pl.num_programs(1) - 1)
    def _():
        for i in range(n_new):
            part = acc_scr[i * n_heads:(i + 1) * n_heads, :]
            o_ref[i:i + 1, :] = jnp.sum(jnp.where(own, part, 0.0), axis=0, keepdims=True)


def sb_attend_sample(q, k_new, v_new, cache_k, cache_v, page_table, bias, layer, n_heads):
    n_seq, n_new, d = q.shape
    n_pages = page_table.shape[1]
    page = cache_k.shape[2]
    pps = SB_PAGES_PER_STEP
    rows = n_new * n_heads
    pad = ((0, 0), (0, page - n_new), (0, 0))
    k_pad = jnp.pad(k_new, pad)
    v_pad = jnp.pad(v_new, pad)
    bias_rows = jnp.tile(bias.astype(F32), n_new).reshape(rows, 1)

    def page_spec(j):
        return pl.BlockSpec((None, None, page, d),
                            lambda b, s, pt: (layer, pt[b, n_pages - 1 - (s * pps + j)], 0, 0))

    seq_spec = pl.BlockSpec((None, n_new, d), lambda b, s, pt: (b, 0, 0))
    new_spec = pl.BlockSpec((None, page, d), lambda b, s, pt: (b, 0, 0))
    kern = functools.partial(_sb_sample_kernel, scale=LANES ** -0.5, n_heads=n_heads, n_new=n_new,
                             pages_per_step=pps)
    grid_spec = pltpu.PrefetchScalarGridSpec(
        num_scalar_prefetch=1,
        grid=(n_seq, n_pages // pps),
        in_specs=[seq_spec, pl.BlockSpec((rows, 1), lambda b, s, pt: (0, 0)), new_spec, new_spec]
                 + [page_spec(j) for j in range(pps)] * 2,
        out_specs=seq_spec,
        scratch_shapes=[pltpu.VMEM((rows, d), BF16), pltpu.VMEM((rows, d), F32),
                        pltpu.VMEM((rows, LANES), F32)],
    )
    return pl.pallas_call(
        kern,
        grid_spec=grid_spec,
        out_shape=jax.ShapeDtypeStruct((n_seq, n_new, d), F32),
        compiler_params=_cparams("parallel", "arbitrary"),
        name="sb_attend_sample",
    )(page_table, q, bias_rows, k_pad, v_pad, *([cache_k] * pps), *([cache_v] * pps))


def _ml_gates_kernel(u_ref, wg_ref, b_ref, o_ref, *, n_heads):
    pre = _dot_nt(wg_ref[...].astype(BF16), u_ref[...]) + b_ref[...]
    pre = GATE_SOFTCAP * jnp.tanh(pre / GATE_SOFTCAP)
    o_ref[0:n_heads, :] = pre[0:n_heads]
    o_ref[n_heads:, :] = -_softplus(-pre[n_heads:])


def ml_gates(u, wg_t, b_gate):
    m, d = u.shape
    g = wg_t.shape[0]
    tm = min(m, ROW_TILE)
    return pl.pallas_call(
        functools.partial(_ml_gates_kernel, n_heads=g // 2),
        grid=(m // tm,),
        in_specs=[pl.BlockSpec((tm, d), lambda i: (i, 0)),
                  pl.BlockSpec((g, d), lambda i: (0, 0)),
                  pl.BlockSpec((g, 1), lambda i: (0, 0))],
        out_specs=pl.BlockSpec((g, tm), lambda i: (0, i)),
        out_shape=jax.ShapeDtypeStruct((g, m), F32),
        compiler_params=_cparams("parallel"),
        name="ml_gates",
    )(u, wg_t, b_gate)


def _lane_cumsum(x):
    lane = lax.broadcasted_iota(jnp.int32, x.shape, 1)
    shift = 1
    while shift < x.shape[1]:
        x = x + jnp.where(lane >= shift, pltpu.roll(x, shift, axis=1), 0.0)
        shift *= 2
    return x


def _mlstm_kernel(g_ref, q_ref, k_ref, v_ref, og_ref, gn_ref, c0_ref, n0_ref, m0_ref,
                  y_ref, c_out, n_out, m_out, c_scr, n_scr, m_scr, *, n_valid, k_scale):
    ci = pl.program_id(2)
    L = q_ref.shape[0]

    @pl.when(ci == 0)
    def _():
        c_scr[...] = c0_ref[...]
        n_scr[...] = n0_ref[...]
        m_scr[...] = m0_ref[...]

    li_r = g_ref[0:1, :]
    lf_r = g_ref[1:2, :]
    if n_valid < L:
        valid = lax.broadcasted_iota(jnp.int32, (1, L), 1) < n_valid
        li_r = jnp.where(valid, li_r, NEG_BIG)
        lf_r = jnp.where(valid, lf_r, 0.0)
    bc_r = _lane_cumsum(jnp.broadcast_to(lf_r, (SUBLANES, L)))[0:1]
    r_idx = lax.broadcasted_iota(jnp.int32, (L, L), 0)
    rows = jnp.where(r_idx == 0, bc_r, jnp.where(r_idx == 1, li_r, 0.0))
    cols = rows.T
    bc_c = cols[:, 0:1]
    li_c = cols[:, 1:2]
    m_prev = m_scr[0:1, 0:1]

    t_idx = lax.broadcasted_iota(jnp.int32, (L, L), 0)
    s_idx = lax.broadcasted_iota(jnp.int32, (L, L), 1)
    dmat = jnp.where(s_idx <= t_idx, bc_c + (li_r - bc_r), NEG_BIG)
    inter = bc_c + m_prev
    m_t = jnp.maximum(inter, jnp.max(dmat, axis=1, keepdims=True))

    qf = q_ref[...]
    q = qf.astype(BF16)
    ks_f = k_ref[...] * k_scale
    v = v_ref[...].astype(BF16)
    a = jnp.exp(dmat - m_t) * _dot_nt(q, ks_f.astype(BF16))
    sc = jnp.exp(inter - m_t)
    c = c_scr[...]
    n_row = n_scr[...]
    num = _dot(a.astype(BF16), v) + sc * _dot(q, c.astype(BF16))
    den = jnp.sum(a, axis=1, keepdims=True) + sc * jnp.sum(qf * n_row, axis=1, keepdims=True)
    hval = num / jnp.maximum(jnp.abs(den), jnp.exp(-m_t))
    y_ref[...] = (jax.nn.sigmoid(og_ref[...]) * _rms(hval, gn_ref[...])).astype(y_ref.dtype)

    m_new = m_t[L - 1:L, :]
    bc_last = bc_c[L - 1:L, :]
    wl = jnp.exp(bc_last - bc_c + li_c - m_new)
    sl = jnp.exp(bc_last + m_prev - m_new)
    kw = ks_f * wl
    c_scr[...] = sl * c + _dot(kw.T.astype(BF16), v)
    n_scr[...] = sl * n_row + jnp.sum(kw, axis=0, keepdims=True)
    m_scr[...] = jnp.broadcast_to(m_new, m_scr.shape)

    @pl.when(ci == pl.num_programs(2) - 1)
    def _():
        c_out[...] = c_scr[...]
        n_out[...] = n_scr[...]
        m_out[...] = m_scr[...]


def mlstm_scan(proj, gates, out_gain, c0, n0, m0, n_heads, n_valid):
    b, t, _ = proj.shape
    L = CHUNK
    dk = c0.shape[2]
    dv = c0.shape[3]
    qk_w = n_heads * dk
    kern = functools.partial(_mlstm_kernel, n_valid=n_valid, k_scale=dk ** -0.5)
    state_specs = [pl.BlockSpec((None, None, dk, dv), lambda bi, h, ci: (bi, h, 0, 0)),
                   pl.BlockSpec((None, None, 1, dk), lambda bi, h, ci: (bi, h, 0, 0)),
                   pl.BlockSpec((None, None, 1, LANES), lambda bi, h, ci: (bi, h, 0, 0))]
    v_off = 2 * qk_w // dv
    o_off = v_off + n_heads
    return pl.pallas_call(
        kern,
        grid=(b, n_heads, t // L),
        in_specs=[pl.BlockSpec((None, None, None, 2, L), lambda bi, h, ci: (bi, ci, h, 0, 0)),
                  pl.BlockSpec((None, L, dk), lambda bi, h, ci: (bi, ci, h)),
                  pl.BlockSpec((None, L, dk), lambda bi, h, ci: (bi, ci, n_heads + h)),
                  pl.BlockSpec((None, L, dv), lambda bi, h, ci: (bi, ci, v_off + h)),
                  pl.BlockSpec((None, L, dv), lambda bi, h, ci: (bi, ci, o_off + h)),
                  pl.BlockSpec((1, dv), lambda bi, h, ci: (0, 0))] + state_specs,
        out_specs=[pl.BlockSpec((None, L, dv), lambda bi, h, ci: (bi, ci, h))] + state_specs,
        out_shape=[jax.ShapeDtypeStruct((b, t, n_heads * dv), BF16),
                   jax.ShapeDtypeStruct(c0.shape, F32),
                   jax.ShapeDtypeStruct(n0.shape, F32),
                   jax.ShapeDtypeStruct(m0.shape, F32)],
        scratch_shapes=[pltpu.VMEM((dk, dv), F32), pltpu.VMEM((1, dk), F32), pltpu.VMEM((1, LANES), F32)],
        compiler_params=_cparams("parallel", "parallel", "arbitrary"),
        name="mlstm_scan",
    )(gates, proj, proj, proj, proj, out_gain, c0, n0, m0)


def _hgrn_tables(c):
    n_lev = int(np.log2(c))
    lev = np.zeros(((n_lev + 1) * c, c), np.float32)
    lev[:c] = np.tril(np.ones((c, c), np.float32))
    for l in range(1, n_lev + 1):
        m = 1 << (l - 1)
        for t in range(c):
            mid = (t >> l << l) + m
            if t & m:
                lev[l * c + t, mid:t + 1] = 1.0
            else:
                lev[l * c + t, t + 1:mid] = 1.0
    t = np.arange(c)[:, None]
    s = np.arange(c)[None, :]
    x = t ^ s
    pair_level = np.where(s > t, -1, np.where(s == t, 0, np.floor(np.log2(np.maximum(x, 1))).astype(np.int32) + 1))
    return jnp.asarray(lev, BF16), jnp.asarray(pair_level, jnp.int32)


def _hgrn_kernel(lev_ref, lvl_ref, q_ref, f_ref, i_ref, g_ref, lb_ref, gn_ref, s0_ref,
                 y_ref, s_out, st_scr, *, n_valid):
    ci = pl.program_id(2)
    c = q_ref.shape[0]
    n_lev = lev_ref.shape[0] // c - 1

    @pl.when(ci == 0)
    def _():
        st_scr[...] = s0_ref[...].T

    fp = f_ref[...]
    lb = lb_ref[...]
    log_sig = -_softplus(-fp)
    la = jnp.log(lb)
    lbb = jnp.log1p(-lb) + log_sig
    lf = jnp.maximum(la, lbb) + jnp.log1p(jnp.exp(-jnp.abs(la - lbb)))
    kk = (1.0 - lb) * jax.nn.sigmoid(-fp)
    if n_valid < c:
        valid = lax.broadcasted_iota(jnp.int32, (c, 1), 0) < n_valid
        lf = jnp.where(valid, lf, 0.0)
        kk = jnp.where(valid, kk, 0.0)

    hi, lo = _split_bf16(lf)
    lev = lev_ref[...]
    d_all = _dot(lev, hi) + _dot(lev, lo)
    bcum = d_all[0:c]
    q = q_ref[...]
    iv_f = i_ref[...]
    iv = iv_f.astype(BF16)
    lvl = lvl_ref[...]
    a = jnp.where(lvl == 0, _dot_nt(q.astype(BF16), kk.astype(BF16)), 0.0)
    for l in range(1, n_lev + 1):
        e = jnp.exp(d_all[l * c:(l + 1) * c])
        a = a + jnp.where(lvl == l, _dot_nt((q * e).astype(BF16), (kk * e).astype(BF16)), 0.0)
    st = st_scr[...]
    o = _dot(a.astype(BF16), iv) + _dot_nt((q * jnp.exp(bcum)).astype(BF16), st.astype(BF16))
    y_ref[...] = (jax.nn.sigmoid(g_ref[...]) * _rms(o, gn_ref[...])).astype(y_ref.dtype)

    blast = bcum[c - 1:c, :]
    ke = kk * jnp.exp(blast - bcum)
    st_scr[...] = jnp.exp(blast) * st + _dot(iv_f.T.astype(BF16), ke.astype(BF16))

    @pl.when(ci == pl.num_programs(2) - 1)
    def _():
        s_out[...] = st_scr[...].T


def hgrn_scan(proj, lower_bound, out_gain, s0, n_heads, n_valid):
    b, t, _ = proj.shape
    c = CHUNK
    dh = s0.shape[2]
    lev, pair_level = _hgrn_tables(c)
    kern = functools.partial(_hgrn_kernel, n_valid=n_valid)

    def part(p):
        return pl.BlockSpec((None, c, dh), lambda bi, h, ci: (bi, ci, p * n_heads + h))

    state_spec = pl.BlockSpec((None, None, dh, dh), lambda bi, h, ci: (bi, h, 0, 0))
    return pl.pallas_call(
        kern,
        grid=(b, n_heads, t // c),
        in_specs=[pl.BlockSpec(lev.shape, lambda bi, h, ci: (0, 0)),
                  pl.BlockSpec(pair_level.shape, lambda bi, h, ci: (0, 0)),
                  part(0), part(1), part(2), part(3),
                  pl.BlockSpec((1, dh), lambda bi, h, ci: (0, h)),
                  pl.BlockSpec((1, dh), lambda bi, h, ci: (0, 0)),
                  state_spec],
        out_specs=[pl.BlockSpec((None, c, dh), lambda bi, h, ci: (bi, ci, h)), state_spec],
        out_shape=[jax.ShapeDtypeStruct((b, t, n_heads * dh), BF16),
                   jax.ShapeDtypeStruct(s0.shape, F32)],
        scratch_shapes=[pltpu.VMEM((dh, dh), F32)],
        compiler_params=_cparams("parallel", "parallel", "arbitrary"),
        name="hgrn_scan",
    )(lev, pair_level, proj, proj, proj, proj, lower_bound, out_gain, s0)


def _pad_tokens(x, t_pad):
    return jnp.pad(x, ((0, 0), (0, t_pad - x.shape[1])) + ((0, 0),) * (x.ndim - 2))


def kernel(x_prompt, x_sample, cache_k, cache_v, state_mlstm_c, state_mlstm_n, state_mlstm_m,
           state_hgrn, page_table, p_prompt, p_sample, norm_mix, norm_ffn, norm_ple,
           sb_w_qkv, sb_q_norm, sb_k_norm, sb_logit_bias, sb_w_out, ml_w_in, ml_b_gate,
           ml_out_norm, ml_w_out, hg_w_in, hg_lb_logits, hg_out_norm, hg_w_out,
           ffn_w_up, ffn_w_down, ple_w_proj, ple_w_gate):
    depth, d = norm_mix.shape
    bp, tp, _ = x_prompt.shape
    bs, ts, _ = x_sample.shape
    sb_heads = sb_logit_bias.shape[1]
    ml_heads = state_mlstm_c.shape[2]
    ml_dk = state_mlstm_c.shape[3]
    ml_dv = state_mlstm_c.shape[4]
    hg_heads = state_hgrn.shape[2]
    hg_dh = state_hgrn.shape[3]
    n_attn, n_pool, page = cache_k.shape[:3]
    ml_qk_w = ml_heads * ml_dk

    g_mix = norm_mix.reshape(depth, 1, d)
    g_ffn = norm_ffn.reshape(depth, 1, d)
    g_ple = norm_ple.reshape(depth, 1, d)
    cache_k2 = cache_k.reshape(n_attn, n_pool, page, d)
    cache_v2 = cache_v.reshape(n_attn, n_pool, page, d)
    lb_soft = jax.nn.softmax(hg_lb_logits.astype(F32), axis=0)
    lower_bounds = jnp.cumsum(lb_soft, axis=0) - lb_soft

    hp = x_prompt.reshape(bp * tp, d)
    hs = x_sample.reshape(bs * ts, d)
    pp_all = p_prompt.reshape(depth, bp * tp, -1)
    ps_all = p_sample.reshape(depth, bs * ts, -1)

    outs = {k: [] for k in ("kp", "vp", "ks", "vs", "mcp", "mnp", "mmp", "mcs", "mns", "mms", "hgp", "hgs")}
    i_attn = i_ml = i_hg = 0
    for layer in range(depth):
        kind = layer % N_MIXERS
        up = rmsnorm_bf16(hp, g_mix, layer)
        us = rmsnorm_bf16(hs, g_mix, layer)
        if kind == 0:
            j = i_attn
            i_attn += 1
            gain = jnp.concatenate([jnp.tile(sb_q_norm[j], sb_heads), jnp.tile(sb_k_norm[j], sb_heads)])[None, :]
            qkv_p = matmul_headnorm(up, sb_w_qkv, j, 3 * d, gain)
            qkv_s = matmul_headnorm(us, sb_w_qkv, j, 3 * d, gain)
            outs["kp"].append(qkv_p[:, d:2 * d].reshape(bp, tp, sb_heads, LANES))
            outs["vp"].append(qkv_p[:, 2 * d:].reshape(bp, tp, sb_heads, LANES))
            qkv_s3 = qkv_s.reshape(bs, ts, 3 * d)
            k_new = qkv_s3[:, :, d:2 * d]
            v_new = qkv_s3[:, :, 2 * d:]
            outs["ks"].append(k_new.reshape(bs, ts, sb_heads, LANES))
            outs["vs"].append(v_new.reshape(bs, ts, sb_heads, LANES))
            ap = sb_attend_prompt(qkv_p.reshape(bp, tp, 3 * d), sb_logit_bias[j], sb_heads).reshape(bp * tp, d)
            a_s = sb_attend_sample(qkv_s3[:, :, :d], k_new, v_new, cache_k2, cache_v2, page_table,
                                   sb_logit_bias[j], j, sb_heads).reshape(bs * ts, d)
            hp = matmul_residual(ap, sb_w_out, j, hp)
            hs = matmul_residual(a_s, sb_w_out, j, hs)
        elif kind == 1:
            j = i_ml
            i_ml += 1
            wg_t = ml_w_in[j][:, 2 * ml_qk_w + 2 * d:].T
            b_gate = ml_b_gate[j].reshape(-1, 1)
            gn = ml_out_norm[j].reshape(1, ml_dv)
            n_main = 2 * ml_qk_w + 2 * d

            def gate_layout(g, b, t):
                g = g.reshape(2, ml_heads, b, t)
                t_pad = -(-t // CHUNK) * CHUNK
                g = jnp.pad(g, ((0, 0), (0, 0), (0, 0), (0, t_pad - t)))
                return g.reshape(2, ml_heads, b, t_pad // CHUNK, CHUNK).transpose(2, 3, 1, 0, 4)

            proj_p = matmul_plain(up, ml_w_in, j, n_main).reshape(bp, tp, n_main)
            proj_s = matmul_plain(us, ml_w_in, j, n_main).reshape(bs, ts, n_main)
            gates_p = gate_layout(ml_gates(up, wg_t, b_gate), bp, tp)
            gates_s = gate_layout(ml_gates(us, wg_t, b_gate), bs, ts)
            zc = jnp.zeros((bp, ml_heads, ml_dk, ml_dv), F32)
            zn = jnp.zeros((bp, ml_heads, 1, ml_dk), F32)
            zm = jnp.zeros((bp, ml_heads, 1, LANES), F32)
            yp, cp, npv, mp = mlstm_scan(proj_p, gates_p, gn, zc, zn, zm, ml_heads, CHUNK)
            m0 = jnp.broadcast_to(state_mlstm_m[j][:, :, None, None], (bs, ml_heads, 1, LANES))
            ys, cs, nsv, ms = mlstm_scan(_pad_tokens(proj_s, CHUNK), gates_s, gn, state_mlstm_c[j],
                                         state_mlstm_n[j][:, :, None, :], m0, ml_heads, ts)
            outs["mcp"].append(cp)
            outs["mnp"].append(npv[:, :, 0, :])
            outs["mmp"].append(mp[:, :, 0, 0])
            outs["mcs"].append(cs)
            outs["mns"].append(nsv[:, :, 0, :])
            outs["mms"].append(ms[:, :, 0, 0])
            hp = matmul_residual(yp.reshape(bp * tp, d), ml_w_out, j, hp)
            hs = matmul_residual(ys[:, :ts].reshape(bs * ts, d), ml_w_out, j, hs)
        else:
            j = i_hg
            i_hg += 1
            lb = lower_bounds[layer].reshape(1, d)
            gn = hg_out_norm[j].reshape(1, hg_dh)
            proj_p = matmul_plain(up, hg_w_in, j, 4 * d).reshape(bp, tp, 4 * d)
            proj_s = matmul_plain(us, hg_w_in, j, 4 * d).reshape(bs, ts, 4 * d)
            zs = jnp.zeros((bp, hg_heads, hg_dh, hg_dh), F32)
            yp, sp = hgrn_scan(proj_p, lb, gn, zs, hg_heads, CHUNK)
            ys, ss = hgrn_scan(_pad_tokens(proj_s, CHUNK), lb, gn, state_hgrn[j], hg_heads, ts)
            outs["hgp"].append(sp)
            outs["hgs"].append(ss)
            hp = matmul_residual(yp.reshape(bp * tp, d), hg_w_out, j, hp)
            hs = matmul_residual(ys[:, :ts].reshape(bs * ts, d), hg_w_out, j, hs)
        hp = ffn(hp, g_ffn, ffn_w_up, ffn_w_down, layer)
        hs = ffn(hs, g_ffn, ffn_w_up, ffn_w_down, layer)
        hp = ple(hp, rmsnorm_bf16(hp, g_ple, layer), pp_all, ple_w_gate, ple_w_proj, layer)
        hs = ple(hs, rmsnorm_bf16(hs, g_ple, layer), ps_all, ple_w_gate, ple_w_proj, layer)

    st = {k: jnp.stack(v) for k, v in outs.items()}
    return (hp.reshape(bp, tp, d), hs.reshape(bs, ts, d), st["kp"], st["vp"], st["ks"], st["vs"],
            st["mcp"], st["mnp"], st["mmp"], st["mcs"], st["mns"], st["mms"], st["hgp"], st["hgs"])
```

```python
import functools

import numpy as np
import jax
import jax.numpy as jnp
from jax import lax
from jax.experimental import pallas as pl
from jax.experimental.pallas import tpu as pltpu

F32 = jnp.float32
BF16 = jnp.bfloat16

NORM_EPS = 1e-6
N_MIXERS = 3
GATE_SOFTCAP = 15.0
LANES = 128
SUBLANES = 8
VMEM_LIMIT_BYTES = 56 * 1024 * 1024
ROW_TILE = 2048
COL_TILE = 512
ACC_ROW_TILE = 1024
ACC_COL_TILE = 1024
ACC_K_TILE = 2048
SB_Q_TILE = 512
SB_K_TILE = 512
SB_PAGES_PER_STEP = 4
CHUNK = 128
ML_HEADS_PER_STEP = 4
HG_HEADS_PER_STEP = 4
NEG_BIG = -1e30
LOG2_E = 1.4426950408889634


def _cparams(*sem):
    return pltpu.CompilerParams(dimension_semantics=sem, vmem_limit_bytes=VMEM_LIMIT_BYTES)


def _dot(a, b):
    return jnp.dot(a, b, preferred_element_type=F32)


def _dot_nt(a, b):
    return lax.dot_general(a, b, (((1,), (1,)), ((), ())), preferred_element_type=F32)


def _split_bf16(x):
    hi = x.astype(BF16)
    lo = (x - hi.astype(F32)).astype(BF16)
    return hi, lo


def _softplus(z):
    return jnp.maximum(z, 0.0) + jnp.log1p(jnp.exp(-jnp.abs(z)))


def _rms(x, gain):
    ms = jnp.mean(x * x, axis=-1, keepdims=True)
    return x * lax.rsqrt(ms + NORM_EPS) * gain


def _rmsnorm_kernel(x_ref, g_ref, o_ref):
    o_ref[...] = _rms(x_ref[...], g_ref[...]).astype(o_ref.dtype)


def rmsnorm_bf16(x, gains, layer):
    m, d = x.shape
    tm = min(m, 512)
    return pl.pallas_call(
        _rmsnorm_kernel,
        grid=(m // tm,),
        in_specs=[pl.BlockSpec((tm, d), lambda i: (i, 0)),
                  pl.BlockSpec((None, 1, d), lambda i: (layer, 0, 0))],
        out_specs=pl.BlockSpec((tm, d), lambda i: (i, 0)),
        out_shape=jax.ShapeDtypeStruct((m, d), BF16),
        compiler_params=_cparams("parallel"),
        name="rmsnorm",
    )(x, gains)


def _mm_headnorm_kernel(x_ref, w_ref, gain_ref, o_ref):
    acc = _dot(x_ref[...].astype(BF16), w_ref[...].astype(BF16))
    for g in range(acc.shape[1] // LANES):
        sl = slice(g * LANES, (g + 1) * LANES)
        o_ref[:, sl] = _rms(acc[:, sl], gain_ref[...])


def _mm_plain_kernel(x_ref, w_ref, o_ref):
    o_ref[...] = _dot(x_ref[...].astype(BF16), w_ref[...].astype(BF16))


def matmul_cols(x, w_all, layer, col_start, n_cols, head_gain=None):
    m, k = x.shape
    tm = min(m, ROW_TILE)
    tn = COL_TILE
    off = col_start // tn
    in_specs = [pl.BlockSpec((tm, k), lambda i, j: (i, 0)),
                pl.BlockSpec((None, k, tn), lambda i, j: (layer, 0, off + j))]
    args = (x, w_all)
    kern = _mm_plain_kernel
    if head_gain is not None:
        in_specs.append(pl.BlockSpec((1, LANES), lambda i, j: (0, 0)))
        args += (head_gain,)
        kern = _mm_headnorm_kernel
    return pl.pallas_call(
        kern,
        grid=(m // tm, n_cols // tn),
        in_specs=in_specs,
        out_specs=pl.BlockSpec((tm, tn), lambda i, j: (i, j)),
        out_shape=jax.ShapeDtypeStruct((m, n_cols), F32),
        compiler_params=_cparams("parallel", "arbitrary"),
        name="matmul_cols",
    )(*args)


def _mm_res_kernel(x_ref, w_ref, r_ref, o_ref):
    o_ref[...] = r_ref[...] + _dot(x_ref[...].astype(BF16), w_ref[...].astype(BF16))


def matmul_residual(x, w_all, layer, res):
    m, k = x.shape
    n = res.shape[1]
    tm = min(m, ROW_TILE)
    tn = COL_TILE
    return pl.pallas_call(
        _mm_res_kernel,
        grid=(m // tm, n // tn),
        in_specs=[pl.BlockSpec((tm, k), lambda i, j: (i, 0)),
                  pl.BlockSpec((None, k, tn), lambda i, j: (layer, 0, j)),
                  pl.BlockSpec((tm, tn), lambda i, j: (i, j))],
        out_specs=pl.BlockSpec((tm, tn), lambda i, j: (i, j)),
        out_shape=jax.ShapeDtypeStruct((m, n), F32),
        compiler_params=_cparams("parallel", "arbitrary"),
        name="matmul_residual",
    )(x, w_all, res)


def _ffn_up_kernel(x_ref, w_ref, o_ref):
    a = jnp.maximum(_dot(x_ref[...], w_ref[...].astype(BF16)), 0.0)
    o_ref[...] = (a * a).astype(o_ref.dtype)


def ffn_up(u, w_up, layer):
    m, d = u.shape
    d_ff = w_up.shape[2]
    tm = min(m, ROW_TILE)
    tn = COL_TILE
    return pl.pallas_call(
        _ffn_up_kernel,
        grid=(m // tm, d_ff // tn),
        in_specs=[pl.BlockSpec((tm, d), lambda i, j: (i, 0)),
                  pl.BlockSpec((None, d, tn), lambda i, j: (layer, 0, j))],
        out_specs=pl.BlockSpec((tm, tn), lambda i, j: (i, j)),
        out_shape=jax.ShapeDtypeStruct((m, d_ff), BF16),
        compiler_params=_cparams("parallel", "arbitrary"),
        name="ffn_up",
    )(u, w_up)


def _mm_acc_kernel(x_ref, w_ref, r_ref, o_ref):
    @pl.when(pl.program_id(2) == 0)
    def _():
        o_ref[...] = r_ref[...]

    o_ref[...] += _dot(x_ref[...], w_ref[...].astype(BF16))


def matmul_residual_acc(x, w_all, layer, res):
    m, k = x.shape
    n = res.shape[1]
    tm = min(m, ACC_ROW_TILE)
    tn = ACC_COL_TILE
    tk = ACC_K_TILE
    return pl.pallas_call(
        _mm_acc_kernel,
        grid=(m // tm, n // tn, k // tk),
        in_specs=[pl.BlockSpec((tm, tk), lambda i, j, kk: (i, kk)),
                  pl.BlockSpec((None, tk, tn), lambda i, j, kk: (layer, kk, j)),
                  pl.BlockSpec((tm, tn), lambda i, j, kk: (i, j))],
        out_specs=pl.BlockSpec((tm, tn), lambda i, j, kk: (i, j)),
        out_shape=jax.ShapeDtypeStruct((m, n), F32),
        compiler_params=_cparams("parallel", "parallel", "arbitrary"),
        name="ffn_down",
    )(x, w_all, res)


def _ple_kernel(u_ref, wg_ref, p_ref, wp_ref, h_ref, o_ref):
    gate = jax.nn.sigmoid(_dot(u_ref[...], wg_ref[...].astype(BF16)))
    pp = _dot(p_ref[...].astype(BF16), wp_ref[...].astype(BF16))
    o_ref[...] = h_ref[...] + gate * pp


def ple(h, u, p_all, w_gate, w_proj, layer):
    m, d = h.shape
    dp = p_all.shape[2]
    tm = min(m, ACC_ROW_TILE)
    tn = COL_TILE
    return pl.pallas_call(
        _ple_kernel,
        grid=(m // tm, d // tn),
        in_specs=[pl.BlockSpec((tm, d), lambda i, j: (i, 0)),
                  pl.BlockSpec((None, d, tn), lambda i, j: (layer, 0, j)),
                  pl.BlockSpec((None, tm, dp), lambda i, j: (layer, i, 0)),
                  pl.BlockSpec((None, dp, tn), lambda i, j: (layer, 0, j)),
                  pl.BlockSpec((tm, tn), lambda i, j: (i, j))],
        out_specs=pl.BlockSpec((tm, tn), lambda i, j: (i, j)),
        out_shape=jax.ShapeDtypeStruct((m, d), F32),
        compiler_params=_cparams("parallel", "arbitrary"),
        name="ple",
    )(u, w_gate, p_all, w_proj, h)


def _cumsum_weights():
    row = lax.broadcasted_iota(jnp.int32, (2 * LANES, 2 * LANES), 0) & (LANES - 1)
    col = lax.broadcasted_iota(jnp.int32, (2 * LANES, 2 * LANES), 1)
    return jnp.where((col >= LANES) | (row > col), 1.0, 0.0).astype(BF16)


def _log2_sigmoid_pair(zc):
    log_sig = jnp.minimum(zc, 0.0) - jnp.log2(1.0 + jnp.exp2(-jnp.abs(zc)))
    return log_sig, log_sig - zc


def _sb_prompt_kernel(bias_ref, q_ref, k_ref, v_ref, o_ref, acc_scr, carry_scr, *, scale, tq, tk):
    h = pl.program_id(1)
    qi = pl.program_id(2)
    q = q_ref[...].astype(BF16)
    bias = bias_ref[h] * LOG2_E
    uo = _cumsum_weights()
    acc_scr[...] = jnp.zeros_like(acc_scr)
    carry_scr[...] = jnp.zeros_like(carry_scr)
    n_diag = tq // tk
    n_blk = tk // LANES

    def tile(ks, masked):
        kb = k_ref[pl.ds(ks, tk), :].astype(BF16)
        vb = v_ref[pl.ds(ks, tk), :].astype(BF16)
        zc = _dot_nt(q, kb) * (scale * LOG2_E) + bias
        log_sig, log_keep = _log2_sigmoid_pair(zc)
        if masked:
            q_pos = qi * tq + lax.broadcasted_iota(jnp.int32, (tq, tk), 0)
            k_pos = ks + lax.broadcasted_iota(jnp.int32, (tq, tk), 1)
            mask = k_pos < q_pos
            log_keep = jnp.where(mask, log_keep, 0.0)
        hi, lo = _split_bf16(log_keep)
        carry = carry_scr[...]
        tails = [None] * n_blk
        for c in reversed(range(n_blk)):
            sl = slice(c * LANES, (c + 1) * LANES)
            cu = _dot(jnp.concatenate([hi[:, sl], lo[:, sl]], axis=1), uo)
            tails[c] = cu[:, :LANES] + carry
            carry = carry + cu[:, LANES:]
        w = jnp.exp2(log_sig + jnp.concatenate(tails, axis=1))
        if masked:
            w = jnp.where(mask, w, 0.0)
        acc_scr[...] += _dot(w.astype(BF16), vb)
        carry_scr[...] = carry

    for j in range(n_diag):
        tile(pl.multiple_of((qi * n_diag + n_diag - 1 - j) * tk, tk), True)

    def body(it, _):
        tile(pl.multiple_of((qi * n_diag - 1 - it) * tk, tk), False)
        return 0

    lax.fori_loop(0, qi * n_diag, body, 0)
    o_ref[...] = acc_scr[...].astype(o_ref.dtype)


def sb_attend_prompt(q, k, v, bias):
    b, t, _ = q.shape
    dh = LANES
    n_heads = bias.shape[0]
    tq = min(t, SB_Q_TILE)
    tk = min(tq, SB_K_TILE)
    kern = functools.partial(_sb_prompt_kernel, scale=dh ** -0.5, tq=tq, tk=tk)
    return pl.pallas_call(
        kern,
        grid=(b, n_heads, t // tq),
        in_specs=[pl.BlockSpec(memory_space=pltpu.SMEM),
                  pl.BlockSpec((None, tq, dh), lambda bi, h, qi: (bi, qi, h)),
                  pl.BlockSpec((None, t, dh), lambda bi, h, qi: (bi, 0, h)),
                  pl.BlockSpec((None, t, dh), lambda bi, h, qi: (bi, 0, h))],
        out_specs=pl.BlockSpec((None, tq, dh), lambda bi, h, qi: (bi, qi, h)),
        out_shape=jax.ShapeDtypeStruct((b, t, n_heads * dh), BF16),
        scratch_shapes=[pltpu.VMEM((tq, dh), F32), pltpu.VMEM((tq, LANES), F32)],
        compiler_params=_cparams("parallel", "parallel", "arbitrary"),
        name="sb_attend_prompt",
    )(bias, q, k, v)


def _suffix_sum_strided(x, stride):
    w = x.shape[1]
    lane = lax.broadcasted_iota(jnp.int32, x.shape, 1)
    s = stride
    while s < w:
        if s % LANES == 0:
            shifted = jnp.concatenate([x[:, s:], jnp.zeros((x.shape[0], s), F32)], axis=1)
        else:
            shifted = jnp.where(lane < w - s, pltpu.roll(x, w - s, axis=1), 0.0)
        x = x + shifted
        s *= 2
    return x


def _sb_sample_kernel(pt_ref, q_ref, bias_ref, kn_ref, vn_ref, *rest, scale, n_heads, n_new, pages_per_step):
    k_refs = rest[:pages_per_step]
    v_refs = rest[pages_per_step:2 * pages_per_step]
    o_ref, z_scr, w_scr, acc_scr, carry_scr = rest[2 * pages_per_step:]
    s = pl.program_id(1)
    width = kn_ref.shape[0]
    lane_h = lax.broadcasted_iota(jnp.int32, (n_heads, width), 1) & (n_heads - 1)
    same_head = lane_h == lax.broadcasted_iota(jnp.int32, (n_heads, width), 0)
    row = lax.broadcasted_iota(jnp.int32, (SUBLANES, width), 0)
    lane = lax.broadcasted_iota(jnp.int32, (SUBLANES, width), 1)
    is_query = row < n_new
    lane128 = lax.broadcasted_iota(jnp.int32, (SUBLANES, LANES), 1)
    q = q_ref[...].astype(BF16)

    def process(k_blk, v_blk, mask):
        zfull = _dot_nt(q, k_blk.astype(BF16))
        for i in range(n_new):
            part = zfull[i * n_heads:(i + 1) * n_heads, :]
            z_scr[i:i + 1, :] = jnp.sum(jnp.where(same_head, part, 0.0), axis=0, keepdims=True)
        zc = z_scr[...] * (scale * LOG2_E) + bias_ref[...] * LOG2_E
        log_sig, log_keep = _log2_sigmoid_pair(zc)
        log_keep = jnp.where(mask, log_keep, 0.0)
        incl = _suffix_sum_strided(log_keep, n_heads)
        w = jnp.where(mask, jnp.exp2(log_sig + (incl - log_keep) + carry_scr[...]), 0.0)
        for i in range(n_new):
            w_scr[i * n_heads:(i + 1) * n_heads, :] = jnp.where(
                same_head, jnp.broadcast_to(w[i:i + 1, :], (n_heads, width)), 0.0).astype(BF16)
        acc_scr[...] += _dot(w_scr[...], v_blk.astype(BF16))
        tot = jnp.where(lane128 < n_heads, incl[:, :LANES], 0.0)
        shift = n_heads
        while shift < LANES:
            tot = tot + pltpu.roll(tot, shift, axis=1)
            shift *= 2
        carry_scr[...] += jnp.concatenate([tot] * (width // LANES), axis=1)

    @pl.when(s == 0)
    def _():
        z_scr[...] = jnp.zeros_like(z_scr)
        acc_scr[...] = jnp.zeros_like(acc_scr)
        carry_scr[...] = jnp.zeros_like(carry_scr)
        process(kn_ref[...], vn_ref[...], is_query & (lane < row * n_heads))

    for j in range(pages_per_step):
        process(k_refs[j][...], v_refs[j][...], is_query)

    @pl.when(s == pl.num_programs(1) - 1)
    def _():
        o_ref[...] = acc_scr[...]


def sb_attend_sample(q, k_new, v_new, cache_k, cache_v, page_table, bias, layer):
    n_seq, n_new, n_heads, dh = q.shape
    n_pages = page_table.shape[1]
    n_attn, n_pool, page = cache_k.shape[:3]
    width = page * n_heads
    pps = SB_PAGES_PER_STEP
    rows = n_new * n_heads
    pad = ((0, 0), (0, page - n_new), (0, 0), (0, 0))
    k_pad = jnp.pad(k_new, pad).reshape(n_seq, width, dh)
    v_pad = jnp.pad(v_new, pad).reshape(n_seq, width, dh)
    ck = cache_k.reshape(n_attn, n_pool, width, dh)
    cv = cache_v.reshape(n_attn, n_pool, width, dh)
    bias_lanes = jnp.tile(bias.astype(F32), page).reshape(1, width)

    def page_spec(j):
        return pl.BlockSpec((None, None, width, dh),
                            lambda b, s, pt: (layer, pt[b, n_pages - 1 - (s * pps + j)], 0, 0))

    seq_spec = pl.BlockSpec((None, rows, dh), lambda b, s, pt: (b, 0, 0))
    new_spec = pl.BlockSpec((None, width, dh), lambda b, s, pt: (b, 0, 0))
    kern = functools.partial(_sb_sample_kernel, scale=dh ** -0.5, n_heads=n_heads, n_new=n_new,
                             pages_per_step=pps)
    grid_spec = pltpu.PrefetchScalarGridSpec(
        num_scalar_prefetch=1,
        grid=(n_seq, n_pages // pps),
        in_specs=[seq_spec, pl.BlockSpec((1, width), lambda b, s, pt: (0, 0)), new_spec, new_spec]
                 + [page_spec(j) for j in range(pps)] * 2,
        out_specs=seq_spec,
        scratch_shapes=[pltpu.VMEM((SUBLANES, width), F32), pltpu.VMEM((rows, width), BF16),
                        pltpu.VMEM((rows, dh), F32), pltpu.VMEM((SUBLANES, width), F32)],
    )
    out = pl.pallas_call(
        kern,
        grid_spec=grid_spec,
        out_shape=jax.ShapeDtypeStruct((n_seq, rows, dh), F32),
        compiler_params=_cparams("parallel", "arbitrary"),
        name="sb_attend_sample",
    )(page_table, q.reshape(n_seq, rows, dh), bias_lanes, k_pad, v_pad, *([ck] * pps), *([cv] * pps))
    return out.reshape(n_seq, n_new, n_heads * dh)


def _ml_gates_kernel(u_ref, wg_ref, b_ref, o_ref, *, n_heads):
    pre = _dot_nt(wg_ref[...].astype(BF16), u_ref[...]) + b_ref[...]
    pre = GATE_SOFTCAP * jnp.tanh(pre / GATE_SOFTCAP)
    o_ref[0:n_heads, :] = pre[0:n_heads]
    o_ref[n_heads:, :] = -_softplus(-pre[n_heads:])


def ml_gates(u, wg_t, b_gate):
    m, d = u.shape
    g = wg_t.shape[0]
    tm = min(m, ROW_TILE)
    return pl.pallas_call(
        functools.partial(_ml_gates_kernel, n_heads=g // 2),
        grid=(m // tm,),
        in_specs=[pl.BlockSpec((tm, d), lambda i: (i, 0)),
                  pl.BlockSpec((g, d), lambda i: (0, 0)),
                  pl.BlockSpec((g, 1), lambda i: (0, 0))],
        out_specs=pl.BlockSpec((g, tm), lambda i: (0, i)),
        out_shape=jax.ShapeDtypeStruct((g, m), F32),
        compiler_params=_cparams("parallel"),
        name="ml_gates",
    )(u, wg_t, b_gate)


def mlstm_gate_layout(g, b, t, n_heads):
    hps = ML_HEADS_PER_STEP
    t_pad = -(-t // CHUNK) * CHUNK
    g = jnp.pad(g.reshape(2, n_heads // hps, hps, b, t), ((0, 0),) * 4 + ((0, t_pad - t),))
    g = g.reshape(2, n_heads // hps, hps, b, t_pad // CHUNK, CHUNK).transpose(3, 4, 1, 2, 0, 5)
    return g.reshape(b, t_pad // CHUNK, n_heads // hps, 2 * hps, CHUNK)


def _lane_cumsum(x):
    lane = lax.broadcasted_iota(jnp.int32, x.shape, 1)
    shift = 1
    while shift < x.shape[1]:
        x = x + jnp.where(lane >= shift, pltpu.roll(x, shift, axis=1), 0.0)
        shift *= 2
    return x


def _mlstm_kernel(g_ref, q_ref, k_ref, v_ref, og_ref, gn_ref, c0_ref, n0_ref, m0_ref,
                  y_ref, c_out, n_out, m_out, c_scr, n_scr, m_scr, *, n_valid, k_scale):
    ci = pl.program_id(2)
    L = q_ref.shape[0]
    hps = c_scr.shape[0]
    dk = c_scr.shape[1]
    dv = c_scr.shape[2]

    @pl.when(ci == 0)
    def _():
        c_scr[...] = c0_ref[...]
        n_scr[...] = n0_ref[...]
        m_scr[...] = m0_ref[...]

    g = g_ref[...]
    row8 = lax.broadcasted_iota(jnp.int32, (2 * hps, L), 0)
    if n_valid < L:
        valid = lax.broadcasted_iota(jnp.int32, (2 * hps, L), 1) < n_valid
        g = jnp.where(valid, g, jnp.where((row8 & 1) == 0, NEG_BIG, 0.0))
    bc = _lane_cumsum(g)
    rows = jnp.concatenate([jnp.where((row8 & 1) == 0, g, bc), jnp.zeros((L - 2 * hps, L), F32)], axis=0)
    cols = rows.T
    t_idx = lax.broadcasted_iota(jnp.int32, (L, L), 0)
    s_idx = lax.broadcasted_iota(jnp.int32, (L, L), 1)
    causal = s_idx <= t_idx

    for hh in range(hps):
        li_r = g[2 * hh:2 * hh + 1, :]
        bc_r = bc[2 * hh + 1:2 * hh + 2, :]
        li_c = cols[:, 2 * hh:2 * hh + 1]
        bc_c = cols[:, 2 * hh + 1:2 * hh + 2]
        m_prev = m_scr[hh, 0:1, 0:1]
        dmat = jnp.where(causal, bc_c + (li_r - bc_r), NEG_BIG)
        inter = bc_c + m_prev
        m_t = jnp.maximum(inter, jnp.max(dmat, axis=1, keepdims=True))

        qf = q_ref[:, hh * dk:(hh + 1) * dk]
        q = qf.astype(BF16)
        ks_f = k_ref[:, hh * dk:(hh + 1) * dk] * k_scale
        v = v_ref[:, hh * dv:(hh + 1) * dv].astype(BF16)
        a = jnp.exp(dmat - m_t) * _dot_nt(q, ks_f.astype(BF16))
        sc = jnp.exp(inter - m_t)
        c = c_scr[hh]
        n_row = n_scr[hh]
        num = _dot(a.astype(BF16), v) + sc * _dot(q, c.astype(BF16))
        den = jnp.sum(a, axis=1, keepdims=True) + sc * jnp.sum(qf * n_row, axis=1, keepdims=True)
        hval = num / jnp.maximum(jnp.abs(den), jnp.exp(-m_t))
        gate = jax.nn.sigmoid(og_ref[:, hh * dv:(hh + 1) * dv])
        y_ref[:, hh * dv:(hh + 1) * dv] = (gate * _rms(hval, gn_ref[...])).astype(y_ref.dtype)

        m_new = m_t[L - 1:L, :]
        bc_last = bc_c[L - 1:L, :]
        wl = jnp.exp(bc_last - bc_c + li_c - m_new)
        sl = jnp.exp(bc_last + m_prev - m_new)
        kw = ks_f * wl
        c_scr[hh] = sl * c + _dot(kw.T.astype(BF16), v)
        n_scr[hh] = sl * n_row + jnp.sum(kw, axis=0, keepdims=True)
        m_scr[hh] = jnp.broadcast_to(m_new, (1, LANES))

    @pl.when(ci == pl.num_programs(2) - 1)
    def _():
        c_out[...] = c_scr[...]
        n_out[...] = n_scr[...]
        m_out[...] = m_scr[...]


def mlstm_scan(proj, gates, out_gain, c0, n0, m0, n_heads, n_valid):
    b, t, _ = proj.shape
    L = CHUNK
    dk = c0.shape[2]
    dv = c0.shape[3]
    hps = ML_HEADS_PER_STEP
    n_grp = n_heads // hps
    v_blk = 2 * n_heads * dk // (hps * dv)
    kern = functools.partial(_mlstm_kernel, n_valid=n_valid, k_scale=dk ** -0.5)
    state_specs = [pl.BlockSpec((None, hps, dk, dv), lambda bi, h, ci: (bi, h, 0, 0)),
                   pl.BlockSpec((None, hps, 1, dk), lambda bi, h, ci: (bi, h, 0, 0)),
                   pl.BlockSpec((None, hps, 1, LANES), lambda bi, h, ci: (bi, h, 0, 0))]
    return pl.pallas_call(
        kern,
        grid=(b, n_grp, t // L),
        in_specs=[pl.BlockSpec((None, None, None, 2 * hps, L), lambda bi, h, ci: (bi, ci, h, 0, 0)),
                  pl.BlockSpec((None, L, hps * dk), lambda bi, h, ci: (bi, ci, h)),
                  pl.BlockSpec((None, L, hps * dk), lambda bi, h, ci: (bi, ci, n_grp + h)),
                  pl.BlockSpec((None, L, hps * dv), lambda bi, h, ci: (bi, ci, v_blk + h)),
                  pl.BlockSpec((None, L, hps * dv), lambda bi, h, ci: (bi, ci, v_blk + n_grp + h)),
                  pl.BlockSpec((1, dv), lambda bi, h, ci: (0, 0))] + state_specs,
        out_specs=[pl.BlockSpec((None, L, hps * dv), lambda bi, h, ci: (bi, ci, h))] + state_specs,
        out_shape=[jax.ShapeDtypeStruct((b, t, n_heads * dv), BF16),
                   jax.ShapeDtypeStruct(c0.shape, F32),
                   jax.ShapeDtypeStruct(n0.shape, F32),
                   jax.ShapeDtypeStruct(m0.shape, F32)],
        scratch_shapes=[pltpu.VMEM((hps, dk, dv), F32), pltpu.VMEM((hps, 1, dk), F32),
                        pltpu.VMEM((hps, 1, LANES), F32)],
        compiler_params=_cparams("parallel", "parallel", "arbitrary"),
        name="mlstm_scan",
    )(gates, proj, proj, proj, proj, out_gain, c0, n0, m0)


def _hgrn_tables(c):
    n_lev = int(np.log2(c))
    lev = np.zeros(((n_lev + 1) * c, c), np.float32)
    lev[:c] = np.tril(np.ones((c, c), np.float32))
    for l in range(1, n_lev + 1):
        m = 1 << (l - 1)
        for t in range(c):
            mid = (t >> l << l) + m
            if t & m:
                lev[l * c + t, mid:t + 1] = 1.0
            else:
                lev[l * c + t, t + 1:mid] = 1.0
    t = np.arange(c)[:, None]
    s = np.arange(c)[None, :]
    x = t ^ s
    pair_level = np.where(s > t, -1, np.where(s == t, 0, np.floor(np.log2(np.maximum(x, 1))).astype(np.int32) + 1))
    return jnp.asarray(lev, BF16), jnp.asarray(pair_level, jnp.int32)


def _hgrn_kernel(lev_ref, lvl_ref, q_ref, f_ref, i_ref, g_ref, lb_ref, gn_ref, s0_ref,
                 y_ref, s_out, st_scr, *, n_valid):
    ci = pl.program_id(2)
    c = q_ref.shape[0]
    n_lev = lev_ref.shape[0] // c - 1
    hps = st_scr.shape[0]
    dh = st_scr.shape[1]

    @pl.when(ci == 0)
    def _():
        for hh in range(hps):
            st_scr[hh] = s0_ref[hh].T

    fp = f_ref[...]
    lb = lb_ref[...]
    log_sig = -_softplus(-fp)
    la = jnp.log(lb)
    lbb = jnp.log1p(-lb) + log_sig
    lf = jnp.maximum(la, lbb) + jnp.log1p(jnp.exp(-jnp.abs(la - lbb)))
    kk = (1.0 - lb) * jax.nn.sigmoid(-fp)
    if n_valid < c:
        valid = lax.broadcasted_iota(jnp.int32, (c, 1), 0) < n_valid
        lf = jnp.where(valid, lf, 0.0)
        kk = jnp.where(valid, kk, 0.0)

    hi, lo = _split_bf16(lf)
    lev = lev_ref[...]
    d_all = _dot(lev, hi) + _dot(lev, lo)
    bcum = d_all[0:c]
    q = q_ref[...]
    iv_f = i_ref[...]
    lvl = lvl_ref[...]
    heads = [slice(hh * dh, (hh + 1) * dh) for hh in range(hps)]
    qb = q.astype(BF16)
    kb = kk.astype(BF16)
    a = [jnp.where(lvl == 0, _dot_nt(qb[:, sl], kb[:, sl]), 0.0) for sl in heads]
    for l in range(1, n_lev + 1):
        e = jnp.exp(d_all[l * c:(l + 1) * c])
        qe = (q * e).astype(BF16)
        ke = (kk * e).astype(BF16)
        a = [a[hh] + jnp.where(lvl == l, _dot_nt(qe[:, sl], ke[:, sl]), 0.0) for hh, sl in enumerate(heads)]
    blast = bcum[c - 1:c, :]
    q_in = (q * jnp.exp(bcum)).astype(BF16)
    k_out = (kk * jnp.exp(blast - bcum)).astype(BF16)
    decay = jnp.exp(blast)
    gate = jax.nn.sigmoid(g_ref[...])
    for hh, sl in enumerate(heads):
        st = st_scr[hh]
        o = _dot(a[hh].astype(BF16), iv_f[:, sl].astype(BF16)) + _dot_nt(q_in[:, sl], st.astype(BF16))
        y_ref[:, sl] = (gate[:, sl] * _rms(o, gn_ref[...])).astype(y_ref.dtype)
        st_scr[hh] = decay[:, sl] * st + _dot(iv_f[:, sl].T.astype(BF16), k_out[:, sl])

    @pl.when(ci == pl.num_programs(2) - 1)
    def _():
        for hh in range(hps):
            s_out[hh] = st_scr[hh].T


def hgrn_scan(proj, lower_bound, out_gain, s0, n_heads, n_valid):
    b, t, _ = proj.shape
    c = CHUNK
    dh = s0.shape[2]
    hps = HG_HEADS_PER_STEP
    n_grp = n_heads // hps
    lev, pair_level = _hgrn_tables(c)
    kern = functools.partial(_hgrn_kernel, n_valid=n_valid)

    def part(p):
        return pl.BlockSpec((None, c, hps * dh), lambda bi, h, ci: (bi, ci, p * n_grp + h))

    state_spec = pl.BlockSpec((None, hps, dh, dh), lambda bi, h, ci: (bi, h, 0, 0))
    return pl.pallas_call(
        kern,
        grid=(b, n_grp, t // c),
        in_specs=[pl.BlockSpec(lev.shape, lambda bi, h, ci: (0, 0)),
                  pl.BlockSpec(pair_level.shape, lambda bi, h, ci: (0, 0)),
                  part(0), part(1), part(2), part(3),
                  pl.BlockSpec((1, hps * dh), lambda bi, h, ci: (0, h)),
                  pl.BlockSpec((1, dh), lambda bi, h, ci: (0, 0)),
                  state_spec],
        out_specs=[pl.BlockSpec((None, c, hps * dh), lambda bi, h, ci: (bi, ci, h)), state_spec],
        out_shape=[jax.ShapeDtypeStruct((b, t, n_heads * dh), BF16),
                   jax.ShapeDtypeStruct(s0.shape, F32)],
        scratch_shapes=[pltpu.VMEM((hps, dh, dh), F32)],
        compiler_params=_cparams("parallel", "parallel", "arbitrary"),
        name="hgrn_scan",
    )(lev, pair_level, proj, proj, proj, proj, lower_bound, out_gain, s0)


def _pad_tokens(x, t_pad):
    return jnp.pad(x, ((0, 0), (0, t_pad - x.shape[1])) + ((0, 0),) * (x.ndim - 2))


def kernel(x_prompt, x_sample, cache_k, cache_v, state_mlstm_c, state_mlstm_n, state_mlstm_m,
           state_hgrn, page_table, p_prompt, p_sample, norm_mix, norm_ffn, norm_ple,
           sb_w_qkv, sb_q_norm, sb_k_norm, sb_logit_bias, sb_w_out, ml_w_in, ml_b_gate,
           ml_out_norm, ml_w_out, hg_w_in, hg_lb_logits, hg_out_norm, hg_w_out,
           ffn_w_up, ffn_w_down, ple_w_proj, ple_w_gate):
    depth, d = norm_mix.shape
    bp, tp, _ = x_prompt.shape
    bs, ts, _ = x_sample.shape
    sb_heads = sb_logit_bias.shape[1]
    ml_heads = state_mlstm_c.shape[2]
    ml_dk = state_mlstm_c.shape[3]
    ml_dv = state_mlstm_c.shape[4]
    hg_heads = state_hgrn.shape[2]
    hg_dh = state_hgrn.shape[3]
    n_attn, n_pool, page = cache_k.shape[:3]
    ml_qk_w = ml_heads * ml_dk

    g_mix = norm_mix.reshape(depth, 1, d)
    g_ffn = norm_ffn.reshape(depth, 1, d)
    g_ple = norm_ple.reshape(depth, 1, d)
    lb_soft = jax.nn.softmax(hg_lb_logits.astype(F32), axis=0)
    lower_bounds = jnp.cumsum(lb_soft, axis=0) - lb_soft

    hp = x_prompt.reshape(bp * tp, d)
    hs = x_sample.reshape(bs * ts, d)
    pp_all = p_prompt.reshape(depth, bp * tp, -1)
    ps_all = p_sample.reshape(depth, bs * ts, -1)

    outs = {k: [] for k in ("kp", "vp", "ks", "vs", "mcp", "mnp", "mmp", "mcs", "mns", "mms", "hgp", "hgs")}
    i_attn = i_ml = i_hg = 0
    for layer in range(depth):
        kind = layer % N_MIXERS
        up = rmsnorm_bf16(hp, g_mix, layer)
        us = rmsnorm_bf16(hs, g_mix, layer)
        if kind == 0:
            j = i_attn
            i_attn += 1
            gq = sb_q_norm[j].reshape(1, LANES)
            gk = sb_k_norm[j].reshape(1, LANES)
            q_p = matmul_cols(up, sb_w_qkv, j, 0, d, gq).reshape(bp, tp, d)
            k_p = matmul_cols(up, sb_w_qkv, j, d, d, gk).reshape(bp, tp, d)
            v_p = matmul_cols(up, sb_w_qkv, j, 2 * d, d).reshape(bp, tp, d)
            q_s = matmul_cols(us, sb_w_qkv, j, 0, d, gq).reshape(bs, ts, sb_heads, LANES)
            k_s = matmul_cols(us, sb_w_qkv, j, d, d, gk).reshape(bs, ts, sb_heads, LANES)
            v_s = matmul_cols(us, sb_w_qkv, j, 2 * d, d).reshape(bs, ts, sb_heads, LANES)
            outs["kp"].append(k_p.reshape(bp, tp, sb_heads, LANES))
            outs["vp"].append(v_p.reshape(bp, tp, sb_heads, LANES))
            outs["ks"].append(k_s)
            outs["vs"].append(v_s)
            ap = sb_attend_prompt(q_p, k_p, v_p, sb_logit_bias[j]).reshape(bp * tp, d)
            a_s = sb_attend_sample(q_s, k_s, v_s, cache_k, cache_v, page_table,
                                   sb_logit_bias[j], j).reshape(bs * ts, d)
            hp = matmul_residual(ap, sb_w_out, j, hp)
            hs = matmul_residual(a_s, sb_w_out, j, hs)
        elif kind == 1:
            j = i_ml
            i_ml += 1
            wg_t = ml_w_in[j][:, 2 * ml_qk_w + 2 * d:].T
            b_gate = ml_b_gate[j].reshape(-1, 1)
            gn = ml_out_norm[j].reshape(1, ml_dv)
            n_main = 2 * ml_qk_w + 2 * d
            proj_p = matmul_cols(up, ml_w_in, j, 0, n_main).reshape(bp, tp, n_main)
            proj_s = matmul_cols(us, ml_w_in, j, 0, n_main).reshape(bs, ts, n_main)
            gates_p = mlstm_gate_layout(ml_gates(up, wg_t, b_gate), bp, tp, ml_heads)
            gates_s = mlstm_gate_layout(ml_gates(us, wg_t, b_gate), bs, ts, ml_heads)
            zc = jnp.zeros((bp, ml_heads, ml_dk, ml_dv), F32)
            zn = jnp.zeros((bp, ml_heads, 1, ml_dk), F32)
            zm = jnp.zeros((bp, ml_heads, 1, LANES), F32)
            yp, cp, npv, mp = mlstm_scan(proj_p, gates_p, gn, zc, zn, zm, ml_heads, CHUNK)
            m0 = jnp.broadcast_to(state_mlstm_m[j][:, :, None, None], (bs, ml_heads, 1, LANES))
            ys, cs, nsv, ms = mlstm_scan(_pad_tokens(proj_s, CHUNK), gates_s, gn, state_mlstm_c[j],
                                         state_mlstm_n[j][:, :, None, :], m0, ml_heads, ts)
            outs["mcp"].append(cp)
            outs["mnp"].append(npv[:, :, 0, :])
            outs["mmp"].append(mp[:, :, 0, 0])
            outs["mcs"].append(cs)
            outs["mns"].append(nsv[:, :, 0, :])
            outs["mms"].append(ms[:, :, 0, 0])
            hp = matmul_residual(yp.reshape(bp * tp, d), ml_w_out, j, hp)
            hs = matmul_residual(ys[:, :ts].reshape(bs * ts, d), ml_w_out, j, hs)
        else:
            j = i_hg
            i_hg += 1
            lb = lower_bounds[layer].reshape(1, d)
            gn = hg_out_norm[j].reshape(1, hg_dh)
            proj_p = matmul_cols(up, hg_w_in, j, 0, 4 * d).reshape(bp, tp, 4 * d)
            proj_s = matmul_cols(us, hg_w_in, j, 0, 4 * d).reshape(bs, ts, 4 * d)
            zs = jnp.zeros((bp, hg_heads, hg_dh, hg_dh), F32)
            yp, sp = hgrn_scan(proj_p, lb, gn, zs, hg_heads, CHUNK)
            ys, ss = hgrn_scan(_pad_tokens(proj_s, CHUNK), lb, gn, state_hgrn[j], hg_heads, ts)
            outs["hgp"].append(sp)
            outs["hgs"].append(ss)
            hp = matmul_residual(yp.reshape(bp * tp, d), hg_w_out, j, hp)
            hs = matmul_residual(ys[:, :ts].reshape(bs * ts, d), hg_w_out, j, hs)
        hp = matmul_residual_acc(ffn_up(rmsnorm_bf16(hp, g_ffn, layer), ffn_w_up, layer), ffn_w_down, layer, hp)
        hs = matmul_residual_acc(ffn_up(rmsnorm_bf16(hs, g_ffn, layer), ffn_w_up, layer), ffn_w_down, layer, hs)
        hp = ple(hp, rmsnorm_bf16(hp, g_ple, layer), pp_all, ple_w_gate, ple_w_proj, layer)
        hs = ple(hs, rmsnorm_bf16(hs, g_ple, layer), ps_all, ple_w_gate, ple_w_proj, layer)

    st = {k: jnp.stack(v) for k, v in outs.items()}
    return (hp.reshape(bp, tp, d), hs.reshape(bs, ts, d), st["kp"], st["vp"], st["ks"], st["vs"],
            st["mcp"], st["mnp"], st["mmp"], st["mcs"], st["mns"], st["mms"], st["hgp"], st["hgs"])
```

```python
import functools

import numpy as np
import jax
import jax.numpy as jnp
from jax import lax
from jax.experimental import pallas as pl
from jax.experimental.pallas import tpu as pltpu

F32 = jnp.float32
BF16 = jnp.bfloat16

NORM_EPS = 1e-6
N_MIXERS = 3
GATE_SOFTCAP = 15.0
LANES = 128
SUBLANES = 8
VMEM_LIMIT_BYTES = 56 * 1024 * 1024
ROW_TILE = 2048
COL_TILE = 512
PLE_COL_TILE = 256
ACC_ROW_TILE = 1024
ACC_COL_TILE = 1024
ACC_K_TILE = 2048
SB_Q_TILE = 512
SB_K_TILE = 512
SB_PAGES_PER_STEP = 8
CHUNK = 128
ML_HEADS_PER_STEP = 8
HG_HEADS_PER_STEP = 8
NEG_BIG = -1e30
LOG2_E = 1.4426950408889634


def _cparams(*sem):
    return pltpu.CompilerParams(dimension_semantics=sem, vmem_limit_bytes=VMEM_LIMIT_BYTES)


def _dot(a, b):
    return jnp.dot(a, b, preferred_element_type=F32)


def _dot_nt(a, b):
    return lax.dot_general(a, b, (((1,), (1,)), ((), ())), preferred_element_type=F32)


def _split_bf16(x):
    hi = x.astype(BF16)
    lo = (x - hi.astype(F32)).astype(BF16)
    return hi, lo


def _softplus(z):
    return jnp.maximum(z, 0.0) + jnp.log1p(jnp.exp(-jnp.abs(z)))


def _pad_rows(x, rows):
    if x.shape[0] == rows:
        return x
    return jnp.concatenate([x, jnp.zeros((rows - x.shape[0], x.shape[1]), x.dtype)], axis=0)


def _rms(x, gain):
    ms = jnp.mean(x * x, axis=-1, keepdims=True)
    return x * lax.rsqrt(ms + NORM_EPS) * gain


def _rmsnorm_kernel(x_ref, g_ref, o_ref):
    o_ref[...] = _rms(x_ref[...], g_ref[...]).astype(o_ref.dtype)


def rmsnorm_bf16(x, gains, layer):
    m, d = x.shape
    tm = min(m, 512)
    return pl.pallas_call(
        _rmsnorm_kernel,
        grid=(m // tm,),
        in_specs=[pl.BlockSpec((tm, d), lambda i: (i, 0)),
                  pl.BlockSpec((None, 1, d), lambda i: (layer, 0, 0))],
        out_specs=pl.BlockSpec((tm, d), lambda i: (i, 0)),
        out_shape=jax.ShapeDtypeStruct((m, d), BF16),
        compiler_params=_cparams("parallel"),
        name="rmsnorm",
    )(x, gains)


_LINEAR_INPUTS = {"plain": (0, 0), "headnorm": (1, 0), "residual": (0, 1), "relu2": (0, 0), "ple": (1, 2)}


def _linear_epilogue(mode, acc, shared, wp, per, cols, dtype):
    if mode == "plain":
        return acc
    if mode == "headnorm":
        return jnp.concatenate([_rms(acc[:, g * LANES:(g + 1) * LANES], shared[0][...])
                                for g in range(acc.shape[1] // LANES)], axis=1)
    if mode == "residual":
        return per[0][:, cols] + acc
    if mode == "relu2":
        a = jnp.maximum(acc, 0.0)
        return (a * a).astype(dtype)
    return per[1][:, cols] + jax.nn.sigmoid(acc) * _dot(per[0][...].astype(BF16), wp)


def _linear_kernel(*refs, mode, has_side):
    n_shared, n_per = _LINEAR_INPUTS[mode]
    n_groups = 2 if has_side else 1
    w = refs[0][...].astype(BF16)
    shared = refs[1:1 + n_shared]
    pos = 1 + n_shared
    groups = [refs[pos + g * (1 + n_per):pos + (g + 1) * (1 + n_per)] for g in range(n_groups)]
    outs = refs[pos + n_groups * (1 + n_per):]
    wp = shared[0][...].astype(BF16) if mode == "ple" else None
    tn = w.shape[1]

    x_ref, *per = groups[0]
    acc = _dot(x_ref[...].astype(BF16), w)
    outs[0][...] = _linear_epilogue(mode, acc, shared, wp, per, slice(None), outs[0].dtype)

    if has_side:
        @pl.when(pl.program_id(0) == 0)
        def _():
            xs_ref, *per_s = groups[1]
            cols = pl.ds(pl.multiple_of(pl.program_id(1) * tn, tn), tn)
            acc_s = _dot(xs_ref[...].astype(BF16), w)
            outs[1][:, cols] = _linear_epilogue(mode, acc_s, shared, wp, per_s, cols, outs[1].dtype)


def linear(mode, x, w_all, layer, col_start, n_cols, shared=(), per=(), side=None, out_dtype=F32,
           col_tile=COL_TILE):
    m, k = x.shape
    tn = col_tile
    off = col_start // tn
    tm = min(m, ROW_TILE)
    in_specs = [pl.BlockSpec((None, k, tn), lambda i, j: (layer, 0, off + j))]
    if mode == "headnorm":
        in_specs.append(pl.BlockSpec((1, LANES), lambda i, j: (0, 0)))
    elif mode == "ple":
        in_specs.append(pl.BlockSpec((None, shared[0].shape[1], tn), lambda i, j: (layer, 0, j)))
    args = [w_all, *shared, x, *per]
    in_specs.append(pl.BlockSpec((tm, k), lambda i, j: (i, 0)))
    if mode == "residual":
        in_specs.append(pl.BlockSpec((tm, tn), lambda i, j: (i, j)))
    elif mode == "ple":
        in_specs.append(pl.BlockSpec((None, tm, per[0].shape[2]), lambda i, j: (layer, i, 0)))
        in_specs.append(pl.BlockSpec((tm, tn), lambda i, j: (i, j)))
    out_specs = [pl.BlockSpec((tm, tn), lambda i, j: (i, j))]
    out_shape = [jax.ShapeDtypeStruct((m, n_cols), out_dtype)]
    if side is not None:
        x_s, per_s = side
        ms = x_s.shape[0]
        args += [x_s, *per_s]
        in_specs.append(pl.BlockSpec((ms, k), lambda i, j: (0, 0)))
        if mode == "residual":
            in_specs.append(pl.BlockSpec((ms, n_cols), lambda i, j: (0, 0)))
        elif mode == "ple":
            in_specs.append(pl.BlockSpec((None, ms, per_s[0].shape[2]), lambda i, j: (layer, 0, 0)))
            in_specs.append(pl.BlockSpec((ms, n_cols), lambda i, j: (0, 0)))
        out_specs.append(pl.BlockSpec((ms, n_cols), lambda i, j: (0, 0)))
        out_shape.append(jax.ShapeDtypeStruct((ms, n_cols), out_dtype))
    outs = pl.pallas_call(
        functools.partial(_linear_kernel, mode=mode, has_side=side is not None),
        grid=(m // tm, n_cols // tn),
        in_specs=in_specs,
        out_specs=out_specs,
        out_shape=out_shape,
        compiler_params=_cparams("arbitrary", "arbitrary"),
        name="linear_" + mode,
    )(*args)
    return outs[0] if side is None else outs


def _linear_acc_kernel(w_ref, x_ref, r_ref, xs_ref, rs_ref, o_ref, os_ref):
    kk = pl.program_id(2)
    w = w_ref[...].astype(BF16)
    tk, tn = w.shape

    @pl.when(kk == 0)
    def _():
        o_ref[...] = r_ref[...]

    o_ref[...] += _dot(x_ref[...], w)

    @pl.when(pl.program_id(0) == 0)
    def _():
        cols = pl.ds(pl.multiple_of(pl.program_id(1) * tn, tn), tn)

        @pl.when(kk == 0)
        def _():
            os_ref[:, cols] = rs_ref[:, cols]

        os_ref[:, cols] += _dot(xs_ref[:, pl.ds(pl.multiple_of(kk * tk, tk), tk)], w)


def linear_residual_acc(x, w_all, layer, res, x_side, res_side):
    m, k = x.shape
    n = res.shape[1]
    ms = x_side.shape[0]
    tm = min(m, ACC_ROW_TILE)
    tn = ACC_COL_TILE
    tk = ACC_K_TILE
    return pl.pallas_call(
        _linear_acc_kernel,
        grid=(m // tm, n // tn, k // tk),
        in_specs=[pl.BlockSpec((None, tk, tn), lambda i, j, kk: (layer, kk, j)),
                  pl.BlockSpec((tm, tk), lambda i, j, kk: (i, kk)),
                  pl.BlockSpec((tm, tn), lambda i, j, kk: (i, j)),
                  pl.BlockSpec((ms, k), lambda i, j, kk: (0, 0)),
                  pl.BlockSpec((ms, n), lambda i, j, kk: (0, 0))],
        out_specs=[pl.BlockSpec((tm, tn), lambda i, j, kk: (i, j)),
                   pl.BlockSpec((ms, n), lambda i, j, kk: (0, 0))],
        out_shape=[jax.ShapeDtypeStruct((m, n), F32), jax.ShapeDtypeStruct((ms, n), F32)],
        compiler_params=_cparams("arbitrary", "arbitrary", "arbitrary"),
        name="linear_acc",
    )(w_all, x, res, x_side, res_side)


def _cumsum_weights():
    row = lax.broadcasted_iota(jnp.int32, (2 * LANES, 2 * LANES), 0) & (LANES - 1)
    col = lax.broadcasted_iota(jnp.int32, (2 * LANES, 2 * LANES), 1)
    return jnp.where((col >= LANES) | (row > col), 1.0, 0.0).astype(BF16)


def _log2_sigmoid_pair(zc):
    neg_abs = lax.bitcast_convert_type(lax.bitcast_convert_type(zc, jnp.uint32) | jnp.uint32(0x80000000), F32)
    log_sig = jnp.minimum(zc, 0.0) - jnp.log2(1.0 + jnp.exp2(neg_abs))
    return log_sig, log_sig - zc


def _sb_prompt_kernel(bias_ref, q_ref, k_ref, v_ref, o_ref, acc_scr, carry_scr, *, scale, tq, tk):
    h = pl.program_id(1)
    qi = pl.program_id(2)
    q = q_ref[...].astype(BF16)
    bias = bias_ref[h] * LOG2_E
    uo = _cumsum_weights()
    acc_scr[...] = jnp.zeros_like(acc_scr)
    carry_scr[...] = jnp.zeros_like(carry_scr)
    n_diag = tq // tk
    n_blk = tk // LANES

    def tile(ks, masked):
        kb = k_ref[pl.ds(ks, tk), :].astype(BF16)
        vb = v_ref[pl.ds(ks, tk), :].astype(BF16)
        zc = _dot_nt(q, kb) * (scale * LOG2_E) + bias
        log_sig, log_keep = _log2_sigmoid_pair(zc)
        if masked:
            q_pos = qi * tq + lax.broadcasted_iota(jnp.int32, (tq, tk), 0)
            k_pos = ks + lax.broadcasted_iota(jnp.int32, (tq, tk), 1)
            mask = k_pos < q_pos
            log_keep = jnp.where(mask, log_keep, 0.0)
        hi, lo = _split_bf16(log_keep)
        carry = carry_scr[...]
        tails = [None] * n_blk
        for c in reversed(range(n_blk)):
            sl = slice(c * LANES, (c + 1) * LANES)
            cu = _dot(jnp.concatenate([hi[:, sl], lo[:, sl]], axis=1), uo)
            tails[c] = cu[:, :LANES] + carry
            carry = carry + cu[:, LANES:]
        w = jnp.exp2(log_sig + jnp.concatenate(tails, axis=1))
        if masked:
            w = jnp.where(mask, w, 0.0)
        acc_scr[...] += _dot(w.astype(BF16), vb)
        carry_scr[...] = carry

    for j in range(n_diag):
        tile(pl.multiple_of((qi * n_diag + n_diag - 1 - j) * tk, tk), True)

    def body(it, _):
        tile(pl.multiple_of((qi * n_diag - 1 - it) * tk, tk), False)
        return 0

    lax.fori_loop(0, qi * n_diag, body, 0)
    o_ref[...] = acc_scr[...].astype(o_ref.dtype)


def sb_attend_prompt(q, k, v, bias):
    b, t, _ = q.shape
    dh = LANES
    n_heads = bias.shape[0]
    tq = min(t, SB_Q_TILE)
    tk = min(tq, SB_K_TILE)
    kern = functools.partial(_sb_prompt_kernel, scale=dh ** -0.5, tq=tq, tk=tk)
    return pl.pallas_call(
        kern,
        grid=(b, n_heads, t // tq),
        in_specs=[pl.BlockSpec(memory_space=pltpu.SMEM),
                  pl.BlockSpec((None, tq, dh), lambda bi, h, qi: (bi, qi, h)),
                  pl.BlockSpec((None, t, dh), lambda bi, h, qi: (bi, 0, h)),
                  pl.BlockSpec((None, t, dh), lambda bi, h, qi: (bi, 0, h))],
        out_specs=pl.BlockSpec((None, tq, dh), lambda bi, h, qi: (bi, qi, h)),
        out_shape=jax.ShapeDtypeStruct((b, t, n_heads * dh), BF16),
        scratch_shapes=[pltpu.VMEM((tq, dh), F32), pltpu.VMEM((tq, LANES), F32)],
        compiler_params=_cparams("parallel", "parallel", "arbitrary"),
        name="sb_attend_prompt",
    )(bias, q, k, v)


def _suffix_sum_strided(x, stride):
    w = x.shape[1]
    lane = lax.broadcasted_iota(jnp.int32, x.shape, 1)
    s = stride
    while s < w:
        if s % LANES == 0:
            shifted = jnp.concatenate([x[:, s:], jnp.zeros((x.shape[0], s), F32)], axis=1)
        else:
            shifted = jnp.where(lane < w - s, pltpu.roll(x, w - s, axis=1), 0.0)
        x = x + shifted
        s *= 2
    return x


def _sb_sample_kernel(pt_ref, q_ref, bias_ref, kn_ref, vn_ref, *rest, scale, n_heads, n_new, pages_per_step):
    k_refs = rest[:pages_per_step]
    v_refs = rest[pages_per_step:2 * pages_per_step]
    o_ref, z_scr, w_scr, acc_scr, carry_scr = rest[2 * pages_per_step:]
    s = pl.program_id(1)
    width = kn_ref.shape[0]
    rows = n_new * n_heads
    lane_h = lax.broadcasted_iota(jnp.int32, (n_heads, width), 1) & (n_heads - 1)
    same_head = lane_h == lax.broadcasted_iota(jnp.int32, (n_heads, width), 0)
    row = lax.broadcasted_iota(jnp.int32, (2 * n_new, width), 0)
    lane = lax.broadcasted_iota(jnp.int32, (2 * n_new, width), 1)
    row128 = lax.broadcasted_iota(jnp.int32, (2 * n_new, LANES), 0)
    lane128 = lax.broadcasted_iota(jnp.int32, (2 * n_new, LANES), 1)
    q = q_ref[...].astype(BF16)

    def process(slot, blocks, mask):
        for half, (k_blk, _) in enumerate(blocks):
            zfull = _dot_nt(q, k_blk.astype(BF16))
            for i in range(n_new):
                part = zfull[i * n_heads:(i + 1) * n_heads, :]
                r = half * n_new + i
                z_scr[slot, r:r + 1, :] = jnp.sum(jnp.where(same_head, part, 0.0), axis=0, keepdims=True)
        zc = z_scr[slot] * (scale * LOG2_E) + bias_ref[...] * LOG2_E
        log_sig, log_keep = _log2_sigmoid_pair(zc)
        if mask is not None:
            log_keep = jnp.where(mask, log_keep, 0.0)
        incl = _suffix_sum_strided(log_keep, n_heads)
        tot = jnp.where(lane128 < n_heads, incl[:, :LANES], 0.0)
        shift = n_heads
        while shift < LANES:
            tot = tot + pltpu.roll(tot, shift, axis=1)
            shift *= 2
        tot_other = pltpu.roll(tot, n_new, axis=0)
        carry = carry_scr[...] + jnp.where(row128 >= n_new, tot_other, 0.0)
        w = jnp.exp2(log_sig + (incl - log_keep) + jnp.concatenate([carry] * (width // LANES), axis=1))
        if mask is not None:
            w = jnp.where(mask, w, 0.0)
        pv = None
        for half, (_, v_blk) in enumerate(blocks):
            for i in range(n_new):
                r = half * n_new + i
                w_scr[slot, half * rows + i * n_heads:half * rows + (i + 1) * n_heads, :] = jnp.where(
                    same_head, jnp.broadcast_to(w[r:r + 1, :], (n_heads, width)), 0.0).astype(BF16)
            d = _dot(w_scr[slot, half * rows:(half + 1) * rows, :], v_blk.astype(BF16))
            pv = d if pv is None else pv + d
        acc_scr[...] += pv
        carry_scr[...] += tot + tot_other

    @pl.when(s == 0)
    def _():
        z_scr[...] = jnp.zeros_like(z_scr)
        acc_scr[...] = jnp.zeros_like(acc_scr)
        carry_scr[...] = jnp.zeros_like(carry_scr)
        process(0, [(kn_ref[...], vn_ref[...])], (row < n_new) & (lane < row * n_heads))

    for p in range(pages_per_step // 2):
        process(p, [(k_refs[2 * p][...], v_refs[2 * p][...]), (k_refs[2 * p + 1][...], v_refs[2 * p + 1][...])],
                None)

    @pl.when(s == pl.num_programs(1) - 1)
    def _():
        o_ref[...] = acc_scr[...]


def sb_attend_sample(q, k_new, v_new, cache_k, cache_v, page_table, bias, layer):
    n_seq, n_new, n_heads, dh = q.shape
    assert 2 * n_new == SUBLANES, "two pages of n_new query rows share one 8-sublane array"
    n_pages = page_table.shape[1]
    n_attn, n_pool, page = cache_k.shape[:3]
    width = page * n_heads
    pps = SB_PAGES_PER_STEP
    rows = n_new * n_heads
    pad = ((0, 0), (0, page - n_new), (0, 0), (0, 0))
    k_pad = jnp.pad(k_new, pad).reshape(n_seq, width, dh)
    v_pad = jnp.pad(v_new, pad).reshape(n_seq, width, dh)
    ck = cache_k.reshape(n_attn, n_pool, width, dh)
    cv = cache_v.reshape(n_attn, n_pool, width, dh)
    bias_lanes = jnp.tile(bias.astype(F32), page).reshape(1, width)

    def page_spec(j):
        return pl.BlockSpec((None, None, width, dh),
                            lambda b, s, pt: (layer, pt[b, n_pages - 1 - (s * pps + j)], 0, 0))

    seq_spec = pl.BlockSpec((None, rows, dh), lambda b, s, pt: (b, 0, 0))
    new_spec = pl.BlockSpec((None, width, dh), lambda b, s, pt: (b, 0, 0))
    kern = functools.partial(_sb_sample_kernel, scale=dh ** -0.5, n_heads=n_heads, n_new=n_new,
                             pages_per_step=pps)
    grid_spec = pltpu.PrefetchScalarGridSpec(
        num_scalar_prefetch=1,
        grid=(n_seq, n_pages // pps),
        in_specs=[seq_spec, pl.BlockSpec((1, width), lambda b, s, pt: (0, 0)), new_spec, new_spec]
                 + [page_spec(j) for j in range(pps)] * 2,
        out_specs=seq_spec,
        scratch_shapes=[pltpu.VMEM((pps // 2, 2 * n_new, width), F32),
                        pltpu.VMEM((pps // 2, 2 * rows, width), BF16),
                        pltpu.VMEM((rows, dh), F32), pltpu.VMEM((2 * n_new, LANES), F32)],
    )
    out = pl.pallas_call(
        kern,
        grid_spec=grid_spec,
        out_shape=jax.ShapeDtypeStruct((n_seq, rows, dh), F32),
        compiler_params=_cparams("parallel", "arbitrary"),
        name="sb_attend_sample",
    )(page_table, q.reshape(n_seq, rows, dh), bias_lanes, k_pad, v_pad, *([ck] * pps), *([cv] * pps))
    return out.reshape(n_seq, n_new, n_heads * dh)


def _ml_gates_kernel(u_ref, wg_ref, b_ref, o_ref, *, n_heads):
    pre = _dot_nt(wg_ref[...].astype(BF16), u_ref[...]) + b_ref[...]
    pre = GATE_SOFTCAP * jnp.tanh(pre / GATE_SOFTCAP)
    o_ref[0:n_heads, :] = pre[0:n_heads]
    o_ref[n_heads:, :] = -_softplus(-pre[n_heads:])


def ml_gates(u, wg_t, b_gate):
    m, d = u.shape
    g = wg_t.shape[0]
    tm = min(m, ROW_TILE)
    return pl.pallas_call(
        functools.partial(_ml_gates_kernel, n_heads=g // 2),
        grid=(m // tm,),
        in_specs=[pl.BlockSpec((tm, d), lambda i: (i, 0)),
                  pl.BlockSpec((g, d), lambda i: (0, 0)),
                  pl.BlockSpec((g, 1), lambda i: (0, 0))],
        out_specs=pl.BlockSpec((g, tm), lambda i: (0, i)),
        out_shape=jax.ShapeDtypeStruct((g, m), F32),
        compiler_params=_cparams("parallel"),
        name="ml_gates",
    )(u, wg_t, b_gate)


def mlstm_gate_layout(g, b, t, n_heads):
    hps = ML_HEADS_PER_STEP
    t_pad = -(-t // CHUNK) * CHUNK
    g = jnp.pad(g.reshape(2, n_heads // hps, hps, b, t), ((0, 0),) * 4 + ((0, t_pad - t),))
    g = g.reshape(2, n_heads // hps, hps, b, t_pad // CHUNK, CHUNK).transpose(3, 4, 1, 2, 0, 5)
    return g.reshape(b, t_pad // CHUNK, n_heads // hps, 2 * hps, CHUNK)


def _lane_cumsum(x):
    lane = lax.broadcasted_iota(jnp.int32, x.shape, 1)
    shift = 1
    while shift < x.shape[1]:
        x = x + jnp.where(lane >= shift, pltpu.roll(x, shift, axis=1), 0.0)
        shift *= 2
    return x


def _mlstm_kernel(g_ref, q_ref, k_ref, v_ref, og_ref, gn_ref, c0_ref, n0_ref, m0_ref,
                  y_ref, c_out, n_out, m_out, c_scr, n_scr, m_scr, *, n_valid, k_scale):
    ci = pl.program_id(2)
    L = g_ref.shape[1]
    t_in = q_ref.shape[0]
    hps = c_scr.shape[0]
    dk = c_scr.shape[1]
    dv = c_scr.shape[2]

    @pl.when(ci == 0)
    def _():
        c_scr[...] = c0_ref[...]
        n_scr[...] = n0_ref[...]
        m_scr[...] = m0_ref[...]

    g = g_ref[...]
    row8 = lax.broadcasted_iota(jnp.int32, (2 * hps, L), 0)
    if n_valid < L:
        valid = lax.broadcasted_iota(jnp.int32, (2 * hps, L), 1) < n_valid
        g = jnp.where(valid, g, jnp.where((row8 & 1) == 0, NEG_BIG, 0.0))
    bc = _lane_cumsum(g)
    rows = jnp.concatenate([jnp.where((row8 & 1) == 0, g, bc), jnp.zeros((L - 2 * hps, L), F32)], axis=0)
    cols = rows.T
    t_idx = lax.broadcasted_iota(jnp.int32, (L, L), 0)
    s_idx = lax.broadcasted_iota(jnp.int32, (L, L), 1)
    causal = s_idx <= t_idx

    for hh in range(hps):
        li_r = g[2 * hh:2 * hh + 1, :]
        bc_r = bc[2 * hh + 1:2 * hh + 2, :]
        li_c = cols[:, 2 * hh:2 * hh + 1]
        bc_c = cols[:, 2 * hh + 1:2 * hh + 2]
        m_prev = m_scr[hh, 0:1, 0:1]
        dmat = jnp.where(causal, bc_c + (li_r - bc_r), NEG_BIG)
        inter = bc_c + m_prev
        m_t = jnp.maximum(inter, jnp.max(dmat, axis=1, keepdims=True))

        qf = _pad_rows(q_ref[:, hh * dk:(hh + 1) * dk], L)
        q = qf.astype(BF16)
        ks_f = _pad_rows(k_ref[:, hh * dk:(hh + 1) * dk], L) * k_scale
        v = _pad_rows(v_ref[:, hh * dv:(hh + 1) * dv], L).astype(BF16)
        a = jnp.exp(dmat - m_t) * _dot_nt(q, ks_f.astype(BF16))
        sc = jnp.exp(inter - m_t)
        c = c_scr[hh]
        n_row = n_scr[hh]
        num = _dot(a.astype(BF16), v) + sc * _dot(q, c.astype(BF16))
        den = jnp.sum(a, axis=1, keepdims=True) + sc * jnp.sum(qf * n_row, axis=1, keepdims=True)
        hval = num / jnp.maximum(jnp.abs(den), jnp.exp(-m_t))
        gate = jax.nn.sigmoid(og_ref[:, hh * dv:(hh + 1) * dv])
        y_ref[:, hh * dv:(hh + 1) * dv] = (gate * _rms(hval[:t_in], gn_ref[...])).astype(y_ref.dtype)

        m_new = m_t[L - 1:L, :]
        bc_last = bc_c[L - 1:L, :]
        wl = jnp.exp(bc_last - bc_c + li_c - m_new)
        sl = jnp.exp(bc_last + m_prev - m_new)
        kw = ks_f * wl
        c_scr[hh] = sl * c + _dot(kw.T.astype(BF16), v)
        n_scr[hh] = sl * n_row + jnp.sum(kw, axis=0, keepdims=True)
        m_scr[hh] = jnp.broadcast_to(m_new, (1, LANES))

    @pl.when(ci == pl.num_programs(2) - 1)
    def _():
        c_out[...] = c_scr[...]
        n_out[...] = n_scr[...]
        m_out[...] = m_scr[...]


def mlstm_scan(proj, gates, out_gain, c0, n0, m0, n_heads, n_valid):
    b, t, _ = proj.shape
    L = min(t, CHUNK)
    dk = c0.shape[2]
    dv = c0.shape[3]
    hps = ML_HEADS_PER_STEP
    n_grp = n_heads // hps
    v_blk = 2 * n_heads * dk // (hps * dv)
    kern = functools.partial(_mlstm_kernel, n_valid=n_valid, k_scale=dk ** -0.5)
    state_specs = [pl.BlockSpec((None, hps, dk, dv), lambda bi, h, ci: (bi, h, 0, 0)),
                   pl.BlockSpec((None, hps, 1, dk), lambda bi, h, ci: (bi, h, 0, 0)),
                   pl.BlockSpec((None, hps, 1, LANES), lambda bi, h, ci: (bi, h, 0, 0))]
    return pl.pallas_call(
        kern,
        grid=(b, n_grp, t // L),
        in_specs=[pl.BlockSpec((None, None, None, 2 * hps, CHUNK), lambda bi, h, ci: (bi, ci, h, 0, 0)),
                  pl.BlockSpec((None, L, hps * dk), lambda bi, h, ci: (bi, ci, h)),
                  pl.BlockSpec((None, L, hps * dk), lambda bi, h, ci: (bi, ci, n_grp + h)),
                  pl.BlockSpec((None, L, hps * dv), lambda bi, h, ci: (bi, ci, v_blk + h)),
                  pl.BlockSpec((None, L, hps * dv), lambda bi, h, ci: (bi, ci, v_blk + n_grp + h)),
                  pl.BlockSpec((1, dv), lambda bi, h, ci: (0, 0))] + state_specs,
        out_specs=[pl.BlockSpec((None, L, hps * dv), lambda bi, h, ci: (bi, ci, h))] + state_specs,
        out_shape=[jax.ShapeDtypeStruct((b, t, n_heads * dv), BF16),
                   jax.ShapeDtypeStruct(c0.shape, F32),
                   jax.ShapeDtypeStruct(n0.shape, F32),
                   jax.ShapeDtypeStruct(m0.shape, F32)],
        scratch_shapes=[pltpu.VMEM((hps, dk, dv), F32), pltpu.VMEM((hps, 1, dk), F32),
                        pltpu.VMEM((hps, 1, LANES), F32)],
        compiler_params=_cparams("parallel", "parallel", "arbitrary"),
        name="mlstm_scan",
    )(gates, proj, proj, proj, proj, out_gain, c0, n0, m0)


def _hgrn_tables(c):
    n_lev = int(np.log2(c))
    lev = np.zeros(((n_lev + 1) * c, c), np.float32)
    lev[:c] = np.tril(np.ones((c, c), np.float32))
    for l in range(1, n_lev + 1):
        m = 1 << (l - 1)
        for t in range(c):
            mid = (t >> l << l) + m
            if t & m:
                lev[l * c + t, mid:t + 1] = 1.0
            else:
                lev[l * c + t, t + 1:mid] = 1.0
    t = np.arange(c)[:, None]
    s = np.arange(c)[None, :]
    x = t ^ s
    pair_level = np.where(s > t, -1, np.where(s == t, 0, np.floor(np.log2(np.maximum(x, 1))).astype(np.int32) + 1))
    return jnp.asarray(lev, BF16), jnp.asarray(pair_level, jnp.int32)


def _hgrn_kernel(lev_ref, lvl_ref, q_ref, f_ref, i_ref, g_ref, lb_ref, gn_ref, s0_ref,
                 y_ref, s_out, st_scr, *, n_valid):
    ci = pl.program_id(2)
    c = lev_ref.shape[1]
    t_in = q_ref.shape[0]
    n_lev = lev_ref.shape[0] // c - 1
    hps = st_scr.shape[0]
    dh = st_scr.shape[1]

    @pl.when(ci == 0)
    def _():
        for hh in range(hps):
            st_scr[hh] = s0_ref[hh].T

    fp = _pad_rows(f_ref[...], c)
    lb = lb_ref[...]
    log_sig = -_softplus(-fp)
    la = jnp.log(lb)
    lbb = jnp.log1p(-lb) + log_sig
    lf = jnp.maximum(la, lbb) + jnp.log1p(jnp.exp(-jnp.abs(la - lbb)))
    kk = (1.0 - lb) * jax.nn.sigmoid(-fp)
    if n_valid < c:
        valid = lax.broadcasted_iota(jnp.int32, (c, 1), 0) < n_valid
        lf = jnp.where(valid, lf, 0.0)
        kk = jnp.where(valid, kk, 0.0)

    hi, lo = _split_bf16(lf)
    lev = lev_ref[...]
    d_all = _dot(lev, hi) + _dot(lev, lo)
    bcum = d_all[0:c]
    q = _pad_rows(q_ref[...], c)
    iv_f = _pad_rows(i_ref[...], c)
    lvl = lvl_ref[...]
    heads = [slice(hh * dh, (hh + 1) * dh) for hh in range(hps)]
    qb = q.astype(BF16)
    kb = kk.astype(BF16)
    a = [jnp.where(lvl == 0, _dot_nt(qb[:, sl], kb[:, sl]), 0.0) for sl in heads]
    for l in range(1, n_lev + 1):
        e = jnp.exp(d_all[l * c:(l + 1) * c])
        qe = (q * e).astype(BF16)
        ke = (kk * e).astype(BF16)
        a = [a[hh] + jnp.where(lvl == l, _dot_nt(qe[:, sl], ke[:, sl]), 0.0) for hh, sl in enumerate(heads)]
    blast = bcum[c - 1:c, :]
    q_in = (q * jnp.exp(bcum)).astype(BF16)
    k_out = (kk * jnp.exp(blast - bcum)).astype(BF16)
    decay = jnp.exp(blast)
    gate = jax.nn.sigmoid(g_ref[...])
    for hh, sl in enumerate(heads):
        st = st_scr[hh]
        o = _dot(a[hh].astype(BF16), iv_f[:, sl].astype(BF16)) + _dot_nt(q_in[:, sl], st.astype(BF16))
        y_ref[:, sl] = (gate[:, sl] * _rms(o[:t_in], gn_ref[...])).astype(y_ref.dtype)
        st_scr[hh] = decay[:, sl] * st + _dot(iv_f[:, sl].T.astype(BF16), k_out[:, sl])

    @pl.when(ci == pl.num_programs(2) - 1)
    def _():
        for hh in range(hps):
            s_out[hh] = st_scr[hh].T


def hgrn_scan(proj, lower_bound, out_gain, s0, n_heads, n_valid):
    b, t, _ = proj.shape
    c = min(t, CHUNK)
    dh = s0.shape[2]
    hps = HG_HEADS_PER_STEP
    n_grp = n_heads // hps
    lev, pair_level = _hgrn_tables(CHUNK)
    kern = functools.partial(_hgrn_kernel, n_valid=n_valid)

    def part(p):
        return pl.BlockSpec((None, c, hps * dh), lambda bi, h, ci: (bi, ci, p * n_grp + h))

    state_spec = pl.BlockSpec((None, hps, dh, dh), lambda bi, h, ci: (bi, h, 0, 0))
    return pl.pallas_call(
        kern,
        grid=(b, n_grp, t // c),
        in_specs=[pl.BlockSpec(lev.shape, lambda bi, h, ci: (0, 0)),
                  pl.BlockSpec(pair_level.shape, lambda bi, h, ci: (0, 0)),
                  part(0), part(1), part(2), part(3),
                  pl.BlockSpec((1, hps * dh), lambda bi, h, ci: (0, h)),
                  pl.BlockSpec((1, dh), lambda bi, h, ci: (0, 0)),
                  state_spec],
        out_specs=[pl.BlockSpec((None, c, hps * dh), lambda bi, h, ci: (bi, ci, h)), state_spec],
        out_shape=[jax.ShapeDtypeStruct((b, t, n_heads * dh), BF16),
                   jax.ShapeDtypeStruct(s0.shape, F32)],
        scratch_shapes=[pltpu.VMEM((hps, dh, dh), F32)],
        compiler_params=_cparams("parallel", "parallel", "arbitrary"),
        name="hgrn_scan",
    )(lev, pair_level, proj, proj, proj, proj, lower_bound, out_gain, s0)


def _pad_tokens(x, t_pad):
    return jnp.pad(x, ((0, 0), (0, t_pad - x.shape[1])) + ((0, 0),) * (x.ndim - 2))


def kernel(x_prompt, x_sample, cache_k, cache_v, state_mlstm_c, state_mlstm_n, state_mlstm_m,
           state_hgrn, page_table, p_prompt, p_sample, norm_mix, norm_ffn, norm_ple,
           sb_w_qkv, sb_q_norm, sb_k_norm, sb_logit_bias, sb_w_out, ml_w_in, ml_b_gate,
           ml_out_norm, ml_w_out, hg_w_in, hg_lb_logits, hg_out_norm, hg_w_out,
           ffn_w_up, ffn_w_down, ple_w_proj, ple_w_gate):
    depth, d = norm_mix.shape
    bp, tp, _ = x_prompt.shape
    bs, ts, _ = x_sample.shape
    sb_heads = sb_logit_bias.shape[1]
    ml_heads = state_mlstm_c.shape[2]
    ml_dk = state_mlstm_c.shape[3]
    ml_dv = state_mlstm_c.shape[4]
    hg_heads = state_hgrn.shape[2]
    hg_dh = state_hgrn.shape[3]
    n_attn, n_pool, page = cache_k.shape[:3]
    ml_qk_w = ml_heads * ml_dk

    g_mix = norm_mix.reshape(depth, 1, d)
    g_ffn = norm_ffn.reshape(depth, 1, d)
    g_ple = norm_ple.reshape(depth, 1, d)
    lb_soft = jax.nn.softmax(hg_lb_logits.astype(F32), axis=0)
    lower_bounds = jnp.cumsum(lb_soft, axis=0) - lb_soft

    hp = x_prompt.reshape(bp * tp, d)
    hs = x_sample.reshape(bs * ts, d)
    pp_all = p_prompt.reshape(depth, bp * tp, -1)
    ps_all = p_sample.reshape(depth, bs * ts, -1)

    outs = {k: [] for k in ("kp", "vp", "ks", "vs", "mcp", "mnp", "mmp", "mcs", "mns", "mms", "hgp", "hgs")}
    i_attn = i_ml = i_hg = 0
    for layer in range(depth):
        kind = layer % N_MIXERS
        up = rmsnorm_bf16(hp, g_mix, layer)
        us = rmsnorm_bf16(hs, g_mix, layer)
        if kind == 0:
            j = i_attn
            i_attn += 1
            gq = sb_q_norm[j].reshape(1, LANES)
            gk = sb_k_norm[j].reshape(1, LANES)
            q_p, q_s = linear("headnorm", up, sb_w_qkv, j, 0, d, shared=(gq,), side=(us, ()))
            k_p, k_s = linear("headnorm", up, sb_w_qkv, j, d, d, shared=(gk,), side=(us, ()))
            v_p, v_s = linear("plain", up, sb_w_qkv, j, 2 * d, d, side=(us, ()))
            q_s, k_s, v_s = (a.reshape(bs, ts, sb_heads, LANES) for a in (q_s, k_s, v_s))
            outs["kp"].append(k_p.reshape(bp, tp, sb_heads, LANES))
            outs["vp"].append(v_p.reshape(bp, tp, sb_heads, LANES))
            outs["ks"].append(k_s)
            outs["vs"].append(v_s)
            ap = sb_attend_prompt(q_p.reshape(bp, tp, d), k_p.reshape(bp, tp, d), v_p.reshape(bp, tp, d),
                                  sb_logit_bias[j]).reshape(bp * tp, d)
            a_s = sb_attend_sample(q_s, k_s, v_s, cache_k, cache_v, page_table,
                                   sb_logit_bias[j], j).reshape(bs * ts, d)
            hp, hs = linear("residual", ap, sb_w_out, j, 0, d, per=(hp,), side=(a_s, (hs,)))
        elif kind == 1:
            j = i_ml
            i_ml += 1
            wg_t = ml_w_in[j][:, 2 * ml_qk_w + 2 * d:].T
            b_gate = ml_b_gate[j].reshape(-1, 1)
            gn = ml_out_norm[j].reshape(1, ml_dv)
            n_main = 2 * ml_qk_w + 2 * d
            proj_p, proj_s = linear("plain", up, ml_w_in, j, 0, n_main, side=(us, ()))
            proj_p = proj_p.reshape(bp, tp, n_main)
            proj_s = proj_s.reshape(bs, ts, n_main)
            gates_p = mlstm_gate_layout(ml_gates(up, wg_t, b_gate), bp, tp, ml_heads)
            gates_s = mlstm_gate_layout(ml_gates(us, wg_t, b_gate), bs, ts, ml_heads)
            zc = jnp.zeros((bp, ml_heads, ml_dk, ml_dv), F32)
            zn = jnp.zeros((bp, ml_heads, 1, ml_dk), F32)
            zm = jnp.zeros((bp, ml_heads, 1, LANES), F32)
            yp, cp, npv, mp = mlstm_scan(proj_p, gates_p, gn, zc, zn, zm, ml_heads, CHUNK)
            m0 = jnp.broadcast_to(state_mlstm_m[j][:, :, None, None], (bs, ml_heads, 1, LANES))
            ys, cs, nsv, ms = mlstm_scan(_pad_tokens(proj_s, SUBLANES), gates_s, gn, state_mlstm_c[j],
                                         state_mlstm_n[j][:, :, None, :], m0, ml_heads, ts)
            outs["mcp"].append(cp)
            outs["mnp"].append(npv[:, :, 0, :])
            outs["mmp"].append(mp[:, :, 0, 0])
            outs["mcs"].append(cs)
            outs["mns"].append(nsv[:, :, 0, :])
            outs["mms"].append(ms[:, :, 0, 0])
            hp, hs = linear("residual", yp.reshape(bp * tp, d), ml_w_out, j, 0, d, per=(hp,),
                            side=(ys[:, :ts].reshape(bs * ts, d), (hs,)))
        else:
            j = i_hg
            i_hg += 1
            lb = lower_bounds[layer].reshape(1, d)
            gn = hg_out_norm[j].reshape(1, hg_dh)
            proj_p, proj_s = linear("plain", up, hg_w_in, j, 0, 4 * d, side=(us, ()))
            proj_p = proj_p.reshape(bp, tp, 4 * d)
            proj_s = proj_s.reshape(bs, ts, 4 * d)
            zs = jnp.zeros((bp, hg_heads, hg_dh, hg_dh), F32)
            yp, sp = hgrn_scan(proj_p, lb, gn, zs, hg_heads, CHUNK)
            ys, ss = hgrn_scan(_pad_tokens(proj_s, SUBLANES), lb, gn, state_hgrn[j], hg_heads, ts)
            outs["hgp"].append(sp)
            outs["hgs"].append(ss)
            hp, hs = linear("residual", yp.reshape(bp * tp, d), hg_w_out, j, 0, d, per=(hp,),
                            side=(ys[:, :ts].reshape(bs * ts, d), (hs,)))
        a_p, a_s = linear("relu2", rmsnorm_bf16(hp, g_ffn, layer), ffn_w_up, layer, 0, ffn_w_up.shape[2],
                          side=(rmsnorm_bf16(hs, g_ffn, layer), ()), out_dtype=BF16)
        hp, hs = linear_residual_acc(a_p, ffn_w_down, layer, hp, a_s, hs)
        hp, hs = linear("ple", rmsnorm_bf16(hp, g_ple, layer), ple_w_gate, layer, 0, d, shared=(ple_w_proj,),
                        per=(pp_all, hp), side=(rmsnorm_bf16(hs, g_ple, layer), (ps_all, hs)),
                        col_tile=PLE_COL_TILE)

    st = {k: jnp.stack(v) for k, v in outs.items()}
    return (hp.reshape(bp, tp, d), hs.reshape(bs, ts, d), st["kp"], st["vp"], st["ks"], st["vs"],
            st["mcp"], st["mnp"], st["mmp"], st["mcs"], st["mns"], st["mms"], st["hgp"], st["hgs"])
```

```python
import functools

import numpy as np
import jax
import jax.numpy as jnp
from jax import lax
from jax.experimental import pallas as pl
from jax.experimental.pallas import tpu as pltpu

F32 = jnp.float32
BF16 = jnp.bfloat16

NORM_EPS = 1e-6
N_MIXERS = 3
GATE_SOFTCAP = 15.0
LANES = 128
SUBLANES = 8
VMEM_LIMIT_BYTES = 56 * 1024 * 1024
ROW_TILE = 2048
COL_TILE = 512
PLE_COL_TILE = 256
ACC_ROW_TILE = 1024
ACC_COL_TILE = 1024
ACC_K_TILE = 2048
SB_Q_TILE = 512
SB_K_TILE = 512
SB_HEADS_PER_STEP = 4
SB_PAGES_PER_STEP = 8
CHUNK = 128
ML_HEADS_PER_STEP = 8
HG_HEADS_PER_STEP = 8
NEG_BIG = -1e30
LOG2_E = 1.4426950408889634


def _cparams(*sem):
    return pltpu.CompilerParams(dimension_semantics=sem, vmem_limit_bytes=VMEM_LIMIT_BYTES)


def _dot(a, b):
    return jnp.dot(a, b, preferred_element_type=F32)


def _dot_nt(a, b):
    return lax.dot_general(a, b, (((1,), (1,)), ((), ())), preferred_element_type=F32)


def _split_bf16(x):
    hi = x.astype(BF16)
    lo = (x - hi.astype(F32)).astype(BF16)
    return hi, lo


def _softplus(z):
    return jnp.maximum(z, 0.0) + jnp.log1p(jnp.exp(-jnp.abs(z)))


def _pad_rows(x, rows):
    if x.shape[0] == rows:
        return x
    return jnp.concatenate([x, jnp.zeros((rows - x.shape[0], x.shape[1]), x.dtype)], axis=0)


def _rms(x, gain):
    ms = jnp.mean(x * x, axis=-1, keepdims=True)
    return x * lax.rsqrt(ms + NORM_EPS) * gain


def _rmsnorm_kernel(x_ref, g_ref, o_ref):
    o_ref[...] = _rms(x_ref[...], g_ref[...]).astype(o_ref.dtype)


def rmsnorm_bf16(x, gains, layer):
    m, d = x.shape
    tm = min(m, 512)
    return pl.pallas_call(
        _rmsnorm_kernel,
        grid=(m // tm,),
        in_specs=[pl.BlockSpec((tm, d), lambda i: (i, 0)),
                  pl.BlockSpec((None, 1, d), lambda i: (layer, 0, 0))],
        out_specs=pl.BlockSpec((tm, d), lambda i: (i, 0)),
        out_shape=jax.ShapeDtypeStruct((m, d), BF16),
        compiler_params=_cparams("parallel"),
        name="rmsnorm",
    )(x, gains)


_LINEAR_INPUTS = {"plain": (0, 0), "headnorm": (1, 0), "residual": (0, 1), "relu2": (0, 0), "ple": (1, 2)}


def _linear_epilogue(mode, acc, shared, wp, per, cols, dtype):
    if mode == "plain":
        return acc
    if mode == "headnorm":
        return jnp.concatenate([_rms(acc[:, g * LANES:(g + 1) * LANES], shared[0][...])
                                for g in range(acc.shape[1] // LANES)], axis=1)
    if mode == "residual":
        return per[0][:, cols] + acc
    if mode == "relu2":
        a = jnp.maximum(acc, 0.0)
        return (a * a).astype(dtype)
    return per[1][:, cols] + jax.nn.sigmoid(acc) * _dot(per[0][...].astype(BF16), wp)


def _linear_kernel(*refs, mode, has_side):
    n_shared, n_per = _LINEAR_INPUTS[mode]
    n_groups = 2 if has_side else 1
    w = refs[0][...].astype(BF16)
    shared = refs[1:1 + n_shared]
    pos = 1 + n_shared
    groups = [refs[pos + g * (1 + n_per):pos + (g + 1) * (1 + n_per)] for g in range(n_groups)]
    outs = refs[len(refs) - n_groups:]
    wp = shared[0][...].astype(BF16) if mode == "ple" else None
    tn = w.shape[1]

    x_ref, *per = groups[0]
    acc = _dot(x_ref[...].astype(BF16), w)
    outs[0][...] = _linear_epilogue(mode, acc, shared, wp, per, slice(None), outs[0].dtype)

    if has_side:
        @pl.when(pl.program_id(0) == 0)
        def _():
            xs_ref, *per_s = groups[1]
            cols = pl.ds(pl.multiple_of(pl.program_id(1) * tn, tn), tn)
            acc_s = _dot(xs_ref[...].astype(BF16), w)
            outs[1][:, cols] = _linear_epilogue(mode, acc_s, shared, wp, per_s, cols, outs[1].dtype)


def linear(mode, x, w_all, layer, col_start, n_cols, shared=(), per=(), side=None, out_dtype=F32,
           col_tile=COL_TILE, slab=None):
    m, k = x.shape
    tn = col_tile
    off = col_start // tn
    tm = min(m, ROW_TILE)
    in_specs = [pl.BlockSpec((None, k, tn), lambda i, j: (layer, 0, off + j))]
    if mode == "headnorm":
        in_specs.append(pl.BlockSpec((1, LANES), lambda i, j: (0, 0)))
    elif mode == "ple":
        in_specs.append(pl.BlockSpec((None, shared[0].shape[1], tn), lambda i, j: (layer, 0, j)))
    args = [w_all, *shared, x, *per]
    in_specs.append(pl.BlockSpec((tm, k), lambda i, j: (i, 0)))
    if mode == "residual":
        in_specs.append(pl.BlockSpec((tm, tn), lambda i, j: (i, j)))
    elif mode == "ple":
        in_specs.append(pl.BlockSpec((None, tm, per[0].shape[2]), lambda i, j: (layer, i, 0)))
        in_specs.append(pl.BlockSpec((tm, tn), lambda i, j: (i, j)))
    out_specs = [pl.BlockSpec((tm, tn), lambda i, j: (i, j))]
    out_shape = [jax.ShapeDtypeStruct((m, n_cols), out_dtype)]
    if side is not None:
        x_s, per_s = side
        ms = x_s.shape[0]
        args += [x_s, *per_s]
        in_specs.append(pl.BlockSpec((ms, k), lambda i, j: (0, 0)))
        if mode == "residual":
            in_specs.append(pl.BlockSpec((ms, n_cols), lambda i, j: (0, 0)))
        elif mode == "ple":
            in_specs.append(pl.BlockSpec((None, ms, per_s[0].shape[2]), lambda i, j: (layer, 0, 0)))
            in_specs.append(pl.BlockSpec((ms, n_cols), lambda i, j: (0, 0)))
        out_specs.append(pl.BlockSpec((ms, n_cols), lambda i, j: (0, 0)))
        out_shape.append(jax.ShapeDtypeStruct((ms, n_cols), out_dtype))
    aliases = {}
    if slab is not None:
        n_slabs, index, prev = slab
        out_specs[0] = pl.BlockSpec((None, tm, tn), lambda i, j: (index, i, j))
        out_shape[0] = jax.ShapeDtypeStruct((n_slabs, m, n_cols), out_dtype)
        if prev is not None:
            aliases = {len(args): 0}
            args.append(prev)
            in_specs.append(pl.BlockSpec(memory_space=pl.ANY))
    outs = pl.pallas_call(
        functools.partial(_linear_kernel, mode=mode, has_side=side is not None),
        grid=(m // tm, n_cols // tn),
        in_specs=in_specs,
        out_specs=out_specs,
        out_shape=out_shape,
        input_output_aliases=aliases,
        compiler_params=_cparams("arbitrary", "arbitrary"),
        name="linear_" + mode,
    )(*args)
    return outs[0] if side is None else outs


def _linear_acc_kernel(w_ref, x_ref, r_ref, xs_ref, rs_ref, o_ref, os_ref):
    kk = pl.program_id(2)
    w = w_ref[...].astype(BF16)
    tk, tn = w.shape

    @pl.when(kk == 0)
    def _():
        o_ref[...] = r_ref[...]

    o_ref[...] += _dot(x_ref[...], w)

    @pl.when(pl.program_id(0) == 0)
    def _():
        cols = pl.ds(pl.multiple_of(pl.program_id(1) * tn, tn), tn)

        @pl.when(kk == 0)
        def _():
            os_ref[:, cols] = rs_ref[:, cols]

        os_ref[:, cols] += _dot(xs_ref[:, pl.ds(pl.multiple_of(kk * tk, tk), tk)], w)


def linear_residual_acc(x, w_all, layer, res, x_side, res_side):
    m, k = x.shape
    n = res.shape[1]
    ms = x_side.shape[0]
    tm = min(m, ACC_ROW_TILE)
    tn = ACC_COL_TILE
    tk = ACC_K_TILE
    return pl.pallas_call(
        _linear_acc_kernel,
        grid=(m // tm, n // tn, k // tk),
        in_specs=[pl.BlockSpec((None, tk, tn), lambda i, j, kk: (layer, kk, j)),
                  pl.BlockSpec((tm, tk), lambda i, j, kk: (i, kk)),
                  pl.BlockSpec((tm, tn), lambda i, j, kk: (i, j)),
                  pl.BlockSpec((ms, k), lambda i, j, kk: (0, 0)),
                  pl.BlockSpec((ms, n), lambda i, j, kk: (0, 0))],
        out_specs=[pl.BlockSpec((tm, tn), lambda i, j, kk: (i, j)),
                   pl.BlockSpec((ms, n), lambda i, j, kk: (0, 0))],
        out_shape=[jax.ShapeDtypeStruct((m, n), F32), jax.ShapeDtypeStruct((ms, n), F32)],
        compiler_params=_cparams("arbitrary", "arbitrary", "arbitrary"),
        name="linear_acc",
    )(w_all, x, res, x_side, res_side)


def _cumsum_weights():
    row = lax.broadcasted_iota(jnp.int32, (LANES, 2 * LANES), 0)
    col = lax.broadcasted_iota(jnp.int32, (LANES, 2 * LANES), 1)
    return jnp.where((col >= LANES) | (row > col), 1.0, 0.0).astype(BF16)


def _log2_sigmoid_pair(zc):
    neg_abs = lax.bitcast_convert_type(lax.bitcast_convert_type(zc, jnp.uint32) | jnp.uint32(0x80000000), F32)
    log_sig = jnp.minimum(zc, 0.0) - jnp.log2(1.0 + jnp.exp2(neg_abs))
    return log_sig, log_sig - zc


def _sb_prompt_kernel(bias_ref, q_ref, k_ref, v_ref, o_ref, acc_scr, carry_scr, *, scale, tq, tk):
    qi = pl.program_id(2)
    hps = acc_scr.shape[0]
    dh = acc_scr.shape[2]
    uo = _cumsum_weights()
    acc_scr[...] = jnp.zeros_like(acc_scr)
    carry_scr[...] = jnp.zeros_like(carry_scr)
    n_diag = tq // tk
    n_blk = tk // LANES

    def tile(ks, masked):
        for hh in range(hps):
            head_tile(hh, ks, masked)

    def head_tile(hh, ks, masked):
        cols = slice(hh * dh, (hh + 1) * dh)
        q = q_ref[:, cols].astype(BF16)
        bias = bias_ref[pl.program_id(1) * hps + hh] * LOG2_E
        kb = k_ref[pl.ds(ks, tk), cols].astype(BF16)
        vb = v_ref[pl.ds(ks, tk), cols].astype(BF16)
        zc = _dot_nt(q, kb) * (scale * LOG2_E) + bias
        log_sig, log_keep = _log2_sigmoid_pair(zc)
        if masked:
            q_pos = qi * tq + lax.broadcasted_iota(jnp.int32, (tq, tk), 0)
            k_pos = ks + lax.broadcasted_iota(jnp.int32, (tq, tk), 1)
            mask = k_pos < q_pos
            log_keep = jnp.where(mask, log_keep, 0.0)
        log_keep = log_keep.astype(BF16)
        carry = carry_scr[hh]
        tails = [None] * n_blk
        for c in reversed(range(n_blk)):
            cu = _dot(log_keep[:, c * LANES:(c + 1) * LANES], uo)
            tails[c] = cu[:, :LANES] + carry
            carry = carry + cu[:, LANES:]
        w = jnp.exp2(log_sig + jnp.concatenate(tails, axis=1))
        if masked:
            w = jnp.where(mask, w, 0.0)
        acc_scr[hh] += _dot(w.astype(BF16), vb)
        carry_scr[hh] = carry

    for j in range(n_diag):
        tile(pl.multiple_of((qi * n_diag + n_diag - 1 - j) * tk, tk), True)

    def body(it, _):
        tile(pl.multiple_of((qi * n_diag - 1 - it) * tk, tk), False)
        return 0

    lax.fori_loop(0, qi * n_diag, body, 0)
    for hh in range(hps):
        o_ref[:, hh * dh:(hh + 1) * dh] = acc_scr[hh].astype(o_ref.dtype)


def sb_attend_prompt(q, kv, layer, bias):
    b, t, _ = q.shape
    dh = LANES
    n_heads = bias.shape[0]
    hps = SB_HEADS_PER_STEP
    tq = min(t, SB_Q_TILE)
    tk = min(tq, SB_K_TILE)
    kern = functools.partial(_sb_prompt_kernel, scale=dh ** -0.5, tq=tq, tk=tk)
    kv_spec = pl.BlockSpec((None, None, t, hps * dh), lambda bi, h, qi: (layer, bi, 0, h))
    return pl.pallas_call(
        kern,
        grid=(b, n_heads // hps, t // tq),
        in_specs=[pl.BlockSpec(memory_space=pltpu.SMEM),
                  pl.BlockSpec((None, tq, hps * dh), lambda bi, h, qi: (bi, qi, h)),
                  kv_spec, kv_spec],
        out_specs=pl.BlockSpec((None, tq, hps * dh), lambda bi, h, qi: (bi, qi, h)),
        out_shape=jax.ShapeDtypeStruct((b, t, n_heads * dh), BF16),
        scratch_shapes=[pltpu.VMEM((hps, tq, dh), F32), pltpu.VMEM((hps, tq, LANES), F32)],
        compiler_params=_cparams("parallel", "parallel", "arbitrary"),
        name="sb_attend_prompt",
    )(bias, q, kv[0], kv[1])


def _suffix_sum_strided(x, stride):
    w = x.shape[1]
    lane = lax.broadcasted_iota(jnp.int32, x.shape, 1)
    s = stride
    while s < w:
        if s % LANES == 0:
            shifted = jnp.concatenate([x[:, s:], jnp.zeros((x.shape[0], s), F32)], axis=1)
        else:
            shifted = jnp.where(lane < w - s, pltpu.roll(x, w - s, axis=1), 0.0)
        x = x + shifted
        s *= 2
    return x


def _sb_sample_kernel(pt_ref, q_ref, bias_ref, kn_ref, vn_ref, *rest, scale, n_heads, n_new, pages_per_step):
    k_refs = rest[:pages_per_step]
    v_refs = rest[pages_per_step:2 * pages_per_step]
    o_ref, z_scr, w_scr, acc_scr, carry_scr = rest[2 * pages_per_step:]
    s = pl.program_id(1)
    width = kn_ref.shape[0]
    rows = n_new * n_heads
    lane_h = lax.broadcasted_iota(jnp.int32, (n_heads, width), 1) & (n_heads - 1)
    same_head = lane_h == lax.broadcasted_iota(jnp.int32, (n_heads, width), 0)
    row = lax.broadcasted_iota(jnp.int32, (2 * n_new, width), 0)
    lane = lax.broadcasted_iota(jnp.int32, (2 * n_new, width), 1)
    row128 = lax.broadcasted_iota(jnp.int32, (2 * n_new, LANES), 0)
    lane128 = lax.broadcasted_iota(jnp.int32, (2 * n_new, LANES), 1)
    q = q_ref[...].astype(BF16)

    def process(slot, blocks, mask):
        for half, (k_blk, _) in enumerate(blocks):
            zfull = _dot_nt(q, k_blk.astype(BF16))
            for i in range(n_new):
                part = zfull[i * n_heads:(i + 1) * n_heads, :]
                r = half * n_new + i
                z_scr[slot, r:r + 1, :] = jnp.sum(jnp.where(same_head, part, 0.0), axis=0, keepdims=True)
        zc = z_scr[slot] * (scale * LOG2_E) + bias_ref[...] * LOG2_E
        log_sig, log_keep = _log2_sigmoid_pair(zc)
        if mask is not None:
            log_keep = jnp.where(mask, log_keep, 0.0)
        incl = _suffix_sum_strided(log_keep, n_heads)
        tot = jnp.where(lane128 < n_heads, incl[:, :LANES], 0.0)
        shift = n_heads
        while shift < LANES:
            tot = tot + pltpu.roll(tot, shift, axis=1)
            shift *= 2
        tot_other = pltpu.roll(tot, n_new, axis=0)
        carry = carry_scr[...] + jnp.where(row128 >= n_new, tot_other, 0.0)
        w = jnp.exp2(log_sig + (incl - log_keep) + jnp.concatenate([carry] * (width // LANES), axis=1))
        if mask is not None:
            w = jnp.where(mask, w, 0.0)
        pv = None
        for half, (_, v_blk) in enumerate(blocks):
            for i in range(n_new):
                r = half * n_new + i
                w_scr[slot, half * rows + i * n_heads:half * rows + (i + 1) * n_heads, :] = jnp.where(
                    same_head, jnp.broadcast_to(w[r:r + 1, :], (n_heads, width)), 0.0).astype(BF16)
            d = _dot(w_scr[slot, half * rows:(half + 1) * rows, :], v_blk.astype(BF16))
            pv = d if pv is None else pv + d
        acc_scr[...] += pv
        carry_scr[...] += tot + tot_other

    @pl.when(s == 0)
    def _():
        z_scr[...] = jnp.zeros_like(z_scr)
        acc_scr[...] = jnp.zeros_like(acc_scr)
        carry_scr[...] = jnp.zeros_like(carry_scr)
        process(0, [(kn_ref[...], vn_ref[...])], (row < n_new) & (lane < row * n_heads))

    for p in range(pages_per_step // 2):
        process(p, [(k_refs[2 * p][...], v_refs[2 * p][...]), (k_refs[2 * p + 1][...], v_refs[2 * p + 1][...])],
                None)

    @pl.when(s == pl.num_programs(1) - 1)
    def _():
        o_ref[...] = acc_scr[...]


def sb_attend_sample(q, k_new, v_new, cache_k, cache_v, page_table, bias, layer):
    n_seq, n_new, n_heads, dh = q.shape
    assert 2 * n_new == SUBLANES, "two pages of n_new query rows share one 8-sublane array"
    n_pages = page_table.shape[1]
    n_attn, n_pool, page = cache_k.shape[:3]
    width = page * n_heads
    pps = SB_PAGES_PER_STEP
    rows = n_new * n_heads
    pad = ((0, 0), (0, page - n_new), (0, 0), (0, 0))
    k_pad = jnp.pad(k_new, pad).reshape(n_seq, width, dh)
    v_pad = jnp.pad(v_new, pad).reshape(n_seq, width, dh)
    ck = cache_k.reshape(n_attn, n_pool, width, dh)
    cv = cache_v.reshape(n_attn, n_pool, width, dh)
    bias_lanes = jnp.tile(bias.astype(F32), page).reshape(1, width)

    def page_spec(j):
        return pl.BlockSpec((None, None, width, dh),
                            lambda b, s, pt: (layer, pt[b, n_pages - 1 - (s * pps + j)], 0, 0))

    seq_spec = pl.BlockSpec((None, rows, dh), lambda b, s, pt: (b, 0, 0))
    new_spec = pl.BlockSpec((None, width, dh), lambda b, s, pt: (b, 0, 0))
    kern = functools.partial(_sb_sample_kernel, scale=dh ** -0.5, n_heads=n_heads, n_new=n_new,
                             pages_per_step=pps)
    grid_spec = pltpu.PrefetchScalarGridSpec(
        num_scalar_prefetch=1,
        grid=(n_seq, n_pages // pps),
        in_specs=[seq_spec, pl.BlockSpec((1, width), lambda b, s, pt: (0, 0)), new_spec, new_spec]
                 + [page_spec(j) for j in range(pps)] * 2,
        out_specs=seq_spec,
        scratch_shapes=[pltpu.VMEM((pps // 2, 2 * n_new, width), F32),
                        pltpu.VMEM((pps // 2, 2 * rows, width), BF16),
                        pltpu.VMEM((rows, dh), F32), pltpu.VMEM((2 * n_new, LANES), F32)],
    )
    out = pl.pallas_call(
        kern,
        grid_spec=grid_spec,
        out_shape=jax.ShapeDtypeStruct((n_seq, rows, dh), F32),
        compiler_params=_cparams("parallel", "arbitrary"),
        name="sb_attend_sample",
    )(page_table, q.reshape(n_seq, rows, dh), bias_lanes, k_pad, v_pad, *([ck] * pps), *([cv] * pps))
    return out.reshape(n_seq, n_new, n_heads * dh)


def _ml_gates_kernel(u_ref, wg_ref, b_ref, o_ref, *, n_heads):
    pre = _dot_nt(wg_ref[...].astype(BF16), u_ref[...]) + b_ref[...]
    pre = GATE_SOFTCAP * jnp.tanh(pre / GATE_SOFTCAP)
    o_ref[0:n_heads, :] = pre[0:n_heads]
    o_ref[n_heads:, :] = -_softplus(-pre[n_heads:])


def ml_gates(u, wg_t, b_gate):
    m, d = u.shape
    g = wg_t.shape[0]
    tm = min(m, ROW_TILE)
    return pl.pallas_call(
        functools.partial(_ml_gates_kernel, n_heads=g // 2),
        grid=(m // tm,),
        in_specs=[pl.BlockSpec((tm, d), lambda i: (i, 0)),
                  pl.BlockSpec((g, d), lambda i: (0, 0)),
                  pl.BlockSpec((g, 1), lambda i: (0, 0))],
        out_specs=pl.BlockSpec((g, tm), lambda i: (0, i)),
        out_shape=jax.ShapeDtypeStruct((g, m), F32),
        compiler_params=_cparams("parallel"),
        name="ml_gates",
    )(u, wg_t, b_gate)


def mlstm_gate_layout(g, b, t, n_heads):
    hps = ML_HEADS_PER_STEP
    t_pad = -(-t // CHUNK) * CHUNK
    g = jnp.pad(g.reshape(2, n_heads // hps, hps, b, t), ((0, 0),) * 4 + ((0, t_pad - t),))
    g = g.reshape(2, n_heads // hps, hps, b, t_pad // CHUNK, CHUNK).transpose(3, 4, 1, 2, 0, 5)
    return g.reshape(b, t_pad // CHUNK, n_heads // hps, 2 * hps, CHUNK)


def _lane_cumsum(x):
    lane = lax.broadcasted_iota(jnp.int32, x.shape, 1)
    shift = 1
    while shift < x.shape[1]:
        x = x + jnp.where(lane >= shift, pltpu.roll(x, shift, axis=1), 0.0)
        shift *= 2
    return x


def _mlstm_kernel(g_ref, q_ref, k_ref, v_ref, og_ref, gn_ref, c0_ref, n0_ref, m0_ref,
                  y_ref, c_out, n_out, m_out, c_scr, n_scr, m_scr, *, n_valid, k_scale):
    ci = pl.program_id(2)
    L = g_ref.shape[1]
    t_in = q_ref.shape[0]
    hps = c_scr.shape[0]
    dk = c_scr.shape[1]
    dv = c_scr.shape[2]

    @pl.when(ci == 0)
    def _():
        c_scr[...] = c0_ref[...]
        n_scr[...] = n0_ref[...]
        m_scr[...] = m0_ref[...]

    g = g_ref[...]
    row8 = lax.broadcasted_iota(jnp.int32, (2 * hps, L), 0)
    if n_valid < L:
        valid = lax.broadcasted_iota(jnp.int32, (2 * hps, L), 1) < n_valid
        g = jnp.where(valid, g, jnp.where((row8 & 1) == 0, NEG_BIG, 0.0))
    bc = _lane_cumsum(g)
    rows = jnp.concatenate([jnp.where((row8 & 1) == 0, g, bc), jnp.zeros((L - 2 * hps, L), F32)], axis=0)
    cols = rows.T
    t_idx = lax.broadcasted_iota(jnp.int32, (L, L), 0)
    s_idx = lax.broadcasted_iota(jnp.int32, (L, L), 1)
    causal = s_idx <= t_idx

    for hh in range(hps):
        li_r = g[2 * hh:2 * hh + 1, :]
        bc_r = bc[2 * hh + 1:2 * hh + 2, :]
        li_c = cols[:, 2 * hh:2 * hh + 1]
        bc_c = cols[:, 2 * hh + 1:2 * hh + 2]
        m_prev = m_scr[hh, 0:1, 0:1]
        dmat = jnp.where(causal, bc_c + (li_r - bc_r), NEG_BIG)
        inter = bc_c + m_prev
        m_t = jnp.maximum(inter, jnp.max(dmat, axis=1, keepdims=True))

        qf = _pad_rows(q_ref[:, hh * dk:(hh + 1) * dk], L)
        q = qf.astype(BF16)
        ks_f = _pad_rows(k_ref[:, hh * dk:(hh + 1) * dk], L) * k_scale
        v = _pad_rows(v_ref[:, hh * dv:(hh + 1) * dv], L).astype(BF16)
        a = jnp.exp(dmat - m_t) * _dot_nt(q, ks_f.astype(BF16))
        sc = jnp.exp(inter - m_t)
        c = c_scr[hh]
        n_row = n_scr[hh]
        num = _dot(a.astype(BF16), v) + sc * _dot(q, c.astype(BF16))
        den = jnp.sum(a, axis=1, keepdims=True) + sc * jnp.sum(qf * n_row, axis=1, keepdims=True)
        hval = num / jnp.maximum(jnp.abs(den), jnp.exp(-m_t))
        gate = jax.nn.sigmoid(og_ref[:, hh * dv:(hh + 1) * dv])
        y_ref[:, hh * dv:(hh + 1) * dv] = (gate * _rms(hval[:t_in], gn_ref[...])).astype(y_ref.dtype)

        m_new = m_t[L - 1:L, :]
        bc_last = bc_c[L - 1:L, :]
        wl = jnp.exp(bc_last - bc_c + li_c - m_new)
        sl = jnp.exp(bc_last + m_prev - m_new)
        kw = ks_f * wl
        c_scr[hh] = sl * c + _dot(kw.T.astype(BF16), v)
        n_scr[hh] = sl * n_row + jnp.sum(kw, axis=0, keepdims=True)
        m_scr[hh] = jnp.broadcast_to(m_new, (1, LANES))

    @pl.when(ci == pl.num_programs(2) - 1)
    def _():
        c_out[...] = c_scr[...]
        n_out[...] = n_scr[...]
        m_out[...] = m_scr[...]


def mlstm_scan(proj, gates, out_gain, c0, n0, m0, n_heads, n_valid):
    b, t, _ = proj.shape
    L = min(t, CHUNK)
    dk = c0.shape[2]
    dv = c0.shape[3]
    hps = ML_HEADS_PER_STEP
    n_grp = n_heads // hps
    v_blk = 2 * n_heads * dk // (hps * dv)
    kern = functools.partial(_mlstm_kernel, n_valid=n_valid, k_scale=dk ** -0.5)
    state_specs = [pl.BlockSpec((None, hps, dk, dv), lambda bi, h, ci: (bi, h, 0, 0)),
                   pl.BlockSpec((None, hps, 1, dk), lambda bi, h, ci: (bi, h, 0, 0)),
                   pl.BlockSpec((None, hps, 1, LANES), lambda bi, h, ci: (bi, h, 0, 0))]
    return pl.pallas_call(
        kern,
        grid=(b, n_grp, t // L),
        in_specs=[pl.BlockSpec((None, None, None, 2 * hps, CHUNK), lambda bi, h, ci: (bi, ci, h, 0, 0)),
                  pl.BlockSpec((None, L, hps * dk), lambda bi, h, ci: (bi, ci, h)),
                  pl.BlockSpec((None, L, hps * dk), lambda bi, h, ci: (bi, ci, n_grp + h)),
                  pl.BlockSpec((None, L, hps * dv), lambda bi, h, ci: (bi, ci, v_blk + h)),
                  pl.BlockSpec((None, L, hps * dv), lambda bi, h, ci: (bi, ci, v_blk + n_grp + h)),
                  pl.BlockSpec((1, dv), lambda bi, h, ci: (0, 0))] + state_specs,
        out_specs=[pl.BlockSpec((None, L, hps * dv), lambda bi, h, ci: (bi, ci, h))] + state_specs,
        out_shape=[jax.ShapeDtypeStruct((b, t, n_heads * dv), BF16),
                   jax.ShapeDtypeStruct(c0.shape, F32),
                   jax.ShapeDtypeStruct(n0.shape, F32),
                   jax.ShapeDtypeStruct(m0.shape, F32)],
        scratch_shapes=[pltpu.VMEM((hps, dk, dv), F32), pltpu.VMEM((hps, 1, dk), F32),
                        pltpu.VMEM((hps, 1, LANES), F32)],
        compiler_params=_cparams("parallel", "parallel", "arbitrary"),
        name="mlstm_scan",
    )(gates, proj, proj, proj, proj, out_gain, c0, n0, m0)


def _hgrn_tables(c):
    n_lev = int(np.log2(c))
    lev = np.zeros(((n_lev + 1) * c, c), np.float32)
    lev[:c] = np.tril(np.ones((c, c), np.float32))
    for l in range(1, n_lev + 1):
        m = 1 << (l - 1)
        for t in range(c):
            mid = (t >> l << l) + m
            if t & m:
                lev[l * c + t, mid:t + 1] = 1.0
            else:
                lev[l * c + t, t + 1:mid] = 1.0
    t = np.arange(c)[:, None]
    s = np.arange(c)[None, :]
    x = t ^ s
    pair_level = np.where(s > t, -1, np.where(s == t, 0, np.floor(np.log2(np.maximum(x, 1))).astype(np.int32) + 1))
    return jnp.asarray(lev, BF16), jnp.asarray(pair_level, jnp.int32)


def _hgrn_kernel(lev_ref, lvl_ref, q_ref, f_ref, i_ref, g_ref, lb_ref, gn_ref, s0_ref,
                 y_ref, s_out, st_scr, *, n_valid):
    ci = pl.program_id(2)
    c = lev_ref.shape[1]
    t_in = q_ref.shape[0]
    n_lev = lev_ref.shape[0] // c - 1
    hps = st_scr.shape[0]
    dh = st_scr.shape[1]

    @pl.when(ci == 0)
    def _():
        for hh in range(hps):
            st_scr[hh] = s0_ref[hh].T

    fp = _pad_rows(f_ref[...], c)
    lb = lb_ref[...]
    log_sig = -_softplus(-fp)
    la = jnp.log(lb)
    lbb = jnp.log1p(-lb) + log_sig
    lf = jnp.maximum(la, lbb) + jnp.log1p(jnp.exp(-jnp.abs(la - lbb)))
    kk = (1.0 - lb) * jax.nn.sigmoid(-fp)
    if n_valid < c:
        valid = lax.broadcasted_iota(jnp.int32, (c, 1), 0) < n_valid
        lf = jnp.where(valid, lf, 0.0)
        kk = jnp.where(valid, kk, 0.0)

    hi, lo = _split_bf16(lf)
    lev = lev_ref[...]
    d_all = _dot(lev, hi) + _dot(lev, lo)
    bcum = d_all[0:c]
    q = _pad_rows(q_ref[...], c)
    iv_f = _pad_rows(i_ref[...], c)
    lvl = lvl_ref[...]
    heads = [slice(hh * dh, (hh + 1) * dh) for hh in range(hps)]
    qb = q.astype(BF16)
    kb = kk.astype(BF16)
    a = [jnp.where(lvl == 0, _dot_nt(qb[:, sl], kb[:, sl]), 0.0) for sl in heads]
    for l in range(1, n_lev + 1):
        e = jnp.exp(d_all[l * c:(l + 1) * c])
        qe = (q * e).astype(BF16)
        ke = (kk * e).astype(BF16)
        a = [a[hh] + jnp.where(lvl == l, _dot_nt(qe[:, sl], ke[:, sl]), 0.0) for hh, sl in enumerate(heads)]
    blast = bcum[c - 1:c, :]
    q_in = (q * jnp.exp(bcum)).astype(BF16)
    k_out = (kk * jnp.exp(blast - bcum)).astype(BF16)
    decay = jnp.exp(blast)
    gate = jax.nn.sigmoid(g_ref[...])
    for hh, sl in enumerate(heads):
        st = st_scr[hh]
        o = _dot(a[hh].astype(BF16), iv_f[:, sl].astype(BF16)) + _dot_nt(q_in[:, sl], st.astype(BF16))
        y_ref[:, sl] = (gate[:, sl] * _rms(o[:t_in], gn_ref[...])).astype(y_ref.dtype)
        st_scr[hh] = decay[:, sl] * st + _dot(iv_f[:, sl].T.astype(BF16), k_out[:, sl])

    @pl.when(ci == pl.num_programs(2) - 1)
    def _():
        for hh in range(hps):
            s_out[hh] = st_scr[hh].T


def hgrn_scan(proj, lower_bound, out_gain, s0, n_heads, n_valid):
    b, t, _ = proj.shape
    c = min(t, CHUNK)
    dh = s0.shape[2]
    hps = HG_HEADS_PER_STEP
    n_grp = n_heads // hps
    lev, pair_level = _hgrn_tables(CHUNK)
    kern = functools.partial(_hgrn_kernel, n_valid=n_valid)

    def part(p):
        return pl.BlockSpec((None, c, hps * dh), lambda bi, h, ci: (bi, ci, p * n_grp + h))

    state_spec = pl.BlockSpec((None, hps, dh, dh), lambda bi, h, ci: (bi, h, 0, 0))
    return pl.pallas_call(
        kern,
        grid=(b, n_grp, t // c),
        in_specs=[pl.BlockSpec(lev.shape, lambda bi, h, ci: (0, 0)),
                  pl.BlockSpec(pair_level.shape, lambda bi, h, ci: (0, 0)),
                  part(0), part(1), part(2), part(3),
                  pl.BlockSpec((1, hps * dh), lambda bi, h, ci: (0, h)),
                  pl.BlockSpec((1, dh), lambda bi, h, ci: (0, 0)),
                  state_spec],
        out_specs=[pl.BlockSpec((None, c, hps * dh), lambda bi, h, ci: (bi, ci, h)), state_spec],
        out_shape=[jax.ShapeDtypeStruct((b, t, n_heads * dh), BF16),
                   jax.ShapeDtypeStruct(s0.shape, F32)],
        scratch_shapes=[pltpu.VMEM((hps, dh, dh), F32)],
        compiler_params=_cparams("parallel", "parallel", "arbitrary"),
        name="hgrn_scan",
    )(lev, pair_level, proj, proj, proj, proj, lower_bound, out_gain, s0)


def _pad_tokens(x, t_pad):
    return jnp.pad(x, ((0, 0), (0, t_pad - x.shape[1])) + ((0, 0),) * (x.ndim - 2))


def kernel(x_prompt, x_sample, cache_k, cache_v, state_mlstm_c, state_mlstm_n, state_mlstm_m,
           state_hgrn, page_table, p_prompt, p_sample, norm_mix, norm_ffn, norm_ple,
           sb_w_qkv, sb_q_norm, sb_k_norm, sb_logit_bias, sb_w_out, ml_w_in, ml_b_gate,
           ml_out_norm, ml_w_out, hg_w_in, hg_lb_logits, hg_out_norm, hg_w_out,
           ffn_w_up, ffn_w_down, ple_w_proj, ple_w_gate):
    depth, d = norm_mix.shape
    bp, tp, _ = x_prompt.shape
    bs, ts, _ = x_sample.shape
    sb_heads = sb_logit_bias.shape[1]
    ml_heads = state_mlstm_c.shape[2]
    ml_dk = state_mlstm_c.shape[3]
    ml_dv = state_mlstm_c.shape[4]
    hg_heads = state_hgrn.shape[2]
    hg_dh = state_hgrn.shape[3]
    n_attn, n_pool, page = cache_k.shape[:3]
    ml_qk_w = ml_heads * ml_dk

    g_mix = norm_mix.reshape(depth, 1, d)
    g_ffn = norm_ffn.reshape(depth, 1, d)
    g_ple = norm_ple.reshape(depth, 1, d)
    lb_soft = jax.nn.softmax(hg_lb_logits.astype(F32), axis=0)
    lower_bounds = jnp.cumsum(lb_soft, axis=0) - lb_soft

    hp = x_prompt.reshape(bp * tp, d)
    hs = x_sample.reshape(bs * ts, d)
    pp_all = p_prompt.reshape(depth, bp * tp, -1)
    ps_all = p_sample.reshape(depth, bs * ts, -1)

    outs = {k: [] for k in ("ks", "vs", "mcp", "mnp", "mmp", "mcs", "mns", "mms", "hgp", "hgs")}
    kp_all = vp_all = None
    i_attn = i_ml = i_hg = 0
    for layer in range(depth):
        kind = layer % N_MIXERS
        up = rmsnorm_bf16(hp, g_mix, layer)
        us = rmsnorm_bf16(hs, g_mix, layer)
        if kind == 0:
            j = i_attn
            i_attn += 1
            gq = sb_q_norm[j].reshape(1, LANES)
            gk = sb_k_norm[j].reshape(1, LANES)
            q_p, q_s = linear("headnorm", up, sb_w_qkv, j, 0, d, shared=(gq,), side=(us, ()))
            kp_all, k_s = linear("headnorm", up, sb_w_qkv, j, d, d, shared=(gk,), side=(us, ()),
                                 slab=(n_attn, j, kp_all))
            vp_all, v_s = linear("plain", up, sb_w_qkv, j, 2 * d, d, side=(us, ()), slab=(n_attn, j, vp_all))
            q_s, k_s, v_s = (a.reshape(bs, ts, sb_heads, LANES) for a in (q_s, k_s, v_s))
            outs["ks"].append(k_s)
            outs["vs"].append(v_s)
            kv = (kp_all.reshape(n_attn, bp, tp, d), vp_all.reshape(n_attn, bp, tp, d))
            ap = sb_attend_prompt(q_p.reshape(bp, tp, d), kv, j, sb_logit_bias[j]).reshape(bp * tp, d)
            a_s = sb_attend_sample(q_s, k_s, v_s, cache_k, cache_v, page_table,
                                   sb_logit_bias[j], j).reshape(bs * ts, d)
            hp, hs = linear("residual", ap, sb_w_out, j, 0, d, per=(hp,), side=(a_s, (hs,)))
        elif kind == 1:
            j = i_ml
            i_ml += 1
            wg_t = ml_w_in[j][:, 2 * ml_qk_w + 2 * d:].T
            b_gate = ml_b_gate[j].reshape(-1, 1)
            gn = ml_out_norm[j].reshape(1, ml_dv)
            n_main = 2 * ml_qk_w + 2 * d
            proj_p, proj_s = linear("plain", up, ml_w_in, j, 0, n_main, side=(us, ()))
            proj_p = proj_p.reshape(bp, tp, n_main)
            proj_s = proj_s.reshape(bs, ts, n_main)
            gates_p = mlstm_gate_layout(ml_gates(up, wg_t, b_gate), bp, tp, ml_heads)
            gates_s = mlstm_gate_layout(ml_gates(us, wg_t, b_gate), bs, ts, ml_heads)
            zc = jnp.zeros((bp, ml_heads, ml_dk, ml_dv), F32)
            zn = jnp.zeros((bp, ml_heads, 1, ml_dk), F32)
            zm = jnp.zeros((bp, ml_heads, 1, LANES), F32)
            yp, cp, npv, mp = mlstm_scan(proj_p, gates_p, gn, zc, zn, zm, ml_heads, CHUNK)
            m0 = jnp.broadcast_to(state_mlstm_m[j][:, :, None, None], (bs, ml_heads, 1, LANES))
            ys, cs, nsv, ms = mlstm_scan(_pad_tokens(proj_s, SUBLANES), gates_s, gn, state_mlstm_c[j],
                                         state_mlstm_n[j][:, :, None, :], m0, ml_heads, ts)
            outs["mcp"].append(cp)
            outs["mnp"].append(npv[:, :, 0, :])
            outs["mmp"].append(mp[:, :, 0, 0])
            outs["mcs"].append(cs)
            outs["mns"].append(nsv[:, :, 0, :])
            outs["mms"].append(ms[:, :, 0, 0])
            hp, hs = linear("residual", yp.reshape(bp * tp, d), ml_w_out, j, 0, d, per=(hp,),
                            side=(ys[:, :ts].reshape(bs * ts, d), (hs,)))
        else:
            j = i_hg
            i_hg += 1
            lb = lower_bounds[layer].reshape(1, d)
            gn = hg_out_norm[j].reshape(1, hg_dh)
            proj_p, proj_s = linear("plain", up, hg_w_in, j, 0, 4 * d, side=(us, ()))
            proj_p = proj_p.reshape(bp, tp, 4 * d)
            proj_s = proj_s.reshape(bs, ts, 4 * d)
            zs = jnp.zeros((bp, hg_heads, hg_dh, hg_dh), F32)
            yp, sp = hgrn_scan(proj_p, lb, gn, zs, hg_heads, CHUNK)
            ys, ss = hgrn_scan(_pad_tokens(proj_s, SUBLANES), lb, gn, state_hgrn[j], hg_heads, ts)
            outs["hgp"].append(sp)
            outs["hgs"].append(ss)
            hp, hs = linear("residual", yp.reshape(bp * tp, d), hg_w_out, j, 0, d, per=(hp,),
                            side=(ys[:, :ts].reshape(bs * ts, d), (hs,)))
        a_p, a_s = linear("relu2", rmsnorm_bf16(hp, g_ffn, layer), ffn_w_up, layer, 0, ffn_w_up.shape[2],
                          side=(rmsnorm_bf16(hs, g_ffn, layer), ()), out_dtype=BF16)
        hp, hs = linear_residual_acc(a_p, ffn_w_down, layer, hp, a_s, hs)
        hp, hs = linear("ple", rmsnorm_bf16(hp, g_ple, layer), ple_w_gate, layer, 0, d, shared=(ple_w_proj,),
                        per=(pp_all, hp), side=(rmsnorm_bf16(hs, g_ple, layer), (ps_all, hs)),
                        col_tile=PLE_COL_TILE)

    st = {k: jnp.stack(v) for k, v in outs.items()}
    kv_shape = (n_attn, bp, tp, sb_heads, LANES)
    return (hp.reshape(bp, tp, d), hs.reshape(bs, ts, d), kp_all.reshape(kv_shape), vp_all.reshape(kv_shape),
            st["ks"], st["vs"],
            st["mcp"], st["mnp"], st["mmp"], st["mcs"], st["mns"], st["mms"], st["hgp"], st["hgs"])
```

```python
import functools

import numpy as np
import jax
import jax.numpy as jnp
from jax import lax
from jax.experimental import pallas as pl
from jax.experimental.pallas import tpu as pltpu

F32 = jnp.float32
BF16 = jnp.bfloat16

NORM_EPS = 1e-6
N_MIXERS = 3
GATE_SOFTCAP = 15.0
LANES = 128
SUBLANES = 8
VMEM_LIMIT_BYTES = 56 * 1024 * 1024
ROW_TILE = 2048
COL_TILE = 512
PLE_COL_TILE = 256
ACC_ROW_TILE = 1024
ACC_COL_TILE = 1024
ACC_K_TILE = 2048
SB_Q_TILE = 512
SB_K_TILE = 512
SB_HEADS_PER_STEP = 4
SB_PAGES_PER_STEP = 8
CHUNK = 128
ML_HEADS_PER_STEP = 8
HG_HEADS_PER_STEP = 8
NEG_BIG = -1e30
LOG2_E = 1.4426950408889634


def _cparams(*sem):
    return pltpu.CompilerParams(dimension_semantics=sem, vmem_limit_bytes=VMEM_LIMIT_BYTES)


def _dot(a, b):
    return jnp.dot(a, b, preferred_element_type=F32)


def _dot_nt(a, b):
    return lax.dot_general(a, b, (((1,), (1,)), ((), ())), preferred_element_type=F32)


def _split_bf16(x):
    hi = x.astype(BF16)
    lo = (x - hi.astype(F32)).astype(BF16)
    return hi, lo


def _softplus(z):
    return jnp.maximum(z, 0.0) + jnp.log1p(jnp.exp(-jnp.abs(z)))


def _pad_rows(x, rows):
    if x.shape[0] == rows:
        return x
    return jnp.concatenate([x, jnp.zeros((rows - x.shape[0], x.shape[1]), x.dtype)], axis=0)


def _rms(x, gain):
    ms = jnp.mean(x * x, axis=-1, keepdims=True)
    return x * lax.rsqrt(ms + NORM_EPS) * gain


def _rmsnorm_kernel(x_ref, g_ref, o_ref):
    o_ref[...] = _rms(x_ref[...], g_ref[...]).astype(o_ref.dtype)


def rmsnorm_bf16(x, gains, layer):
    m, d = x.shape
    tm = min(m, 512)
    return pl.pallas_call(
        _rmsnorm_kernel,
        grid=(m // tm,),
        in_specs=[pl.BlockSpec((tm, d), lambda i: (i, 0)),
                  pl.BlockSpec((None, 1, d), lambda i: (layer, 0, 0))],
        out_specs=pl.BlockSpec((tm, d), lambda i: (i, 0)),
        out_shape=jax.ShapeDtypeStruct((m, d), BF16),
        compiler_params=_cparams("parallel"),
        name="rmsnorm",
    )(x, gains)


_LINEAR_INPUTS = {"plain": (0, 0), "headnorm": (1, 0), "residual": (0, 1), "relu2": (0, 0), "ple": (1, 2)}


def _linear_epilogue(mode, acc, shared, wp, per, cols, dtype):
    if mode == "plain":
        return acc
    if mode == "headnorm":
        return jnp.concatenate([_rms(acc[:, g * LANES:(g + 1) * LANES], shared[0][...])
                                for g in range(acc.shape[1] // LANES)], axis=1)
    if mode == "residual":
        return per[0][:, cols] + acc
    if mode == "relu2":
        a = jnp.maximum(acc, 0.0)
        return (a * a).astype(dtype)
    return per[1][:, cols] + jax.nn.sigmoid(acc) * _dot(per[0][...].astype(BF16), wp)


def _linear_kernel(*refs, mode, has_side, has_norm):
    n_shared, n_per = _LINEAR_INPUTS[mode]
    n_groups = 2 if has_side else 1
    n_scratch = 1 if has_norm else 0
    w = refs[0][...].astype(BF16)
    shared = refs[1:1 + n_shared]
    pos = 1 + n_shared
    if has_norm:
        gain_ref = refs[pos]
        pos += 1
    groups = [refs[pos + g * (1 + n_per):pos + (g + 1) * (1 + n_per)] for g in range(n_groups)]
    outs = refs[len(refs) - n_groups - n_scratch:len(refs) - n_scratch]
    wp = shared[0][...].astype(BF16) if mode == "ple" else None
    tn = w.shape[1]

    x_ref, *per = groups[0]
    if has_norm:
        u_scr = refs[-1]

        @pl.when(pl.program_id(1) == 0)
        def _():
            u_scr[...] = _rms(x_ref[...], gain_ref[...]).astype(BF16)

        xv = u_scr[...]
    else:
        xv = x_ref[...].astype(BF16)
    acc = _dot(xv, w)
    outs[0][...] = _linear_epilogue(mode, acc, shared, wp, per, slice(None), outs[0].dtype)

    if has_side:
        @pl.when(pl.program_id(0) == 0)
        def _():
            xs_ref, *per_s = groups[1]
            cols = pl.ds(pl.multiple_of(pl.program_id(1) * tn, tn), tn)
            xs = _rms(xs_ref[...], gain_ref[...]) if has_norm else xs_ref[...]
            acc_s = _dot(xs.astype(BF16), w)
            outs[1][:, cols] = _linear_epilogue(mode, acc_s, shared, wp, per_s, cols, outs[1].dtype)


def linear(mode, x, w_all, layer, col_start, n_cols, shared=(), per=(), side=None, out_dtype=F32,
           col_tile=COL_TILE, slab=None, norm=None):
    m, k = x.shape
    tn = col_tile
    off = col_start // tn
    tm = min(m, ROW_TILE)
    in_specs = [pl.BlockSpec((None, k, tn), lambda i, j: (layer, 0, off + j))]
    if mode == "headnorm":
        in_specs.append(pl.BlockSpec((1, LANES), lambda i, j: (0, 0)))
    elif mode == "ple":
        in_specs.append(pl.BlockSpec((None, shared[0].shape[1], tn), lambda i, j: (layer, 0, j)))
    args = [w_all, *shared]
    scratch = []
    if norm is None:
        in_specs.append(pl.BlockSpec((tm, k), lambda i, j: (i, 0)))
    else:
        gains, g_idx = norm
        args.append(gains)
        in_specs.append(pl.BlockSpec((None, 1, k), lambda i, j: (g_idx, 0, 0)))
        in_specs.append(pl.BlockSpec((tm, k), lambda i, j: (i, 0), pipeline_mode=pl.Buffered(1)))
        scratch.append(pltpu.VMEM((tm, k), BF16))
    args += [x, *per]
    if mode == "residual":
        in_specs.append(pl.BlockSpec((tm, tn), lambda i, j: (i, j)))
    elif mode == "ple":
        in_specs.append(pl.BlockSpec((None, tm, per[0].shape[2]), lambda i, j: (layer, i, 0)))
        in_specs.append(pl.BlockSpec((tm, tn), lambda i, j: (i, j)))
    out_specs = [pl.BlockSpec((tm, tn), lambda i, j: (i, j))]
    out_shape = [jax.ShapeDtypeStruct((m, n_cols), out_dtype)]
    if side is not None:
        x_s, per_s = side
        ms = x_s.shape[0]
        args += [x_s, *per_s]
        in_specs.append(pl.BlockSpec((ms, k), lambda i, j: (0, 0)))
        if mode == "residual":
            in_specs.append(pl.BlockSpec((ms, n_cols), lambda i, j: (0, 0)))
        elif mode == "ple":
            in_specs.append(pl.BlockSpec((None, ms, per_s[0].shape[2]), lambda i, j: (layer, 0, 0)))
            in_specs.append(pl.BlockSpec((ms, n_cols), lambda i, j: (0, 0)))
        out_specs.append(pl.BlockSpec((ms, n_cols), lambda i, j: (0, 0)))
        out_shape.append(jax.ShapeDtypeStruct((ms, n_cols), out_dtype))
    aliases = {}
    if slab is not None:
        n_slabs, index, prev = slab
        out_specs[0] = pl.BlockSpec((None, tm, tn), lambda i, j: (index, i, j))
        out_shape[0] = jax.ShapeDtypeStruct((n_slabs, m, n_cols), out_dtype)
        if prev is not None:
            aliases = {len(args): 0}
            args.append(prev)
            in_specs.append(pl.BlockSpec(memory_space=pl.ANY))
    outs = pl.pallas_call(
        functools.partial(_linear_kernel, mode=mode, has_side=side is not None, has_norm=norm is not None),
        grid=(m // tm, n_cols // tn),
        in_specs=in_specs,
        out_specs=out_specs,
        out_shape=out_shape,
        scratch_shapes=scratch,
        input_output_aliases=aliases,
        compiler_params=_cparams("arbitrary", "arbitrary"),
        name="linear_" + mode,
    )(*args)
    return outs[0] if side is None else outs


def _linear_acc_kernel(w_ref, x_ref, r_ref, xs_ref, rs_ref, o_ref, os_ref):
    kk = pl.program_id(2)
    w = w_ref[...].astype(BF16)
    tk, tn = w.shape

    @pl.when(kk == 0)
    def _():
        o_ref[...] = r_ref[...]

    o_ref[...] += _dot(x_ref[...], w)

    @pl.when(pl.program_id(0) == 0)
    def _():
        cols = pl.ds(pl.multiple_of(pl.program_id(1) * tn, tn), tn)

        @pl.when(kk == 0)
        def _():
            os_ref[:, cols] = rs_ref[:, cols]

        os_ref[:, cols] += _dot(xs_ref[:, pl.ds(pl.multiple_of(kk * tk, tk), tk)], w)


def linear_residual_acc(x, w_all, layer, res, x_side, res_side):
    m, k = x.shape
    n = res.shape[1]
    ms = x_side.shape[0]
    tm = min(m, ACC_ROW_TILE)
    tn = ACC_COL_TILE
    tk = ACC_K_TILE
    return pl.pallas_call(
        _linear_acc_kernel,
        grid=(m // tm, n // tn, k // tk),
        in_specs=[pl.BlockSpec((None, tk, tn), lambda i, j, kk: (layer, kk, j)),
                  pl.BlockSpec((tm, tk), lambda i, j, kk: (i, kk)),
                  pl.BlockSpec((tm, tn), lambda i, j, kk: (i, j)),
                  pl.BlockSpec((ms, k), lambda i, j, kk: (0, 0)),
                  pl.BlockSpec((ms, n), lambda i, j, kk: (0, 0))],
        out_specs=[pl.BlockSpec((tm, tn), lambda i, j, kk: (i, j)),
                   pl.BlockSpec((ms, n), lambda i, j, kk: (0, 0))],
        out_shape=[jax.ShapeDtypeStruct((m, n), F32), jax.ShapeDtypeStruct((ms, n), F32)],
        compiler_params=_cparams("arbitrary", "arbitrary", "arbitrary"),
        name="linear_acc",
    )(w_all, x, res, x_side, res_side)


def _cumsum_weights():
    row = lax.broadcasted_iota(jnp.int32, (LANES, 2 * LANES), 0)
    col = lax.broadcasted_iota(jnp.int32, (LANES, 2 * LANES), 1)
    return jnp.where((col >= LANES) | (row > col), 1.0, 0.0).astype(BF16)


def _log2_sigmoid_pair(zc):
    neg_abs = lax.bitcast_convert_type(lax.bitcast_convert_type(zc, jnp.uint32) | jnp.uint32(0x80000000), F32)
    log_sig = jnp.minimum(zc, 0.0) - jnp.log2(1.0 + jnp.exp2(neg_abs))
    return log_sig, log_sig - zc


def _sb_prompt_kernel(bias_ref, q_ref, k_ref, v_ref, o_ref, acc_scr, carry_scr, *, scale, tq, tk):
    qi = pl.program_id(2)
    hps = acc_scr.shape[0]
    dh = acc_scr.shape[2]
    uo = _cumsum_weights()
    acc_scr[...] = jnp.zeros_like(acc_scr)
    carry_scr[...] = jnp.zeros_like(carry_scr)
    n_diag = tq // tk
    n_blk = tk // LANES

    def tile(ks, masked):
        for hh in range(hps):
            head_tile(hh, ks, masked)

    def head_tile(hh, ks, masked):
        cols = slice(hh * dh, (hh + 1) * dh)
        q = q_ref[:, cols].astype(BF16)
        bias = bias_ref[pl.program_id(1) * hps + hh] * LOG2_E
        kb = k_ref[pl.ds(ks, tk), cols].astype(BF16)
        vb = v_ref[pl.ds(ks, tk), cols].astype(BF16)
        zc = _dot_nt(q, kb) * (scale * LOG2_E) + bias
        log_sig, log_keep = _log2_sigmoid_pair(zc)
        if masked:
            q_pos = qi * tq + lax.broadcasted_iota(jnp.int32, (tq, tk), 0)
            k_pos = ks + lax.broadcasted_iota(jnp.int32, (tq, tk), 1)
            mask = k_pos < q_pos
            log_keep = jnp.where(mask, log_keep, 0.0)
        log_keep = log_keep.astype(BF16)
        carry = carry_scr[hh]
        tails = [None] * n_blk
        for c in reversed(range(n_blk)):
            cu = _dot(log_keep[:, c * LANES:(c + 1) * LANES], uo)
            tails[c] = cu[:, :LANES] + carry
            carry = carry + cu[:, LANES:]
        w = jnp.exp2(log_sig + jnp.concatenate(tails, axis=1))
        if masked:
            w = jnp.where(mask, w, 0.0)
        acc_scr[hh] += _dot(w.astype(BF16), vb)
        carry_scr[hh] = carry

    for j in range(n_diag):
        tile(pl.multiple_of((qi * n_diag + n_diag - 1 - j) * tk, tk), True)

    def body(it, _):
        tile(pl.multiple_of((qi * n_diag - 1 - it) * tk, tk), False)
        return 0

    lax.fori_loop(0, qi * n_diag, body, 0)
    for hh in range(hps):
        o_ref[:, hh * dh:(hh + 1) * dh] = acc_scr[hh].astype(o_ref.dtype)


def sb_attend_prompt(q, kv, layer, bias):
    b, t, _ = q.shape
    dh = LANES
    n_heads = bias.shape[0]
    hps = SB_HEADS_PER_STEP
    tq = min(t, SB_Q_TILE)
    tk = min(tq, SB_K_TILE)
    kern = functools.partial(_sb_prompt_kernel, scale=dh ** -0.5, tq=tq, tk=tk)
    kv_spec = pl.BlockSpec((None, None, t, hps * dh), lambda bi, h, qi: (layer, bi, 0, h))
    return pl.pallas_call(
        kern,
        grid=(b, n_heads // hps, t // tq),
        in_specs=[pl.BlockSpec(memory_space=pltpu.SMEM),
                  pl.BlockSpec((None, tq, hps * dh), lambda bi, h, qi: (bi, qi, h)),
                  kv_spec, kv_spec],
        out_specs=pl.BlockSpec((None, tq, hps * dh), lambda bi, h, qi: (bi, qi, h)),
        out_shape=jax.ShapeDtypeStruct((b, t, n_heads * dh), BF16),
        scratch_shapes=[pltpu.VMEM((hps, tq, dh), F32), pltpu.VMEM((hps, tq, LANES), F32)],
        compiler_params=_cparams("parallel", "parallel", "arbitrary"),
        name="sb_attend_prompt",
    )(bias, q, kv[0], kv[1])


def _suffix_sum_strided(x, stride):
    w = x.shape[1]
    lane = lax.broadcasted_iota(jnp.int32, x.shape, 1)
    s = stride
    while s < w:
        if s % LANES == 0:
            shifted = jnp.concatenate([x[:, s:], jnp.zeros((x.shape[0], s), F32)], axis=1)
        else:
            shifted = jnp.where(lane < w - s, pltpu.roll(x, w - s, axis=1), 0.0)
        x = x + shifted
        s *= 2
    return x


def _sb_sample_kernel(pt_ref, q_ref, bias_ref, kn_ref, vn_ref, *rest, scale, n_heads, n_new, pages_per_step):
    k_refs = rest[:pages_per_step]
    v_refs = rest[pages_per_step:2 * pages_per_step]
    o_ref, z_scr, w_scr, acc_scr, carry_scr = rest[2 * pages_per_step:]
    s = pl.program_id(1)
    width = kn_ref.shape[0]
    rows = n_new * n_heads
    lane_h = lax.broadcasted_iota(jnp.int32, (n_heads, width), 1) & (n_heads - 1)
    same_head = lane_h == lax.broadcasted_iota(jnp.int32, (n_heads, width), 0)
    row = lax.broadcasted_iota(jnp.int32, (2 * n_new, width), 0)
    lane = lax.broadcasted_iota(jnp.int32, (2 * n_new, width), 1)
    row128 = lax.broadcasted_iota(jnp.int32, (2 * n_new, LANES), 0)
    lane128 = lax.broadcasted_iota(jnp.int32, (2 * n_new, LANES), 1)
    q = q_ref[...].astype(BF16)

    def process(slot, blocks, mask, carry_in):
        for half, (k_blk, _) in enumerate(blocks):
            zfull = _dot_nt(q, k_blk.astype(BF16))
            for i in range(n_new):
                part = zfull[i * n_heads:(i + 1) * n_heads, :]
                r = half * n_new + i
                z_scr[slot, r:r + 1, :] = jnp.sum(jnp.where(same_head, part, 0.0), axis=0, keepdims=True)
        zc = z_scr[slot] * (scale * LOG2_E) + bias_ref[...] * LOG2_E
        log_sig, log_keep = _log2_sigmoid_pair(zc)
        if mask is not None:
            log_keep = jnp.where(mask, log_keep, 0.0)
        incl = _suffix_sum_strided(log_keep, n_heads)
        tot = jnp.where(lane128 < n_heads, incl[:, :LANES], 0.0)
        shift = n_heads
        while shift < LANES:
            tot = tot + pltpu.roll(tot, shift, axis=1)
            shift *= 2
        tot_other = pltpu.roll(tot, n_new, axis=0)
        carry = carry_in + jnp.where(row128 >= n_new, tot_other, 0.0)
        w = jnp.exp2(log_sig + (incl - log_keep) + jnp.concatenate([carry] * (width // LANES), axis=1))
        if mask is not None:
            w = jnp.where(mask, w, 0.0)
        pv = None
        for half, (_, v_blk) in enumerate(blocks):
            for i in range(n_new):
                r = half * n_new + i
                w_scr[slot, half * rows + i * n_heads:half * rows + (i + 1) * n_heads, :] = jnp.where(
                    same_head, jnp.broadcast_to(w[r:r + 1, :], (n_heads, width)), 0.0).astype(BF16)
            d = _dot(w_scr[slot, half * rows:(half + 1) * rows, :], v_blk.astype(BF16))
            pv = d if pv is None else pv + d
        return pv, carry_in + tot + tot_other

    @pl.when(s == 0)
    def _():
        z_scr[...] = jnp.zeros_like(z_scr)
        pv, carry = process(0, [(kn_ref[...], vn_ref[...])], (row < n_new) & (lane < row * n_heads),
                            jnp.zeros(carry_scr.shape, F32))
        acc_scr[...] = pv
        carry_scr[...] = carry

    carry = carry_scr[...]
    acc = None
    for p in range(pages_per_step // 2):
        pv, carry = process(p, [(k_refs[2 * p][...], v_refs[2 * p][...]),
                                (k_refs[2 * p + 1][...], v_refs[2 * p + 1][...])], None, carry)
        acc = pv if acc is None else acc + pv
    acc_scr[...] += acc
    carry_scr[...] = carry

    @pl.when(s == pl.num_programs(1) - 1)
    def _():
        o_ref[...] = acc_scr[...]


def sb_attend_sample(q, k_new, v_new, cache_k, cache_v, page_table, bias, layer):
    n_seq, n_new, n_heads, dh = q.shape
    assert 2 * n_new == SUBLANES, "two pages of n_new query rows share one 8-sublane array"
    n_pages = page_table.shape[1]
    n_attn, n_pool, page = cache_k.shape[:3]
    width = page * n_heads
    pps = SB_PAGES_PER_STEP
    rows = n_new * n_heads
    pad = ((0, 0), (0, page - n_new), (0, 0), (0, 0))
    k_pad = jnp.pad(k_new, pad).reshape(n_seq, width, dh)
    v_pad = jnp.pad(v_new, pad).reshape(n_seq, width, dh)
    ck = cache_k.reshape(n_attn, n_pool, width, dh)
    cv = cache_v.reshape(n_attn, n_pool, width, dh)
    bias_lanes = jnp.tile(bias.astype(F32), page).reshape(1, width)

    def page_spec(j):
        return pl.BlockSpec((None, None, width, dh),
                            lambda b, s, pt: (layer, pt[b, n_pages - 1 - (s * pps + j)], 0, 0))

    seq_spec = pl.BlockSpec((None, rows, dh), lambda b, s, pt: (b, 0, 0))
    new_spec = pl.BlockSpec((None, width, dh), lambda b, s, pt: (b, 0, 0))
    kern = functools.partial(_sb_sample_kernel, scale=dh ** -0.5, n_heads=n_heads, n_new=n_new,
                             pages_per_step=pps)
    grid_spec = pltpu.PrefetchScalarGridSpec(
        num_scalar_prefetch=1,
        grid=(n_seq, n_pages // pps),
        in_specs=[seq_spec, pl.BlockSpec((1, width), lambda b, s, pt: (0, 0)), new_spec, new_spec]
                 + [page_spec(j) for j in range(pps)] * 2,
        out_specs=seq_spec,
        scratch_shapes=[pltpu.VMEM((pps // 2, 2 * n_new, width), F32),
                        pltpu.VMEM((pps // 2, 2 * rows, width), BF16),
                        pltpu.VMEM((rows, dh), F32), pltpu.VMEM((2 * n_new, LANES), F32)],
    )
    out = pl.pallas_call(
        kern,
        grid_spec=grid_spec,
        out_shape=jax.ShapeDtypeStruct((n_seq, rows, dh), F32),
        compiler_params=_cparams("parallel", "arbitrary"),
        name="sb_attend_sample",
    )(page_table, q.reshape(n_seq, rows, dh), bias_lanes, k_pad, v_pad, *([ck] * pps), *([cv] * pps))
    return out.reshape(n_seq, n_new, n_heads * dh)


def _ml_gates_kernel(u_ref, wg_ref, b_ref, o_ref, *, n_heads):
    pre = _dot_nt(wg_ref[...].astype(BF16), u_ref[...]) + b_ref[...]
    pre = GATE_SOFTCAP * jnp.tanh(pre / GATE_SOFTCAP)
    o_ref[0:n_heads, :] = pre[0:n_heads]
    o_ref[n_heads:, :] = -_softplus(-pre[n_heads:])


def ml_gates(u, wg_t, b_gate):
    m, d = u.shape
    g = wg_t.shape[0]
    tm = min(m, ROW_TILE)
    return pl.pallas_call(
        functools.partial(_ml_gates_kernel, n_heads=g // 2),
        grid=(m // tm,),
        in_specs=[pl.BlockSpec((tm, d), lambda i: (i, 0)),
                  pl.BlockSpec((g, d), lambda i: (0, 0)),
                  pl.BlockSpec((g, 1), lambda i: (0, 0))],
        out_specs=pl.BlockSpec((g, tm), lambda i: (0, i)),
        out_shape=jax.ShapeDtypeStruct((g, m), F32),
        compiler_params=_cparams("parallel"),
        name="ml_gates",
    )(u, wg_t, b_gate)


def mlstm_gate_layout(g, b, t, n_heads):
    hps = ML_HEADS_PER_STEP
    t_pad = -(-t // CHUNK) * CHUNK
    g = jnp.pad(g.reshape(2, n_heads // hps, hps, b, t), ((0, 0),) * 4 + ((0, t_pad - t),))
    g = g.reshape(2, n_heads // hps, hps, b, t_pad // CHUNK, CHUNK).transpose(3, 4, 1, 2, 0, 5)
    return g.reshape(b, t_pad // CHUNK, n_heads // hps, 2 * hps, CHUNK)


def _lane_cumsum(x):
    lane = lax.broadcasted_iota(jnp.int32, x.shape, 1)
    shift = 1
    while shift < x.shape[1]:
        x = x + jnp.where(lane >= shift, pltpu.roll(x, shift, axis=1), 0.0)
        shift *= 2
    return x


def _mlstm_kernel(g_ref, q_ref, k_ref, v_ref, og_ref, gn_ref, c0_ref, n0_ref, m0_ref,
                  y_ref, c_out, n_out, m_out, c_scr, n_scr, m_scr, *, n_valid, k_scale):
    ci = pl.program_id(2)
    L = g_ref.shape[1]
    t_in = q_ref.shape[0]
    hps = c_scr.shape[0]
    dk = c_scr.shape[1]
    dv = c_scr.shape[2]

    @pl.when(ci == 0)
    def _():
        c_scr[...] = c0_ref[...]
        n_scr[...] = n0_ref[...]
        m_scr[...] = m0_ref[...]

    g = g_ref[...]
    row8 = lax.broadcasted_iota(jnp.int32, (2 * hps, L), 0)
    if n_valid < L:
        valid = lax.broadcasted_iota(jnp.int32, (2 * hps, L), 1) < n_valid
        g = jnp.where(valid, g, jnp.where((row8 & 1) == 0, NEG_BIG, 0.0))
    bc = _lane_cumsum(g)
    rows = jnp.concatenate([jnp.where((row8 & 1) == 0, g, bc), jnp.zeros((L - 2 * hps, L), F32)], axis=0)
    cols = rows.T
    t_idx = lax.broadcasted_iota(jnp.int32, (L, L), 0)
    s_idx = lax.broadcasted_iota(jnp.int32, (L, L), 1)
    causal = s_idx <= t_idx

    for hh in range(hps):
        li_r = g[2 * hh:2 * hh + 1, :]
        bc_r = bc[2 * hh + 1:2 * hh + 2, :]
        li_c = cols[:, 2 * hh:2 * hh + 1]
        bc_c = cols[:, 2 * hh + 1:2 * hh + 2]
        m_prev = m_scr[hh, 0:1, 0:1]
        dmat = jnp.where(causal, bc_c + (li_r - bc_r), NEG_BIG)
        inter = bc_c + m_prev
        m_t = jnp.maximum(inter, jnp.max(dmat, axis=1, keepdims=True))

        qf = _pad_rows(q_ref[:, hh * dk:(hh + 1) * dk], L)
        q = qf.astype(BF16)
        ks_f = _pad_rows(k_ref[:, hh * dk:(hh + 1) * dk], L) * k_scale
        v = _pad_rows(v_ref[:, hh * dv:(hh + 1) * dv], L).astype(BF16)
        a = jnp.exp(dmat - m_t) * _dot_nt(q, ks_f.astype(BF16))
        sc = jnp.exp(inter - m_t)
        c = c_scr[hh]
        n_row = n_scr[hh]
        num = _dot(a.astype(BF16), v) + sc * _dot(q, c.astype(BF16))
        den = jnp.sum(a, axis=1, keepdims=True) + sc * jnp.sum(qf * n_row, axis=1, keepdims=True)
        hval = num / jnp.maximum(jnp.abs(den), jnp.exp(-m_t))
        gate = jax.nn.sigmoid(og_ref[:, hh * dv:(hh + 1) * dv])
        y_ref[:, hh * dv:(hh + 1) * dv] = (gate * _rms(hval[:t_in], gn_ref[...])).astype(y_ref.dtype)

        m_new = m_t[L - 1:L, :]
        bc_last = bc_c[L - 1:L, :]
        wl = jnp.exp(bc_last - bc_c + li_c - m_new)
        sl = jnp.exp(bc_last + m_prev - m_new)
        kw = ks_f * wl
        c_scr[hh] = sl * c + _dot(kw.T.astype(BF16), v)
        n_scr[hh] = sl * n_row + jnp.sum(kw, axis=0, keepdims=True)
        m_scr[hh] = jnp.broadcast_to(m_new, (1, LANES))

    @pl.when(ci == pl.num_programs(2) - 1)
    def _():
        c_out[...] = c_scr[...]
        n_out[...] = n_scr[...]
        m_out[...] = m_scr[...]


def mlstm_scan(proj, gates, out_gain, c0, n0, m0, n_heads, n_valid):
    b, t, _ = proj.shape
    L = min(t, CHUNK)
    dk = c0.shape[2]
    dv = c0.shape[3]
    hps = ML_HEADS_PER_STEP
    n_grp = n_heads // hps
    v_blk = 2 * n_heads * dk // (hps * dv)
    kern = functools.partial(_mlstm_kernel, n_valid=n_valid, k_scale=dk ** -0.5)
    state_specs = [pl.BlockSpec((None, hps, dk, dv), lambda bi, h, ci: (bi, h, 0, 0)),
                   pl.BlockSpec((None, hps, 1, dk), lambda bi, h, ci: (bi, h, 0, 0)),
                   pl.BlockSpec((None, hps, 1, LANES), lambda bi, h, ci: (bi, h, 0, 0))]
    return pl.pallas_call(
        kern,
        grid=(b, n_grp, t // L),
        in_specs=[pl.BlockSpec((None, None, None, 2 * hps, CHUNK), lambda bi, h, ci: (bi, ci, h, 0, 0)),
                  pl.BlockSpec((None, L, hps * dk), lambda bi, h, ci: (bi, ci, h)),
                  pl.BlockSpec((None, L, hps * dk), lambda bi, h, ci: (bi, ci, n_grp + h)),
                  pl.BlockSpec((None, L, hps * dv), lambda bi, h, ci: (bi, ci, v_blk + h)),
                  pl.BlockSpec((None, L, hps * dv), lambda bi, h, ci: (bi, ci, v_blk + n_grp + h)),
                  pl.BlockSpec((1, dv), lambda bi, h, ci: (0, 0))] + state_specs,
        out_specs=[pl.BlockSpec((None, L, hps * dv), lambda bi, h, ci: (bi, ci, h))] + state_specs,
        out_shape=[jax.ShapeDtypeStruct((b, t, n_heads * dv), BF16),
                   jax.ShapeDtypeStruct(c0.shape, F32),
                   jax.ShapeDtypeStruct(n0.shape, F32),
                   jax.ShapeDtypeStruct(m0.shape, F32)],
        scratch_shapes=[pltpu.VMEM((hps, dk, dv), F32), pltpu.VMEM((hps, 1, dk), F32),
                        pltpu.VMEM((hps, 1, LANES), F32)],
        compiler_params=_cparams("parallel", "parallel", "arbitrary"),
        name="mlstm_scan",
    )(gates, proj, proj, proj, proj, out_gain, c0, n0, m0)


def _hgrn_tables(c):
    n_lev = int(np.log2(c))
    lev = np.zeros(((n_lev + 1) * c, c), np.float32)
    lev[:c] = np.tril(np.ones((c, c), np.float32))
    for l in range(1, n_lev + 1):
        m = 1 << (l - 1)
        for t in range(c):
            mid = (t >> l << l) + m
            if t & m:
                lev[l * c + t, mid:t + 1] = 1.0
            else:
                lev[l * c + t, t + 1:mid] = 1.0
    t = np.arange(c)[:, None]
    s = np.arange(c)[None, :]
    x = t ^ s
    pair_level = np.where(s > t, -1, np.where(s == t, 0, np.floor(np.log2(np.maximum(x, 1))).astype(np.int32) + 1))
    return jnp.asarray(np.concatenate([lev, lev], axis=1), BF16), jnp.asarray(pair_level, jnp.int32)


def _hgrn_kernel(lev_ref, lvl_ref, q_ref, f_ref, i_ref, g_ref, lb_ref, gn_ref, s0_ref,
                 y_ref, s_out, st_scr, *, n_valid):
    ci = pl.program_id(2)
    c = lvl_ref.shape[0]
    t_in = q_ref.shape[0]
    n_lev = lev_ref.shape[0] // c - 1
    hps = st_scr.shape[0]
    dh = st_scr.shape[1]

    @pl.when(ci == 0)
    def _():
        for hh in range(hps):
            st_scr[hh] = s0_ref[hh].T

    fp = _pad_rows(f_ref[...], c)
    lb = lb_ref[...]
    log_sig = -_softplus(-fp)
    la = jnp.log(lb)
    lbb = jnp.log1p(-lb) + log_sig
    lf = jnp.maximum(la, lbb) + jnp.log1p(jnp.exp(-jnp.abs(la - lbb)))
    kk = (1.0 - lb) * jax.nn.sigmoid(-fp)
    if n_valid < c:
        valid = lax.broadcasted_iota(jnp.int32, (c, 1), 0) < n_valid
        lf = jnp.where(valid, lf, 0.0)
        kk = jnp.where(valid, kk, 0.0)

    hi, lo = _split_bf16(lf)
    d_all = _dot(lev_ref[...], jnp.concatenate([hi, lo], axis=0))
    bcum = d_all[0:c]
    q = _pad_rows(q_ref[...], c)
    iv_f = _pad_rows(i_ref[...], c)
    lvl = lvl_ref[...]
    heads = [slice(hh * dh, (hh + 1) * dh) for hh in range(hps)]
    qb = q.astype(BF16)
    kb = kk.astype(BF16)
    a = [jnp.where(lvl == 0, _dot_nt(qb[:, sl], kb[:, sl]), 0.0) for sl in heads]
    for l in range(1, n_lev + 1):
        e = jnp.exp(d_all[l * c:(l + 1) * c])
        qe = (q * e).astype(BF16)
        ke = (kk * e).astype(BF16)
        a = [a[hh] + jnp.where(lvl == l, _dot_nt(qe[:, sl], ke[:, sl]), 0.0) for hh, sl in enumerate(heads)]
    blast = bcum[c - 1:c, :]
    q_in = (q * jnp.exp(bcum)).astype(BF16)
    k_out = (kk * jnp.exp(blast - bcum)).astype(BF16)
    decay = jnp.exp(blast)
    gate = jax.nn.sigmoid(g_ref[...])
    for hh, sl in enumerate(heads):
        st = st_scr[hh]
        o = _dot(a[hh].astype(BF16), iv_f[:, sl].astype(BF16)) + _dot_nt(q_in[:, sl], st.astype(BF16))
        y_ref[:, sl] = (gate[:, sl] * _rms(o[:t_in], gn_ref[...])).astype(y_ref.dtype)
        st_scr[hh] = decay[:, sl] * st + _dot(iv_f[:, sl].T.astype(BF16), k_out[:, sl])

    @pl.when(ci == pl.num_programs(2) - 1)
    def _():
        for hh in range(hps):
            s_out[hh] = st_scr[hh].T


def hgrn_scan(proj, lower_bound, out_gain, s0, n_heads, n_valid):
    b, t, _ = proj.shape
    c = min(t, CHUNK)
    dh = s0.shape[2]
    hps = HG_HEADS_PER_STEP
    n_grp = n_heads // hps
    lev, pair_level = _hgrn_tables(CHUNK)
    kern = functools.partial(_hgrn_kernel, n_valid=n_valid)

    def part(p):
        return pl.BlockSpec((None, c, hps * dh), lambda bi, h, ci: (bi, ci, p * n_grp + h))

    state_spec = pl.BlockSpec((None, hps, dh, dh), lambda bi, h, ci: (bi, h, 0, 0))
    return pl.pallas_call(
        kern,
        grid=(b, n_grp, t // c),
        in_specs=[pl.BlockSpec(lev.shape, lambda bi, h, ci: (0, 0)),
                  pl.BlockSpec(pair_level.shape, lambda bi, h, ci: (0, 0)),
                  part(0), part(1), part(2), part(3),
                  pl.BlockSpec((1, hps * dh), lambda bi, h, ci: (0, h)),
                  pl.BlockSpec((1, dh), lambda bi, h, ci: (0, 0)),
                  state_spec],
        out_specs=[pl.BlockSpec((None, c, hps * dh), lambda bi, h, ci: (bi, ci, h)), state_spec],
        out_shape=[jax.ShapeDtypeStruct((b, t, n_heads * dh), BF16),
                   jax.ShapeDtypeStruct(s0.shape, F32)],
        scratch_shapes=[pltpu.VMEM((hps, dh, dh), F32)],
        compiler_params=_cparams("parallel", "parallel", "arbitrary"),
        name="hgrn_scan",
    )(lev, pair_level, proj, proj, proj, proj, lower_bound, out_gain, s0)


def _pad_tokens(x, t_pad):
    return jnp.pad(x, ((0, 0), (0, t_pad - x.shape[1])) + ((0, 0),) * (x.ndim - 2))


def kernel(x_prompt, x_sample, cache_k, cache_v, state_mlstm_c, state_mlstm_n, state_mlstm_m,
           state_hgrn, page_table, p_prompt, p_sample, norm_mix, norm_ffn, norm_ple,
           sb_w_qkv, sb_q_norm, sb_k_norm, sb_logit_bias, sb_w_out, ml_w_in, ml_b_gate,
           ml_out_norm, ml_w_out, hg_w_in, hg_lb_logits, hg_out_norm, hg_w_out,
           ffn_w_up, ffn_w_down, ple_w_proj, ple_w_gate):
    depth, d = norm_mix.shape
    bp, tp, _ = x_prompt.shape
    bs, ts, _ = x_sample.shape
    sb_heads = sb_logit_bias.shape[1]
    ml_heads = state_mlstm_c.shape[2]
    ml_dk = state_mlstm_c.shape[3]
    ml_dv = state_mlstm_c.shape[4]
    hg_heads = state_hgrn.shape[2]
    hg_dh = state_hgrn.shape[3]
    n_attn, n_pool, page = cache_k.shape[:3]
    ml_qk_w = ml_heads * ml_dk

    g_mix = norm_mix.reshape(depth, 1, d)
    g_ffn = norm_ffn.reshape(depth, 1, d)
    g_ple = norm_ple.reshape(depth, 1, d)
    lb_soft = jax.nn.softmax(hg_lb_logits.astype(F32), axis=0)
    lower_bounds = jnp.cumsum(lb_soft, axis=0) - lb_soft

    hp = x_prompt.reshape(bp * tp, d)
    hs = x_sample.reshape(bs * ts, d)
    pp_all = p_prompt.reshape(depth, bp * tp, -1)
    ps_all = p_sample.reshape(depth, bs * ts, -1)

    outs = {k: [] for k in ("ks", "vs", "mcp", "mnp", "mmp", "mcs", "mns", "mms", "hgp", "hgs")}
    kp_all = vp_all = None
    i_attn = i_ml = i_hg = 0
    for layer in range(depth):
        kind = layer % N_MIXERS
        up = rmsnorm_bf16(hp, g_mix, layer)
        us = rmsnorm_bf16(hs, g_mix, layer)
        if kind == 0:
            j = i_attn
            i_attn += 1
            gq = sb_q_norm[j].reshape(1, LANES)
            gk = sb_k_norm[j].reshape(1, LANES)
            q_p, q_s = linear("headnorm", up, sb_w_qkv, j, 0, d, shared=(gq,), side=(us, ()))
            kp_all, k_s = linear("headnorm", up, sb_w_qkv, j, d, d, shared=(gk,), side=(us, ()),
                                 slab=(n_attn, j, kp_all))
            vp_all, v_s = linear("plain", up, sb_w_qkv, j, 2 * d, d, side=(us, ()), slab=(n_attn, j, vp_all))
            q_s, k_s, v_s = (a.reshape(bs, ts, sb_heads, LANES) for a in (q_s, k_s, v_s))
            outs["ks"].append(k_s)
            outs["vs"].append(v_s)
            kv = (kp_all.reshape(n_attn, bp, tp, d), vp_all.reshape(n_attn, bp, tp, d))
            ap = sb_attend_prompt(q_p.reshape(bp, tp, d), kv, j, sb_logit_bias[j]).reshape(bp * tp, d)
            a_s = sb_attend_sample(q_s, k_s, v_s, cache_k, cache_v, page_table,
                                   sb_logit_bias[j], j).reshape(bs * ts, d)
            hp, hs = linear("residual", ap, sb_w_out, j, 0, d, per=(hp,), side=(a_s, (hs,)))
        elif kind == 1:
            j = i_ml
            i_ml += 1
            wg_t = ml_w_in[j][:, 2 * ml_qk_w + 2 * d:].T
            b_gate = ml_b_gate[j].reshape(-1, 1)
            gn = ml_out_norm[j].reshape(1, ml_dv)
            n_main = 2 * ml_qk_w + 2 * d
            proj_p, proj_s = linear("plain", up, ml_w_in, j, 0, n_main, side=(us, ()))
            proj_p = proj_p.reshape(bp, tp, n_main)
            proj_s = proj_s.reshape(bs, ts, n_main)
            gates_p = mlstm_gate_layout(ml_gates(up, wg_t, b_gate), bp, tp, ml_heads)
            gates_s = mlstm_gate_layout(ml_gates(us, wg_t, b_gate), bs, ts, ml_heads)
            zc = jnp.zeros((bp, ml_heads, ml_dk, ml_dv), F32)
            zn = jnp.zeros((bp, ml_heads, 1, ml_dk), F32)
            zm = jnp.zeros((bp, ml_heads, 1, LANES), F32)
            yp, cp, npv, mp = mlstm_scan(proj_p, gates_p, gn, zc, zn, zm, ml_heads, CHUNK)
            m0 = jnp.broadcast_to(state_mlstm_m[j][:, :, None, None], (bs, ml_heads, 1, LANES))
            ys, cs, nsv, ms = mlstm_scan(_pad_tokens(proj_s, SUBLANES), gates_s, gn, state_mlstm_c[j],
                                         state_mlstm_n[j][:, :, None, :], m0, ml_heads, ts)
            outs["mcp"].append(cp)
            outs["mnp"].append(npv[:, :, 0, :])
            outs["mmp"].append(mp[:, :, 0, 0])
            outs["mcs"].append(cs)
            outs["mns"].append(nsv[:, :, 0, :])
            outs["mms"].append(ms[:, :, 0, 0])
            hp, hs = linear("residual", yp.reshape(bp * tp, d), ml_w_out, j, 0, d, per=(hp,),
                            side=(ys[:, :ts].reshape(bs * ts, d), (hs,)))
        else:
            j = i_hg
            i_hg += 1
            lb = lower_bounds[layer].reshape(1, d)
            gn = hg_out_norm[j].reshape(1, hg_dh)
            proj_p, proj_s = linear("plain", up, hg_w_in, j, 0, 4 * d, side=(us, ()))
            proj_p = proj_p.reshape(bp, tp, 4 * d)
            proj_s = proj_s.reshape(bs, ts, 4 * d)
            zs = jnp.zeros((bp, hg_heads, hg_dh, hg_dh), F32)
            yp, sp = hgrn_scan(proj_p, lb, gn, zs, hg_heads, CHUNK)
            ys, ss = hgrn_scan(_pad_tokens(proj_s, SUBLANES), lb, gn, state_hgrn[j], hg_heads, ts)
            outs["hgp"].append(sp)
            outs["hgs"].append(ss)
            hp, hs = linear("residual", yp.reshape(bp * tp, d), hg_w_out, j, 0, d, per=(hp,),
                            side=(ys[:, :ts].reshape(bs * ts, d), (hs,)))
        a_p, a_s = linear("relu2", hp, ffn_w_up, layer, 0, ffn_w_up.shape[2], side=(hs, ()), out_dtype=BF16,
                          norm=(g_ffn, layer))
        hp, hs = linear_residual_acc(a_p, ffn_w_down, layer, hp, a_s, hs)
        hp, hs = linear("ple", hp, ple_w_gate, layer, 0, d, shared=(ple_w_proj,), per=(pp_all, hp),
                        side=(hs, (ps_all, hs)), col_tile=PLE_COL_TILE, norm=(g_ple, layer))

    st = {k: jnp.stack(v) for k, v in outs.items()}
    kv_shape = (n_attn, bp, tp, sb_heads, LANES)
    return (hp.reshape(bp, tp, d), hs.reshape(bs, ts, d), kp_all.reshape(kv_shape), vp_all.reshape(kv_shape),
            st["ks"], st["vs"],
            st["mcp"], st["mnp"], st["mmp"], st["mcs"], st["mns"], st["mms"], st["hgp"], st["hgs"])
```

```python
import functools

import numpy as np
import jax
import jax.numpy as jnp
from jax import lax
from jax.experimental import pallas as pl
from jax.experimental.pallas import tpu as pltpu

F32 = jnp.float32
BF16 = jnp.bfloat16

NORM_EPS = 1e-6
N_MIXERS = 3
GATE_SOFTCAP = 15.0
LANES = 128
SUBLANES = 8
VMEM_LIMIT_BYTES = 56 * 1024 * 1024
ROW_TILE = 2048
COL_TILE = 512
PLE_COL_TILE = 256
FFN_COL_TILE = 1024
ACC_ROW_TILE = 1024
ACC_COL_TILE = 1024
ACC_K_TILE = 2048
SB_Q_TILE = 512
SB_K_TILE = 512
SB_HEADS_PER_STEP = 4
SB_PAGES_PER_STEP = 8
CHUNK = 128
ML_HEADS_PER_STEP = 8
HG_HEADS_PER_STEP = 8
NEG_BIG = -1e30


def _cparams(*sem):
    return pltpu.CompilerParams(dimension_semantics=sem, vmem_limit_bytes=VMEM_LIMIT_BYTES)


def _dot(a, b):
    return jnp.dot(a, b, preferred_element_type=F32)


def _dot_nt(a, b):
    return lax.dot_general(a, b, (((1,), (1,)), ((), ())), preferred_element_type=F32)


def _split_bf16(x):
    hi = x.astype(BF16)
    lo = (x - hi.astype(F32)).astype(BF16)
    return hi, lo


def _softplus(z):
    return jnp.maximum(z, 0.0) + jnp.log(1.0 + jnp.exp(-jnp.abs(z)))


def _pad_rows(x, rows):
    if x.shape[0] == rows:
        return x
    return jnp.concatenate([x, jnp.zeros((rows - x.shape[0], x.shape[1]), x.dtype)], axis=0)


def _rms(x, gain):
    ms = jnp.mean(x * x, axis=-1, keepdims=True)
    return x * lax.rsqrt(ms + NORM_EPS) * gain


def _rmsnorm_kernel(x_ref, g_ref, o_ref):
    o_ref[...] = _rms(x_ref[...], g_ref[...]).astype(o_ref.dtype)


def rmsnorm_bf16(x, gains, layer):
    m, d = x.shape
    tm = min(m, 512)
    return pl.pallas_call(
        _rmsnorm_kernel,
        grid=(m // tm,),
        in_specs=[pl.BlockSpec((tm, d), lambda i: (i, 0)),
                  pl.BlockSpec((None, 1, d), lambda i: (layer, 0, 0))],
        out_specs=pl.BlockSpec((tm, d), lambda i: (i, 0)),
        out_shape=jax.ShapeDtypeStruct((m, d), BF16),
        compiler_params=_cparams("parallel"),
        name="rmsnorm",
    )(x, gains)


_LINEAR_INPUTS = {"plain": (0, 0), "headnorm": (1, 0), "residual": (0, 1), "relu2": (0, 0), "ple": (1, 2)}


def _linear_epilogue(mode, acc, shared, wp, per, cols, dtype):
    if mode == "plain":
        return acc
    if mode == "headnorm":
        return jnp.concatenate([_rms(acc[:, g * LANES:(g + 1) * LANES], shared[0][...])
                                for g in range(acc.shape[1] // LANES)], axis=1)
    if mode == "residual":
        return per[0][:, cols] + acc
    if mode == "relu2":
        a = jnp.maximum(acc, 0.0)
        return (a * a).astype(dtype)
    return per[1][:, cols] + jax.nn.sigmoid(acc) * _dot(per[0][...].astype(BF16), wp)


def _linear_kernel(*refs, mode, has_side, has_norm, first_slab=None):
    n_shared, n_per = _LINEAR_INPUTS[mode]
    n_groups = 2 if has_side else 1
    n_scratch = 1 if has_norm else 0
    w = refs[0][...].astype(BF16)
    shared = refs[1:1 + n_shared]
    pos = 1 + n_shared
    if has_norm:
        gain_ref = refs[pos]
        pos += 1
    groups = [refs[pos + g * (1 + n_per):pos + (g + 1) * (1 + n_per)] for g in range(n_groups)]
    outs = refs[len(refs) - n_groups - n_scratch:len(refs) - n_scratch]
    wp = shared[0][...].astype(BF16) if mode == "ple" else None
    tn = w.shape[1]

    x_ref, *per = groups[0]
    if has_norm:
        u_scr = refs[-1]

        @pl.when(pl.program_id(1) == 0)
        def _():
            u_scr[...] = _rms(x_ref[...], gain_ref[...]).astype(BF16)

        xv = u_scr[...]
    else:
        xv = x_ref[...].astype(BF16)
    acc = _dot(xv, w)
    res = _linear_epilogue(mode, acc, shared, wp, per, slice(None), outs[0].dtype)
    if first_slab is None:
        outs[0][...] = res
    else:
        for s in range(outs[0].shape[0]):
            outs[0][s] = res if s == first_slab else jnp.zeros_like(res)

    if has_side:
        @pl.when(pl.program_id(0) == 0)
        def _():
            xs_ref, *per_s = groups[1]
            cols = pl.ds(pl.multiple_of(pl.program_id(1) * tn, tn), tn)
            xs = _rms(xs_ref[...], gain_ref[...]) if has_norm else xs_ref[...]
            acc_s = _dot(xs.astype(BF16), w)
            outs[1][:, cols] = _linear_epilogue(mode, acc_s, shared, wp, per_s, cols, outs[1].dtype)


def linear(mode, x, w_all, layer, col_start, n_cols, shared=(), per=(), side=None, out_dtype=F32,
           col_tile=COL_TILE, slab=None, norm=None):
    m, k = x.shape
    tn = col_tile
    off = col_start // tn
    tm = min(m, ROW_TILE)
    in_specs = [pl.BlockSpec((None, k, tn), lambda i, j: (layer, 0, off + j))]
    if mode == "headnorm":
        in_specs.append(pl.BlockSpec((1, LANES), lambda i, j: (0, 0)))
    elif mode == "ple":
        in_specs.append(pl.BlockSpec((None, shared[0].shape[1], tn), lambda i, j: (layer, 0, j)))
    args = [w_all, *shared]
    scratch = []
    if norm is None:
        in_specs.append(pl.BlockSpec((tm, k), lambda i, j: (i, 0)))
    else:
        gains, g_idx = norm
        args.append(gains)
        in_specs.append(pl.BlockSpec((None, 1, k), lambda i, j: (g_idx, 0, 0)))
        in_specs.append(pl.BlockSpec((tm, k), lambda i, j: (i, 0), pipeline_mode=pl.Buffered(1)))
        scratch.append(pltpu.VMEM((tm, k), BF16))
    args += [x, *per]
    if mode == "residual":
        in_specs.append(pl.BlockSpec((tm, tn), lambda i, j: (i, j)))
    elif mode == "ple":
        in_specs.append(pl.BlockSpec((None, tm, per[0].shape[2]), lambda i, j: (layer, i, 0)))
        in_specs.append(pl.BlockSpec((tm, tn), lambda i, j: (i, j)))
    out_specs = [pl.BlockSpec((tm, tn), lambda i, j: (i, j))]
    out_shape = [jax.ShapeDtypeStruct((m, n_cols), out_dtype)]
    if side is not None:
        x_s, per_s = side
        ms = x_s.shape[0]
        args += [x_s, *per_s]
        in_specs.append(pl.BlockSpec((ms, k), lambda i, j: (0, 0)))
        if mode == "residual":
            in_specs.append(pl.BlockSpec((ms, n_cols), lambda i, j: (0, 0)))
        elif mode == "ple":
            in_specs.append(pl.BlockSpec((None, ms, per_s[0].shape[2]), lambda i, j: (layer, 0, 0)))
            in_specs.append(pl.BlockSpec((ms, n_cols), lambda i, j: (0, 0)))
        out_specs.append(pl.BlockSpec((ms, n_cols), lambda i, j: (0, 0)))
        out_shape.append(jax.ShapeDtypeStruct((ms, n_cols), out_dtype))
    aliases = {}
    first_slab = None
    if slab is not None:
        n_slabs, index, prev = slab
        out_shape[0] = jax.ShapeDtypeStruct((n_slabs, m, n_cols), out_dtype)
        if prev is None:
            first_slab = index
            out_specs[0] = pl.BlockSpec((n_slabs, tm, tn), lambda i, j: (0, i, j))
        else:
            out_specs[0] = pl.BlockSpec((None, tm, tn), lambda i, j: (index, i, j))
            aliases = {len(args): 0}
            args.append(prev)
            in_specs.append(pl.BlockSpec(memory_space=pl.ANY))
    outs = pl.pallas_call(
        functools.partial(_linear_kernel, mode=mode, has_side=side is not None, has_norm=norm is not None,
                          first_slab=first_slab),
        grid=(m // tm, n_cols // tn),
        in_specs=in_specs,
        out_specs=out_specs,
        out_shape=out_shape,
        scratch_shapes=scratch,
        input_output_aliases=aliases,
        compiler_params=_cparams("arbitrary", "arbitrary"),
        name="linear_" + mode,
    )(*args)
    return outs[0] if side is None else outs


def _linear_acc_kernel(w_ref, x_ref, r_ref, xs_ref, rs_ref, o_ref, os_ref):
    kk = pl.program_id(2)
    w = w_ref[...].astype(BF16)
    tk, tn = w.shape

    @pl.when(kk == 0)
    def _():
        o_ref[...] = r_ref[...]

    o_ref[...] += _dot(x_ref[...], w)

    @pl.when(pl.program_id(0) == 0)
    def _():
        cols = pl.ds(pl.multiple_of(pl.program_id(1) * tn, tn), tn)

        @pl.when(kk == 0)
        def _():
            os_ref[:, cols] = rs_ref[:, cols]

        os_ref[:, cols] += _dot(xs_ref[:, pl.ds(pl.multiple_of(kk * tk, tk), tk)], w)


def linear_residual_acc(x, w_all, layer, res, x_side, res_side):
    m, k = x.shape
    n = res.shape[1]
    ms = x_side.shape[0]
    tm = min(m, ACC_ROW_TILE)
    tn = ACC_COL_TILE
    tk = ACC_K_TILE
    return pl.pallas_call(
        _linear_acc_kernel,
        grid=(m // tm, n // tn, k // tk),
        in_specs=[pl.BlockSpec((None, tk, tn), lambda i, j, kk: (layer, kk, j)),
                  pl.BlockSpec((tm, tk), lambda i, j, kk: (i, kk)),
                  pl.BlockSpec((tm, tn), lambda i, j, kk: (i, j)),
                  pl.BlockSpec((ms, k), lambda i, j, kk: (0, 0)),
                  pl.BlockSpec((ms, n), lambda i, j, kk: (0, 0))],
        out_specs=[pl.BlockSpec((tm, tn), lambda i, j, kk: (i, j)),
                   pl.BlockSpec((ms, n), lambda i, j, kk: (0, 0))],
        out_shape=[jax.ShapeDtypeStruct((m, n), F32), jax.ShapeDtypeStruct((ms, n), F32)],
        compiler_params=_cparams("arbitrary", "arbitrary", "arbitrary"),
        name="linear_acc",
    )(w_all, x, res, x_side, res_side)


def _cumsum_weights():
    row = lax.broadcasted_iota(jnp.int32, (LANES, 2 * LANES), 0)
    col = lax.broadcasted_iota(jnp.int32, (LANES, 2 * LANES), 1)
    return jnp.where((col >= LANES) | (row > col), 1.0, 0.0).astype(BF16)


def _log_sigmoid_pair(z):
    neg_abs = lax.bitcast_convert_type(lax.bitcast_convert_type(z, jnp.uint32) | jnp.uint32(0x80000000), F32)
    log_sig = jnp.minimum(z, 0.0) - jnp.log(1.0 + jnp.exp(neg_abs))
    return log_sig, log_sig - z


def _sb_prompt_kernel(bias_ref, q_ref, k_ref, v_ref, o_ref, acc_scr, carry_scr, *, scale, tq, tk):
    qi = pl.program_id(2)
    hps = acc_scr.shape[0]
    dh = acc_scr.shape[2]
    uo = _cumsum_weights()
    acc_scr[...] = jnp.zeros_like(acc_scr)
    carry_scr[...] = jnp.zeros_like(carry_scr)
    n_diag = tq // tk
    n_blk = tk // LANES

    def tile(ks, masked):
        for hh in range(hps):
            head_tile(hh, ks, masked)

    def head_tile(hh, ks, masked):
        cols = slice(hh * dh, (hh + 1) * dh)
        q = q_ref[:, cols].astype(BF16)
        bias = bias_ref[pl.program_id(1) * hps + hh]
        kb = k_ref[pl.ds(ks, tk), cols].astype(BF16)
        vb = v_ref[pl.ds(ks, tk), cols].astype(BF16)
        z = _dot_nt(q, kb) * scale + bias
        log_sig, log_keep = _log_sigmoid_pair(z)
        if masked:
            q_pos = qi * tq + lax.broadcasted_iota(jnp.int32, (tq, tk), 0)
            k_pos = ks + lax.broadcasted_iota(jnp.int32, (tq, tk), 1)
            mask = k_pos < q_pos
            log_keep = jnp.where(mask, log_keep, 0.0)
        log_keep = log_keep.astype(BF16)
        carry = carry_scr[hh]
        tails = [None] * n_blk
        for c in reversed(range(n_blk)):
            cu = _dot(log_keep[:, c * LANES:(c + 1) * LANES], uo)
            tails[c] = cu[:, :LANES] + carry
            carry = carry + cu[:, LANES:]
        w = jnp.exp(log_sig + jnp.concatenate(tails, axis=1))
        if masked:
            w = jnp.where(mask, w, 0.0)
        acc_scr[hh] += _dot(w.astype(BF16), vb)
        carry_scr[hh] = carry

    for j in range(n_diag):
        tile(pl.multiple_of((qi * n_diag + n_diag - 1 - j) * tk, tk), True)

    def body(it, _):
        tile(pl.multiple_of((qi * n_diag - 1 - it) * tk, tk), False)
        return 0

    lax.fori_loop(0, qi * n_diag, body, 0)
    for hh in range(hps):
        o_ref[:, hh * dh:(hh + 1) * dh] = acc_scr[hh].astype(o_ref.dtype)


def sb_attend_prompt(q, kv, layer, bias):
    b, t, _ = q.shape
    dh = LANES
    n_heads = bias.shape[0]
    hps = SB_HEADS_PER_STEP
    tq = min(t, SB_Q_TILE)
    tk = min(tq, SB_K_TILE)
    kern = functools.partial(_sb_prompt_kernel, scale=dh ** -0.5, tq=tq, tk=tk)
    kv_spec = pl.BlockSpec((None, None, t, hps * dh), lambda bi, h, qi: (layer, bi, 0, h))
    return pl.pallas_call(
        kern,
        grid=(b, n_heads // hps, t // tq),
        in_specs=[pl.BlockSpec(memory_space=pltpu.SMEM),
                  pl.BlockSpec((None, tq, hps * dh), lambda bi, h, qi: (bi, qi, h)),
                  kv_spec, kv_spec],
        out_specs=pl.BlockSpec((None, tq, hps * dh), lambda bi, h, qi: (bi, qi, h)),
        out_shape=jax.ShapeDtypeStruct((b, t, n_heads * dh), BF16),
        scratch_shapes=[pltpu.VMEM((hps, tq, dh), F32), pltpu.VMEM((hps, tq, LANES), F32)],
        compiler_params=_cparams("parallel", "parallel", "arbitrary"),
        name="sb_attend_prompt",
    )(bias, q, kv[0], kv[1])


def _suffix_sum_strided(x, stride):
    w = x.shape[1]
    lane = lax.broadcasted_iota(jnp.int32, x.shape, 1)
    s = stride
    while s < w:
        if s % LANES == 0:
            shifted = jnp.concatenate([x[:, s:], jnp.zeros((x.shape[0], s), F32)], axis=1)
        else:
            shifted = jnp.where(lane < w - s, pltpu.roll(x, w - s, axis=1), 0.0)
        x = x + shifted
        s *= 2
    return x


def _sb_sample_kernel(pt_ref, q_ref, bias_ref, kn_ref, vn_ref, *rest, scale, n_heads, n_new, pages_per_step):
    k_refs = rest[:pages_per_step]
    v_refs = rest[pages_per_step:2 * pages_per_step]
    o_ref, z_scr, w_scr, acc_scr, carry_scr = rest[2 * pages_per_step:]
    s = pl.program_id(1)
    width = kn_ref.shape[0]
    rows = n_new * n_heads
    lane_h = lax.broadcasted_iota(jnp.int32, (n_heads, width), 1) & (n_heads - 1)
    same_head = lane_h == lax.broadcasted_iota(jnp.int32, (n_heads, width), 0)
    row = lax.broadcasted_iota(jnp.int32, (2 * n_new, width), 0)
    lane = lax.broadcasted_iota(jnp.int32, (2 * n_new, width), 1)
    row128 = lax.broadcasted_iota(jnp.int32, (2 * n_new, LANES), 0)
    lane128 = lax.broadcasted_iota(jnp.int32, (2 * n_new, LANES), 1)
    q = q_ref[...].astype(BF16)

    def process(slot, blocks, mask, carry_in):
        for half, (k_blk, _) in enumerate(blocks):
            zfull = _dot_nt(q, k_blk.astype(BF16))
            for i in range(n_new):
                part = zfull[i * n_heads:(i + 1) * n_heads, :]
                r = half * n_new + i
                z_scr[slot, r:r + 1, :] = jnp.sum(jnp.where(same_head, part, 0.0), axis=0, keepdims=True)
        z = z_scr[slot] * scale + bias_ref[...]
        log_sig, log_keep = _log_sigmoid_pair(z)
        if mask is not None:
            log_keep = jnp.where(mask, log_keep, 0.0)
        incl = _suffix_sum_strided(log_keep, n_heads)
        tot = jnp.where(lane128 < n_heads, incl[:, :LANES], 0.0)
        shift = n_heads
        while shift < LANES:
            tot = tot + pltpu.roll(tot, shift, axis=1)
            shift *= 2
        tot_other = pltpu.roll(tot, n_new, axis=0)
        carry = carry_in + jnp.where(row128 >= n_new, tot_other, 0.0)
        w = jnp.exp(log_sig + (incl - log_keep) + jnp.concatenate([carry] * (width // LANES), axis=1))
        if mask is not None:
            w = jnp.where(mask, w, 0.0)
        pv = None
        for half, (_, v_blk) in enumerate(blocks):
            for i in range(n_new):
                r = half * n_new + i
                w_scr[slot, half * rows + i * n_heads:half * rows + (i + 1) * n_heads, :] = jnp.where(
                    same_head, jnp.broadcast_to(w[r:r + 1, :], (n_heads, width)), 0.0).astype(BF16)
            d = _dot(w_scr[slot, half * rows:(half + 1) * rows, :], v_blk.astype(BF16))
            pv = d if pv is None else pv + d
        return pv, carry_in + tot + tot_other

    @pl.when(s == 0)
    def _():
        z_scr[...] = jnp.zeros_like(z_scr)
        pv, carry = process(0, [(kn_ref[...], vn_ref[...])], (row < n_new) & (lane < row * n_heads),
                            jnp.zeros(carry_scr.shape, F32))
        acc_scr[...] = pv
        carry_scr[...] = carry

    carry = carry_scr[...]
    acc = None
    for p in range(pages_per_step // 2):
        pv, carry = process(p, [(k_refs[2 * p][...], v_refs[2 * p][...]),
                                (k_refs[2 * p + 1][...], v_refs[2 * p + 1][...])], None, carry)
        acc = pv if acc is None else acc + pv
    acc_scr[...] += acc
    carry_scr[...] = carry

    @pl.when(s == pl.num_programs(1) - 1)
    def _():
        o_ref[...] = acc_scr[...]


def sb_attend_sample(q, k_new, v_new, cache_k, cache_v, page_table, bias, layer):
    n_seq, n_new, n_heads, dh = q.shape
    assert 2 * n_new == SUBLANES, "two pages of n_new query rows share one 8-sublane array"
    n_pages = page_table.shape[1]
    n_attn, n_pool, page = cache_k.shape[:3]
    width = page * n_heads
    pps = SB_PAGES_PER_STEP
    rows = n_new * n_heads
    pad = ((0, 0), (0, page - n_new), (0, 0), (0, 0))
    k_pad = jnp.pad(k_new, pad).reshape(n_seq, width, dh)
    v_pad = jnp.pad(v_new, pad).reshape(n_seq, width, dh)
    ck = cache_k.reshape(n_attn, n_pool, width, dh)
    cv = cache_v.reshape(n_attn, n_pool, width, dh)
    bias_lanes = jnp.tile(bias.astype(F32), page).reshape(1, width)

    def page_spec(j):
        return pl.BlockSpec((None, None, width, dh),
                            lambda b, s, pt: (layer, pt[b, n_pages - 1 - (s * pps + j)], 0, 0))

    seq_spec = pl.BlockSpec((None, rows, dh), lambda b, s, pt: (b, 0, 0))
    new_spec = pl.BlockSpec((None, width, dh), lambda b, s, pt: (b, 0, 0))
    kern = functools.partial(_sb_sample_kernel, scale=dh ** -0.5, n_heads=n_heads, n_new=n_new,
                             pages_per_step=pps)
    grid_spec = pltpu.PrefetchScalarGridSpec(
        num_scalar_prefetch=1,
        grid=(n_seq, n_pages // pps),
        in_specs=[seq_spec, pl.BlockSpec((1, width), lambda b, s, pt: (0, 0)), new_spec, new_spec]
                 + [page_spec(j) for j in range(pps)] * 2,
        out_specs=seq_spec,
        scratch_shapes=[pltpu.VMEM((pps // 2, 2 * n_new, width), F32),
                        pltpu.VMEM((pps // 2, 2 * rows, width), BF16),
                        pltpu.VMEM((rows, dh), F32), pltpu.VMEM((2 * n_new, LANES), F32)],
    )
    out = pl.pallas_call(
        kern,
        grid_spec=grid_spec,
        out_shape=jax.ShapeDtypeStruct((n_seq, rows, dh), F32),
        compiler_params=_cparams("parallel", "arbitrary"),
        name="sb_attend_sample",
    )(page_table, q.reshape(n_seq, rows, dh), bias_lanes, k_pad, v_pad, *([ck] * pps), *([cv] * pps))
    return out.reshape(n_seq, n_new, n_heads * dh)


def _ml_gates_kernel(u_ref, wg_ref, b_ref, o_ref, *, n_heads):
    pre = _dot_nt(wg_ref[...].astype(BF16), u_ref[...]) + b_ref[...]
    pre = GATE_SOFTCAP * jnp.tanh(pre / GATE_SOFTCAP)
    o_ref[0:n_heads, :] = pre[0:n_heads]
    o_ref[n_heads:, :] = -_softplus(-pre[n_heads:])


def ml_gates(u, wg_t, b_gate):
    m, d = u.shape
    g = wg_t.shape[0]
    tm = min(m, ROW_TILE)
    return pl.pallas_call(
        functools.partial(_ml_gates_kernel, n_heads=g // 2),
        grid=(m // tm,),
        in_specs=[pl.BlockSpec((tm, d), lambda i: (i, 0)),
                  pl.BlockSpec((g, d), lambda i: (0, 0)),
                  pl.BlockSpec((g, 1), lambda i: (0, 0))],
        out_specs=pl.BlockSpec((g, tm), lambda i: (0, i)),
        out_shape=jax.ShapeDtypeStruct((g, m), F32),
        compiler_params=_cparams("parallel"),
        name="ml_gates",
    )(u, wg_t, b_gate)


def mlstm_gate_layout(g, b, t, n_heads):
    hps = ML_HEADS_PER_STEP
    t_pad = -(-t // CHUNK) * CHUNK
    g = jnp.pad(g.reshape(2, n_heads // hps, hps, b, t), ((0, 0),) * 4 + ((0, t_pad - t),))
    g = g.reshape(2, n_heads // hps, hps, b, t_pad // CHUNK, CHUNK).transpose(3, 4, 1, 2, 0, 5)
    return g.reshape(b, t_pad // CHUNK, n_heads // hps, 2 * hps, CHUNK)


def _lane_cumsum(x):
    lane = lax.broadcasted_iota(jnp.int32, x.shape, 1)
    shift = 1
    while shift < x.shape[1]:
        x = x + jnp.where(lane >= shift, pltpu.roll(x, shift, axis=1), 0.0)
        shift *= 2
    return x


def _mlstm_kernel(g_ref, q_ref, k_ref, v_ref, og_ref, gn_ref, c0_ref, n0_ref, m0_ref,
                  y_ref, c_out, n_out, m_out, c_scr, n_scr, m_scr, *, n_valid, k_scale):
    ci = pl.program_id(2)
    L = g_ref.shape[1]
    t_in = q_ref.shape[0]
    hps = c_scr.shape[0]
    dk = c_scr.shape[1]
    dv = c_scr.shape[2]

    @pl.when(ci == 0)
    def _():
        c_scr[...] = c0_ref[...]
        n_scr[...] = n0_ref[...]
        m_scr[...] = m0_ref[...]

    g = g_ref[...]
    row8 = lax.broadcasted_iota(jnp.int32, (2 * hps, L), 0)
    if n_valid < L:
        valid = lax.broadcasted_iota(jnp.int32, (2 * hps, L), 1) < n_valid
        g = jnp.where(valid, g, jnp.where((row8 & 1) == 0, NEG_BIG, 0.0))
    bc = _lane_cumsum(g)
    rows = jnp.concatenate([jnp.where((row8 & 1) == 0, g, bc), jnp.zeros((L - 2 * hps, L), F32)], axis=0)
    cols = rows.T
    t_idx = lax.broadcasted_iota(jnp.int32, (L, L), 0)
    s_idx = lax.broadcasted_iota(jnp.int32, (L, L), 1)
    causal = s_idx <= t_idx

    for hh in range(hps):
        li_r = g[2 * hh:2 * hh + 1, :]
        bc_r = bc[2 * hh + 1:2 * hh + 2, :]
        li_c = cols[:, 2 * hh:2 * hh + 1]
        bc_c = cols[:, 2 * hh + 1:2 * hh + 2]
        m_prev = m_scr[hh, 0:1, 0:1]
        dmat = jnp.where(causal, bc_c + (li_r - bc_r), NEG_BIG)
        inter = bc_c + m_prev
        m_t = jnp.maximum(inter, jnp.max(dmat, axis=1, keepdims=True))

        qf = _pad_rows(q_ref[:, hh * dk:(hh + 1) * dk], L)
        q = qf.astype(BF16)
        ks_f = _pad_rows(k_ref[:, hh * dk:(hh + 1) * dk], L) * k_scale
        v = _pad_rows(v_ref[:, hh * dv:(hh + 1) * dv], L).astype(BF16)
        a = jnp.exp(dmat - m_t) * _dot_nt(q, ks_f.astype(BF16))
        sc = jnp.exp(inter - m_t)
        c = c_scr[hh]
        n_row = n_scr[hh]
        num = _dot(a.astype(BF16), v) + sc * _dot(q, c.astype(BF16))
        den = jnp.sum(a, axis=1, keepdims=True) + sc * jnp.sum(qf * n_row, axis=1, keepdims=True)
        hval = num / jnp.maximum(jnp.abs(den), jnp.exp(-m_t))
        gate = jax.nn.sigmoid(og_ref[:, hh * dv:(hh + 1) * dv])
        y_ref[:, hh * dv:(hh + 1) * dv] = (gate * _rms(hval[:t_in], gn_ref[...])).astype(y_ref.dtype)

        m_new = m_t[L - 1:L, :]
        bc_last = bc_c[L - 1:L, :]
        wl = jnp.exp(bc_last - bc_c + li_c - m_new)
        sl = jnp.exp(bc_last + m_prev - m_new)
        kw = ks_f * wl
        c_scr[hh] = sl * c + _dot(kw.T.astype(BF16), v)
        n_scr[hh] = sl * n_row + jnp.sum(kw, axis=0, keepdims=True)
        m_scr[hh] = jnp.broadcast_to(m_new, (1, LANES))

    @pl.when(ci == pl.num_programs(2) - 1)
    def _():
        c_out[...] = c_scr[...]
        n_out[...] = n_scr[...]
        m_out[...] = m_scr[...]


def mlstm_scan(proj, gates, out_gain, c0, n0, m0, n_heads, n_valid):
    b, t, _ = proj.shape
    L = min(t, CHUNK)
    dk = c0.shape[2]
    dv = c0.shape[3]
    hps = ML_HEADS_PER_STEP
    n_grp = n_heads // hps
    v_blk = 2 * n_heads * dk // (hps * dv)
    kern = functools.partial(_mlstm_kernel, n_valid=n_valid, k_scale=dk ** -0.5)
    state_specs = [pl.BlockSpec((None, hps, dk, dv), lambda bi, h, ci: (bi, h, 0, 0)),
                   pl.BlockSpec((None, hps, 1, dk), lambda bi, h, ci: (bi, h, 0, 0)),
                   pl.BlockSpec((None, hps, 1, LANES), lambda bi, h, ci: (bi, h, 0, 0))]
    return pl.pallas_call(
        kern,
        grid=(b, n_grp, t // L),
        in_specs=[pl.BlockSpec((None, None, None, 2 * hps, CHUNK), lambda bi, h, ci: (bi, ci, h, 0, 0)),
                  pl.BlockSpec((None, L, hps * dk), lambda bi, h, ci: (bi, ci, h)),
                  pl.BlockSpec((None, L, hps * dk), lambda bi, h, ci: (bi, ci, n_grp + h)),
                  pl.BlockSpec((None, L, hps * dv), lambda bi, h, ci: (bi, ci, v_blk + h)),
                  pl.BlockSpec((None, L, hps * dv), lambda bi, h, ci: (bi, ci, v_blk + n_grp + h)),
                  pl.BlockSpec((1, dv), lambda bi, h, ci: (0, 0))] + state_specs,
        out_specs=[pl.BlockSpec((None, L, hps * dv), lambda bi, h, ci: (bi, ci, h))] + state_specs,
        out_shape=[jax.ShapeDtypeStruct((b, t, n_heads * dv), BF16),
                   jax.ShapeDtypeStruct(c0.shape, F32),
                   jax.ShapeDtypeStruct(n0.shape, F32),
                   jax.ShapeDtypeStruct(m0.shape, F32)],
        scratch_shapes=[pltpu.VMEM((hps, dk, dv), F32), pltpu.VMEM((hps, 1, dk), F32),
                        pltpu.VMEM((hps, 1, LANES), F32)],
        compiler_params=_cparams("parallel", "parallel", "arbitrary"),
        name="mlstm_scan",
    )(gates, proj, proj, proj, proj, out_gain, c0, n0, m0)


def _hgrn_tables(c):
    n_lev = int(np.log2(c))
    lev = np.zeros(((n_lev + 1) * c, c), np.float32)
    lev[:c] = np.tril(np.ones((c, c), np.float32))
    for l in range(1, n_lev + 1):
        m = 1 << (l - 1)
        for t in range(c):
            mid = (t >> l << l) + m
            if t & m:
                lev[l * c + t, mid:t + 1] = 1.0
            else:
                lev[l * c + t, t + 1:mid] = 1.0
    t = np.arange(c)[:, None]
    s = np.arange(c)[None, :]
    x = t ^ s
    pair_level = np.where(s > t, -1, np.where(s == t, 0, np.floor(np.log2(np.maximum(x, 1))).astype(np.int32) + 1))
    return jnp.asarray(np.concatenate([lev, lev], axis=1), BF16), jnp.asarray(pair_level, jnp.int32)


def _hgrn_kernel(lev_ref, lvl_ref, q_ref, f_ref, i_ref, g_ref, lb_ref, gn_ref, s0_ref,
                 y_ref, s_out, st_scr, *, n_valid):
    ci = pl.program_id(2)
    c = lvl_ref.shape[0]
    t_in = q_ref.shape[0]
    n_lev = lev_ref.shape[0] // c - 1
    hps = st_scr.shape[0]
    dh = st_scr.shape[1]

    @pl.when(ci == 0)
    def _():
        for hh in range(hps):
            st_scr[hh] = s0_ref[hh].T

    fp = _pad_rows(f_ref[...], c)
    lb = lb_ref[...]
    log_sig = -_softplus(-fp)
    la = jnp.log(lb)
    lbb = jnp.log1p(-lb) + log_sig
    lf = jnp.maximum(la, lbb) + jnp.log(1.0 + jnp.exp(-jnp.abs(la - lbb)))
    kk = (1.0 - lb) * jax.nn.sigmoid(-fp)
    if n_valid < c:
        valid = lax.broadcasted_iota(jnp.int32, (c, 1), 0) < n_valid
        lf = jnp.where(valid, lf, 0.0)
        kk = jnp.where(valid, kk, 0.0)

    hi, lo = _split_bf16(lf)
    d_all = _dot(lev_ref[...], jnp.concatenate([hi, lo], axis=0))
    bcum = d_all[0:c]
    q = _pad_rows(q_ref[...], c)
    iv_f = _pad_rows(i_ref[...], c)
    lvl = lvl_ref[...]
    heads = [slice(hh * dh, (hh + 1) * dh) for hh in range(hps)]
    qb = q.astype(BF16)
    kb = kk.astype(BF16)
    a = [jnp.where(lvl == 0, _dot_nt(qb[:, sl], kb[:, sl]), 0.0) for sl in heads]
    for l in range(1, n_lev + 1):
        e = jnp.exp(d_all[l * c:(l + 1) * c])
        qe = (q * e).astype(BF16)
        ke = (kk * e).astype(BF16)
        a = [a[hh] + jnp.where(lvl == l, _dot_nt(qe[:, sl], ke[:, sl]), 0.0) for hh, sl in enumerate(heads)]
    blast = bcum[c - 1:c, :]
    q_in = (q * jnp.exp(bcum)).astype(BF16)
    k_out = (kk * jnp.exp(blast - bcum)).astype(BF16)
    decay = jnp.exp(blast)
    gate = jax.nn.sigmoid(g_ref[...])
    for hh, sl in enumerate(heads):
        st = st_scr[hh]
        o = _dot(a[hh].astype(BF16), iv_f[:, sl].astype(BF16)) + _dot_nt(q_in[:, sl], st.astype(BF16))
        y_ref[:, sl] = (gate[:, sl] * _rms(o[:t_in], gn_ref[...])).astype(y_ref.dtype)
        st_scr[hh] = decay[:, sl] * st + _dot(iv_f[:, sl].T.astype(BF16), k_out[:, sl])

    @pl.when(ci == pl.num_programs(2) - 1)
    def _():
        for hh in range(hps):
            s_out[hh] = st_scr[hh].T


def hgrn_scan(proj, lower_bound, out_gain, s0, n_heads, n_valid):
    b, t, _ = proj.shape
    c = min(t, CHUNK)
    dh = s0.shape[2]
    hps = HG_HEADS_PER_STEP
    n_grp = n_heads // hps
    lev, pair_level = _hgrn_tables(CHUNK)
    kern = functools.partial(_hgrn_kernel, n_valid=n_valid)

    def part(p):
        return pl.BlockSpec((None, c, hps * dh), lambda bi, h, ci: (bi, ci, p * n_grp + h))

    state_spec = pl.BlockSpec((None, hps, dh, dh), lambda bi, h, ci: (bi, h, 0, 0))
    return pl.pallas_call(
        kern,
        grid=(b, n_grp, t // c),
        in_specs=[pl.BlockSpec(lev.shape, lambda bi, h, ci: (0, 0)),
                  pl.BlockSpec(pair_level.shape, lambda bi, h, ci: (0, 0)),
                  part(0), part(1), part(2), part(3),
                  pl.BlockSpec((1, hps * dh), lambda bi, h, ci: (0, h)),
                  pl.BlockSpec((1, dh), lambda bi, h, ci: (0, 0)),
                  state_spec],
        out_specs=[pl.BlockSpec((None, c, hps * dh), lambda bi, h, ci: (bi, ci, h)), state_spec],
        out_shape=[jax.ShapeDtypeStruct((b, t, n_heads * dh), BF16),
                   jax.ShapeDtypeStruct(s0.shape, F32)],
        scratch_shapes=[pltpu.VMEM((hps, dh, dh), F32)],
        compiler_params=_cparams("parallel", "parallel", "arbitrary"),
        name="hgrn_scan",
    )(lev, pair_level, proj, proj, proj, proj, lower_bound, out_gain, s0)


def _pad_tokens(x, t_pad):
    return jnp.pad(x, ((0, 0), (0, t_pad - x.shape[1])) + ((0, 0),) * (x.ndim - 2))


def kernel(x_prompt, x_sample, cache_k, cache_v, state_mlstm_c, state_mlstm_n, state_mlstm_m,
           state_hgrn, page_table, p_prompt, p_sample, norm_mix, norm_ffn, norm_ple,
           sb_w_qkv, sb_q_norm, sb_k_norm, sb_logit_bias, sb_w_out, ml_w_in, ml_b_gate,
           ml_out_norm, ml_w_out, hg_w_in, hg_lb_logits, hg_out_norm, hg_w_out,
           ffn_w_up, ffn_w_down, ple_w_proj, ple_w_gate):
    depth, d = norm_mix.shape
    bp, tp, _ = x_prompt.shape
    bs, ts, _ = x_sample.shape
    sb_heads = sb_logit_bias.shape[1]
    ml_heads = state_mlstm_c.shape[2]
    ml_dk = state_mlstm_c.shape[3]
    ml_dv = state_mlstm_c.shape[4]
    hg_heads = state_hgrn.shape[2]
    hg_dh = state_hgrn.shape[3]
    n_attn, n_pool, page = cache_k.shape[:3]
    ml_qk_w = ml_heads * ml_dk

    g_mix = norm_mix.reshape(depth, 1, d)
    g_ffn = norm_ffn.reshape(depth, 1, d)
    g_ple = norm_ple.reshape(depth, 1, d)
    lb_soft = jax.nn.softmax(hg_lb_logits.astype(F32), axis=0)
    lower_bounds = jnp.cumsum(lb_soft, axis=0) - lb_soft

    hp = x_prompt.reshape(bp * tp, d)
    hs = x_sample.reshape(bs * ts, d)
    pp_all = p_prompt.reshape(depth, bp * tp, -1)
    ps_all = p_sample.reshape(depth, bs * ts, -1)

    outs = {k: [] for k in ("ks", "vs", "mcp", "mnp", "mmp", "mcs", "mns", "mms", "hgp", "hgs")}
    kp_all = vp_all = None
    i_attn = i_ml = i_hg = 0
    for layer in range(depth):
        kind = layer % N_MIXERS
        up = rmsnorm_bf16(hp, g_mix, layer)
        us = rmsnorm_bf16(hs, g_mix, layer)
        if kind == 0:
            j = i_attn
            i_attn += 1
            gq = sb_q_norm[j].reshape(1, LANES)
            gk = sb_k_norm[j].reshape(1, LANES)
            q_p, q_s = linear("headnorm", up, sb_w_qkv, j, 0, d, shared=(gq,), side=(us, ()))
            kp_all, k_s = linear("headnorm", up, sb_w_qkv, j, d, d, shared=(gk,), side=(us, ()),
                                 slab=(n_attn, j, kp_all))
            vp_all, v_s = linear("plain", up, sb_w_qkv, j, 2 * d, d, side=(us, ()), slab=(n_attn, j, vp_all))
            q_s, k_s, v_s = (a.reshape(bs, ts, sb_heads, LANES) for a in (q_s, k_s, v_s))
            outs["ks"].append(k_s)
            outs["vs"].append(v_s)
            kv = (kp_all.reshape(n_attn, bp, tp, d), vp_all.reshape(n_attn, bp, tp, d))
            ap = sb_attend_prompt(q_p.reshape(bp, tp, d), kv, j, sb_logit_bias[j]).reshape(bp * tp, d)
            a_s = sb_attend_sample(q_s, k_s, v_s, cache_k, cache_v, page_table,
                                   sb_logit_bias[j], j).reshape(bs * ts, d)
            hp, hs = linear("residual", ap, sb_w_out, j, 0, d, per=(hp,), side=(a_s, (hs,)))
        elif kind == 1:
            j = i_ml
            i_ml += 1
            wg_t = ml_w_in[j][:, 2 * ml_qk_w + 2 * d:].T
            b_gate = ml_b_gate[j].reshape(-1, 1)
            gn = ml_out_norm[j].reshape(1, ml_dv)
            n_main = 2 * ml_qk_w + 2 * d
            proj_p, proj_s = linear("plain", up, ml_w_in, j, 0, n_main, side=(us, ()))
            proj_p = proj_p.reshape(bp, tp, n_main)
            proj_s = proj_s.reshape(bs, ts, n_main)
            gates_p = mlstm_gate_layout(ml_gates(up, wg_t, b_gate), bp, tp, ml_heads)
            gates_s = mlstm_gate_layout(ml_gates(us, wg_t, b_gate), bs, ts, ml_heads)
            zc = jnp.zeros((bp, ml_heads, ml_dk, ml_dv), F32)
            zn = jnp.zeros((bp, ml_heads, 1, ml_dk), F32)
            zm = jnp.zeros((bp, ml_heads, 1, LANES), F32)
            yp, cp, npv, mp = mlstm_scan(proj_p, gates_p, gn, zc, zn, zm, ml_heads, CHUNK)
            m0 = jnp.broadcast_to(state_mlstm_m[j][:, :, None, None], (bs, ml_heads, 1, LANES))
            ys, cs, nsv, ms = mlstm_scan(_pad_tokens(proj_s, SUBLANES), gates_s, gn, state_mlstm_c[j],
                                         state_mlstm_n[j][:, :, None, :], m0, ml_heads, ts)
            outs["mcp"].append(cp)
            outs["mnp"].append(npv[:, :, 0, :])
            outs["mmp"].append(mp[:, :, 0, 0])
            outs["mcs"].append(cs)
            outs["mns"].append(nsv[:, :, 0, :])
            outs["mms"].append(ms[:, :, 0, 0])
            hp, hs = linear("residual", yp.reshape(bp * tp, d), ml_w_out, j, 0, d, per=(hp,),
                            side=(ys[:, :ts].reshape(bs * ts, d), (hs,)))
        else:
            j = i_hg
            i_hg += 1
            lb = lower_bounds[layer].reshape(1, d)
            gn = hg_out_norm[j].reshape(1, hg_dh)
            proj_p, proj_s = linear("plain", up, hg_w_in, j, 0, 4 * d, side=(us, ()))
            proj_p = proj_p.reshape(bp, tp, 4 * d)
            proj_s = proj_s.reshape(bs, ts, 4 * d)
            zs = jnp.zeros((bp, hg_heads, hg_dh, hg_dh), F32)
            yp, sp = hgrn_scan(proj_p, lb, gn, zs, hg_heads, CHUNK)
            ys, ss = hgrn_scan(_pad_tokens(proj_s, SUBLANES), lb, gn, state_hgrn[j], hg_heads, ts)
            outs["hgp"].append(sp)
            outs["hgs"].append(ss)
            hp, hs = linear("residual", yp.reshape(bp * tp, d), hg_w_out, j, 0, d, per=(hp,),
                            side=(ys[:, :ts].reshape(bs * ts, d), (hs,)))
        a_p, a_s = linear("relu2", rmsnorm_bf16(hp, g_ffn, layer), ffn_w_up, layer, 0, ffn_w_up.shape[2],
                          side=(rmsnorm_bf16(hs, g_ffn, layer), ()), out_dtype=BF16, col_tile=FFN_COL_TILE)
        hp, hs = linear_residual_acc(a_p, ffn_w_down, layer, hp, a_s, hs)
        hp, hs = linear("ple", hp, ple_w_gate, layer, 0, d, shared=(ple_w_proj,), per=(pp_all, hp),
                        side=(hs, (ps_all, hs)), col_tile=PLE_COL_TILE, norm=(g_ple, layer))

    st = {k: jnp.stack(v) for k, v in outs.items()}
    kv_shape = (n_attn, bp, tp, sb_heads, LANES)
    return (hp.reshape(bp, tp, d), hs.reshape(bs, ts, d), kp_all.reshape(kv_shape), vp_all.reshape(kv_shape),
            st["ks"], st["vs"],
            st["mcp"], st["mnp"], st["mmp"], st["mcs"], st["mns"], st["mms"], st["hgp"], st["hgs"])
```

```python
import functools

import numpy as np
import jax
import jax.numpy as jnp
from jax import lax
from jax.experimental import pallas as pl
from jax.experimental.pallas import tpu as pltpu

F32 = jnp.float32
BF16 = jnp.bfloat16

NORM_EPS = 1e-6
N_MIXERS = 3
GATE_SOFTCAP = 15.0
LANES = 128
SUBLANES = 8
VMEM_LIMIT_BYTES = 56 * 1024 * 1024
ROW_TILE = 2048
COL_TILE = 512
PLE_COL_TILE = 256
FFN_COL_TILE = 1024
ACC_ROW_TILE = 1024
ACC_COL_TILE = 1024
ACC_K_TILE = 2048
SB_Q_TILE = 512
SB_K_TILE = 512
SB_HEADS_PER_STEP = 4
SB_PAGES_PER_STEP = 8
CHUNK = 128
ML_HEADS_PER_STEP = 8
HG_HEADS_PER_STEP = 8
NEG_BIG = -1e30


def _cparams(*sem):
    return pltpu.CompilerParams(dimension_semantics=sem, vmem_limit_bytes=VMEM_LIMIT_BYTES)


def _dot(a, b):
    return jnp.dot(a, b, preferred_element_type=F32)


def _dot_nt(a, b):
    return lax.dot_general(a, b, (((1,), (1,)), ((), ())), preferred_element_type=F32)


def _split_bf16(x):
    hi = x.astype(BF16)
    lo = (x - hi.astype(F32)).astype(BF16)
    return hi, lo


def _softplus(z):
    return jnp.maximum(z, 0.0) + jnp.log(1.0 + jnp.exp(-jnp.abs(z)))


def _pad_rows(x, rows):
    if x.shape[0] == rows:
        return x
    return jnp.concatenate([x, jnp.zeros((rows - x.shape[0], x.shape[1]), x.dtype)], axis=0)


def _rms(x, gain):
    ms = jnp.mean(x * x, axis=-1, keepdims=True)
    return x * lax.rsqrt(ms + NORM_EPS) * gain


def _rmsnorm_kernel(x_ref, g_ref, o_ref):
    o_ref[...] = _rms(x_ref[...], g_ref[...]).astype(o_ref.dtype)


def rmsnorm_bf16(x, gains, layer):
    m, d = x.shape
    tm = min(m, 512)
    return pl.pallas_call(
        _rmsnorm_kernel,
        grid=(m // tm,),
        in_specs=[pl.BlockSpec((tm, d), lambda i: (i, 0)),
                  pl.BlockSpec((None, 1, d), lambda i: (layer, 0, 0))],
        out_specs=pl.BlockSpec((tm, d), lambda i: (i, 0)),
        out_shape=jax.ShapeDtypeStruct((m, d), BF16),
        compiler_params=_cparams("parallel"),
        name="rmsnorm",
    )(x, gains)


_LINEAR_INPUTS = {"plain": (0, 0), "headnorm": (1, 0), "residual": (0, 1), "relu2": (0, 0), "ple": (1, 2)}


def _linear_epilogue(mode, acc, shared, wp, per, cols, dtype):
    if mode == "plain":
        return acc
    if mode == "headnorm":
        return jnp.concatenate([_rms(acc[:, g * LANES:(g + 1) * LANES], shared[0][...])
                                for g in range(acc.shape[1] // LANES)], axis=1)
    if mode == "residual":
        return per[0][:, cols] + acc
    if mode == "relu2":
        a = jnp.maximum(acc, 0.0)
        return (a * a).astype(dtype)
    return per[1][:, cols] + jax.nn.sigmoid(acc) * _dot(per[0][...].astype(BF16), wp)


def _linear_kernel(*refs, mode, has_side, has_norm, first_slab=None):
    n_shared, n_per = _LINEAR_INPUTS[mode]
    n_groups = 2 if has_side else 1
    n_scratch = 1 if has_norm else 0
    w = refs[0][...].astype(BF16)
    shared = refs[1:1 + n_shared]
    pos = 1 + n_shared
    if has_norm:
        gain_ref = refs[pos]
        pos += 1
    groups = [refs[pos + g * (1 + n_per):pos + (g + 1) * (1 + n_per)] for g in range(n_groups)]
    outs = refs[len(refs) - n_groups - n_scratch:len(refs) - n_scratch]
    wp = shared[0][...].astype(BF16) if mode == "ple" else None
    tn = w.shape[1]

    x_ref, *per = groups[0]
    if has_norm:
        u_scr = refs[-1]

        @pl.when(pl.program_id(1) == 0)
        def _():
            u_scr[...] = _rms(x_ref[...], gain_ref[...]).astype(BF16)

        xv = u_scr[...]
    else:
        xv = x_ref[...].astype(BF16)
    acc = _dot(xv, w)
    res = _linear_epilogue(mode, acc, shared, wp, per, slice(None), outs[0].dtype)
    if first_slab is None:
        outs[0][...] = res
    else:
        for s in range(outs[0].shape[0]):
            outs[0][s] = res if s == first_slab else jnp.zeros_like(res)

    if has_side:
        @pl.when(pl.program_id(0) == 0)
        def _():
            xs_ref, *per_s = groups[1]
            cols = pl.ds(pl.multiple_of(pl.program_id(1) * tn, tn), tn)
            xs = _rms(xs_ref[...], gain_ref[...]) if has_norm else xs_ref[...]
            acc_s = _dot(xs.astype(BF16), w)
            outs[1][:, cols] = _linear_epilogue(mode, acc_s, shared, wp, per_s, cols, outs[1].dtype)


def linear(mode, x, w_all, layer, col_start, n_cols, shared=(), per=(), side=None, out_dtype=F32,
           col_tile=COL_TILE, slab=None, norm=None):
    m, k = x.shape
    tn = col_tile
    off = col_start // tn
    tm = min(m, ROW_TILE)
    in_specs = [pl.BlockSpec((None, k, tn), lambda i, j: (layer, 0, off + j))]
    if mode == "headnorm":
        in_specs.append(pl.BlockSpec((1, LANES), lambda i, j: (0, 0)))
    elif mode == "ple":
        in_specs.append(pl.BlockSpec((None, shared[0].shape[1], tn), lambda i, j: (layer, 0, j)))
    args = [w_all, *shared]
    scratch = []
    if norm is None:
        in_specs.append(pl.BlockSpec((tm, k), lambda i, j: (i, 0)))
    else:
        gains, g_idx = norm
        args.append(gains)
        in_specs.append(pl.BlockSpec((None, 1, k), lambda i, j: (g_idx, 0, 0)))
        in_specs.append(pl.BlockSpec((tm, k), lambda i, j: (i, 0), pipeline_mode=pl.Buffered(1)))
        scratch.append(pltpu.VMEM((tm, k), BF16))
    args += [x, *per]
    if mode == "residual":
        in_specs.append(pl.BlockSpec((tm, tn), lambda i, j: (i, j)))
    elif mode == "ple":
        in_specs.append(pl.BlockSpec((None, tm, per[0].shape[2]), lambda i, j: (layer, i, 0)))
        in_specs.append(pl.BlockSpec((tm, tn), lambda i, j: (i, j)))
    out_specs = [pl.BlockSpec((tm, tn), lambda i, j: (i, j))]
    out_shape = [jax.ShapeDtypeStruct((m, n_cols), out_dtype)]
    if side is not None:
        x_s, per_s = side
        ms = x_s.shape[0]
        args += [x_s, *per_s]
        in_specs.append(pl.BlockSpec((ms, k), lambda i, j: (0, 0)))
        if mode == "residual":
            in_specs.append(pl.BlockSpec((ms, n_cols), lambda i, j: (0, 0)))
        elif mode == "ple":
            in_specs.append(pl.BlockSpec((None, ms, per_s[0].shape[2]), lambda i, j: (layer, 0, 0)))
            in_specs.append(pl.BlockSpec((ms, n_cols), lambda i, j: (0, 0)))
        out_specs.append(pl.BlockSpec((ms, n_cols), lambda i, j: (0, 0)))
        out_shape.append(jax.ShapeDtypeStruct((ms, n_cols), out_dtype))
    aliases = {}
    first_slab = None
    if slab is not None:
        n_slabs, index, prev = slab
        out_shape[0] = jax.ShapeDtypeStruct((n_slabs, m, n_cols), out_dtype)
        if prev is None:
            first_slab = index
            out_specs[0] = pl.BlockSpec((n_slabs, tm, tn), lambda i, j: (0, i, j))
        else:
            out_specs[0] = pl.BlockSpec((None, tm, tn), lambda i, j: (index, i, j))
            aliases = {len(args): 0}
            args.append(prev)
            in_specs.append(pl.BlockSpec(memory_space=pl.ANY))
    outs = pl.pallas_call(
        functools.partial(_linear_kernel, mode=mode, has_side=side is not None, has_norm=norm is not None,
                          first_slab=first_slab),
        grid=(m // tm, n_cols // tn),
        in_specs=in_specs,
        out_specs=out_specs,
        out_shape=out_shape,
        scratch_shapes=scratch,
        input_output_aliases=aliases,
        compiler_params=_cparams("arbitrary", "arbitrary"),
        name="linear_" + mode,
    )(*args)
    return outs[0] if side is None else outs


def _linear_acc_kernel(w_ref, x_ref, r_ref, xs_ref, rs_ref, o_ref, os_ref):
    kk = pl.program_id(2)
    w = w_ref[...].astype(BF16)
    tk, tn = w.shape

    @pl.when(kk == 0)
    def _():
        o_ref[...] = r_ref[...]

    o_ref[...] += _dot(x_ref[...], w)

    @pl.when(pl.program_id(0) == 0)
    def _():
        cols = pl.ds(pl.multiple_of(pl.program_id(1) * tn, tn), tn)

        @pl.when(kk == 0)
        def _():
            os_ref[:, cols] = rs_ref[:, cols]

        os_ref[:, cols] += _dot(xs_ref[:, pl.ds(pl.multiple_of(kk * tk, tk), tk)], w)


def linear_residual_acc(x, w_all, layer, res, x_side, res_side):
    m, k = x.shape
    n = res.shape[1]
    ms = x_side.shape[0]
    tm = min(m, ACC_ROW_TILE)
    tn = ACC_COL_TILE
    tk = ACC_K_TILE
    return pl.pallas_call(
        _linear_acc_kernel,
        grid=(m // tm, n // tn, k // tk),
        in_specs=[pl.BlockSpec((None, tk, tn), lambda i, j, kk: (layer, kk, j)),
                  pl.BlockSpec((tm, tk), lambda i, j, kk: (i, kk)),
                  pl.BlockSpec((tm, tn), lambda i, j, kk: (i, j)),
                  pl.BlockSpec((ms, k), lambda i, j, kk: (0, 0)),
                  pl.BlockSpec((ms, n), lambda i, j, kk: (0, 0))],
        out_specs=[pl.BlockSpec((tm, tn), lambda i, j, kk: (i, j)),
                   pl.BlockSpec((ms, n), lambda i, j, kk: (0, 0))],
        out_shape=[jax.ShapeDtypeStruct((m, n), F32), jax.ShapeDtypeStruct((ms, n), F32)],
        compiler_params=_cparams("arbitrary", "arbitrary", "arbitrary"),
        name="linear_acc",
    )(w_all, x, res, x_side, res_side)


def _cumsum_weights():
    row = lax.broadcasted_iota(jnp.int32, (LANES, 2 * LANES), 0)
    col = lax.broadcasted_iota(jnp.int32, (LANES, 2 * LANES), 1)
    return jnp.where((col >= LANES) | (row > col), 1.0, 0.0).astype(BF16)


def _log_sigmoid_pair(z):
    neg_abs = lax.bitcast_convert_type(lax.bitcast_convert_type(z, jnp.uint32) | jnp.uint32(0x80000000), F32)
    log_sig = jnp.minimum(z, 0.0) - jnp.log(1.0 + jnp.exp(neg_abs))
    return log_sig, log_sig - z


def _sb_prompt_kernel(bias_ref, q_ref, k_ref, v_ref, o_ref, acc_scr, carry_scr, *, scale, tq, tk):
    qi = pl.program_id(2)
    hps = acc_scr.shape[0]
    dh = acc_scr.shape[2]
    uo = _cumsum_weights()
    acc_scr[...] = jnp.zeros_like(acc_scr)
    carry_scr[...] = jnp.zeros_like(carry_scr)
    n_diag = tq // tk
    n_blk = tk // LANES

    def tile(ks, masked):
        for hh in range(hps):
            head_tile(hh, ks, masked)

    def head_tile(hh, ks, masked):
        cols = slice(hh * dh, (hh + 1) * dh)
        q = q_ref[:, cols].astype(BF16)
        bias = bias_ref[pl.program_id(1) * hps + hh]
        kb = k_ref[pl.ds(ks, tk), cols].astype(BF16)
        vb = v_ref[pl.ds(ks, tk), cols].astype(BF16)
        z = _dot_nt(q, kb) * scale + bias
        log_sig, log_keep = _log_sigmoid_pair(z)
        if masked:
            q_pos = qi * tq + lax.broadcasted_iota(jnp.int32, (tq, tk), 0)
            k_pos = ks + lax.broadcasted_iota(jnp.int32, (tq, tk), 1)
            mask = k_pos < q_pos
            log_keep = jnp.where(mask, log_keep, 0.0)
        log_keep = log_keep.astype(BF16)
        carry = carry_scr[hh]
        tails = [None] * n_blk
        for c in reversed(range(n_blk)):
            cu = _dot(log_keep[:, c * LANES:(c + 1) * LANES], uo)
            tails[c] = cu[:, :LANES] + carry
            carry = carry + cu[:, LANES:]
        w = jnp.exp(log_sig + jnp.concatenate(tails, axis=1))
        if masked:
            w = jnp.where(mask, w, 0.0)
        acc_scr[hh] += _dot(w.astype(BF16), vb)
        carry_scr[hh] = carry

    for j in range(n_diag):
        tile(pl.multiple_of((qi * n_diag + n_diag - 1 - j) * tk, tk), True)

    def body(it, _):
        tile(pl.multiple_of((qi * n_diag - 1 - it) * tk, tk), False)
        return 0

    lax.fori_loop(0, qi * n_diag, body, 0)
    for hh in range(hps):
        o_ref[:, hh * dh:(hh + 1) * dh] = acc_scr[hh].astype(o_ref.dtype)


def sb_attend_prompt(q, kv, layer, bias):
    b, t, _ = q.shape
    dh = LANES
    n_heads = bias.shape[0]
    hps = SB_HEADS_PER_STEP
    tq = min(t, SB_Q_TILE)
    tk = min(tq, SB_K_TILE)
    kern = functools.partial(_sb_prompt_kernel, scale=dh ** -0.5, tq=tq, tk=tk)
    kv_spec = pl.BlockSpec((None, None, t, hps * dh), lambda bi, h, qi: (layer, bi, 0, h))
    return pl.pallas_call(
        kern,
        grid=(b, n_heads // hps, t // tq),
        in_specs=[pl.BlockSpec(memory_space=pltpu.SMEM),
                  pl.BlockSpec((None, tq, hps * dh), lambda bi, h, qi: (bi, qi, h)),
                  kv_spec, kv_spec],
        out_specs=pl.BlockSpec((None, tq, hps * dh), lambda bi, h, qi: (bi, qi, h)),
        out_shape=jax.ShapeDtypeStruct((b, t, n_heads * dh), BF16),
        scratch_shapes=[pltpu.VMEM((hps, tq, dh), F32), pltpu.VMEM((hps, tq, LANES), F32)],
        compiler_params=_cparams("parallel", "parallel", "arbitrary"),
        name="sb_attend_prompt",
    )(bias, q, kv[0], kv[1])


def _suffix_sum_strided(x, stride):
    w = x.shape[1]
    lane = lax.broadcasted_iota(jnp.int32, x.shape, 1)
    s = stride
    while s < w:
        if s % LANES == 0:
            shifted = jnp.concatenate([x[:, s:], jnp.zeros((x.shape[0], s), F32)], axis=1)
        else:
            shifted = jnp.where(lane < w - s, pltpu.roll(x, w - s, axis=1), 0.0)
        x = x + shifted
        s *= 2
    return x


def _sb_sample_kernel(pt_ref, q_ref, bias_ref, kn_ref, vn_ref, *rest, scale, n_heads, n_new, pages_per_step):
    k_refs = rest[:pages_per_step]
    v_refs = rest[pages_per_step:2 * pages_per_step]
    o_ref, z_scr, w_scr, acc_scr, carry_scr = rest[2 * pages_per_step:]
    s = pl.program_id(1)
    width = kn_ref.shape[0]
    rows = n_new * n_heads
    lane_h = lax.broadcasted_iota(jnp.int32, (n_heads, width), 1) & (n_heads - 1)
    same_head = lane_h == lax.broadcasted_iota(jnp.int32, (n_heads, width), 0)
    row8 = lax.broadcasted_iota(jnp.int32, (SUBLANES, LANES), 0)
    q = q_ref[...].astype(BF16)

    def process(blocks, mask, carry_in):
        r_all = max(SUBLANES, len(blocks) * n_new)
        for p, (k_blk, _) in enumerate(blocks):
            zfull = _dot_nt(q, k_blk.astype(BF16))
            for i in range(n_new):
                part = zfull[i * n_heads:(i + 1) * n_heads, :]
                r = p * n_new + i
                z_scr[r:r + 1, :] = jnp.sum(jnp.where(same_head, part, 0.0), axis=0, keepdims=True)
        z = z_scr[0:r_all, :] * scale + bias_ref[...]
        log_sig, log_keep = _log_sigmoid_pair(z)
        if mask is not None:
            log_keep = jnp.where(mask, log_keep, 0.0)
        incl = _suffix_sum_strided(log_keep, n_heads)
        row128 = lax.broadcasted_iota(jnp.int32, (r_all, LANES), 0)
        lane128 = lax.broadcasted_iota(jnp.int32, (r_all, LANES), 1)
        tot = jnp.where(lane128 < n_heads, incl[:, :LANES], 0.0)
        shift = n_heads
        while shift < LANES:
            tot = tot + pltpu.roll(tot, shift, axis=1)
            shift *= 2
        pre = tot
        shift = n_new
        while shift < r_all:
            pre = pre + jnp.where(row128 >= shift, pltpu.roll(pre, shift, axis=0), 0.0)
            shift *= 2
        carry = jnp.concatenate([carry_in] * (r_all // SUBLANES), axis=0) + (pre - tot)
        w = jnp.exp(log_sig + (incl - log_keep) + jnp.concatenate([carry] * (width // LANES), axis=1))
        if mask is not None:
            w = jnp.where(mask, w, 0.0)
        pv = None
        for p, (_, v_blk) in enumerate(blocks):
            for i in range(n_new):
                r = p * n_new + i
                w_scr[p * rows + i * n_heads:p * rows + (i + 1) * n_heads, :] = jnp.where(
                    same_head, jnp.broadcast_to(w[r:r + 1, :], (n_heads, width)), 0.0).astype(BF16)
            d = _dot(w_scr[p * rows:(p + 1) * rows, :], v_blk.astype(BF16))
            pv = d if pv is None else pv + d
        last = pre[r_all - SUBLANES:, :]
        total = jnp.where(row8 >= n_new, last, pltpu.roll(last, n_new, axis=0))
        return pv, carry_in + total

    @pl.when(s == 0)
    def _():
        z_scr[...] = jnp.zeros_like(z_scr)
        row = lax.broadcasted_iota(jnp.int32, (SUBLANES, width), 0)
        lane = lax.broadcasted_iota(jnp.int32, (SUBLANES, width), 1)
        pv, carry = process([(kn_ref[...], vn_ref[...])], (row < n_new) & (lane < row * n_heads),
                            jnp.zeros(carry_scr.shape, F32))
        acc_scr[...] = pv
        carry_scr[...] = carry

    pv, carry = process([(k_refs[p][...], v_refs[p][...]) for p in range(pages_per_step)], None, carry_scr[...])
    acc_scr[...] += pv
    carry_scr[...] = carry

    @pl.when(s == pl.num_programs(1) - 1)
    def _():
        o_ref[...] = acc_scr[...]


def sb_attend_sample(q, k_new, v_new, cache_k, cache_v, page_table, bias, layer):
    n_seq, n_new, n_heads, dh = q.shape
    assert 2 * n_new == SUBLANES, "two pages of n_new query rows share one 8-sublane array"
    n_pages = page_table.shape[1]
    n_attn, n_pool, page = cache_k.shape[:3]
    width = page * n_heads
    pps = SB_PAGES_PER_STEP
    rows = n_new * n_heads
    pad = ((0, 0), (0, page - n_new), (0, 0), (0, 0))
    k_pad = jnp.pad(k_new, pad).reshape(n_seq, width, dh)
    v_pad = jnp.pad(v_new, pad).reshape(n_seq, width, dh)
    ck = cache_k.reshape(n_attn, n_pool, width, dh)
    cv = cache_v.reshape(n_attn, n_pool, width, dh)
    bias_lanes = jnp.tile(bias.astype(F32), page).reshape(1, width)

    def page_spec(j):
        return pl.BlockSpec((None, None, width, dh),
                            lambda b, s, pt: (layer, pt[b, n_pages - 1 - (s * pps + j)], 0, 0))

    seq_spec = pl.BlockSpec((None, rows, dh), lambda b, s, pt: (b, 0, 0))
    new_spec = pl.BlockSpec((None, width, dh), lambda b, s, pt: (b, 0, 0))
    kern = functools.partial(_sb_sample_kernel, scale=dh ** -0.5, n_heads=n_heads, n_new=n_new,
                             pages_per_step=pps)
    grid_spec = pltpu.PrefetchScalarGridSpec(
        num_scalar_prefetch=1,
        grid=(n_seq, n_pages // pps),
        in_specs=[seq_spec, pl.BlockSpec((1, width), lambda b, s, pt: (0, 0)), new_spec, new_spec]
                 + [page_spec(j) for j in range(pps)] * 2,
        out_specs=seq_spec,
        scratch_shapes=[pltpu.VMEM((pps * n_new, width), F32), pltpu.VMEM((pps * rows, width), BF16),
                        pltpu.VMEM((rows, dh), F32), pltpu.VMEM((SUBLANES, LANES), F32)],
    )
    out = pl.pallas_call(
        kern,
        grid_spec=grid_spec,
        out_shape=jax.ShapeDtypeStruct((n_seq, rows, dh), F32),
        compiler_params=_cparams("parallel", "arbitrary"),
        name="sb_attend_sample",
    )(page_table, q.reshape(n_seq, rows, dh), bias_lanes, k_pad, v_pad, *([ck] * pps), *([cv] * pps))
    return out.reshape(n_seq, n_new, n_heads * dh)


def _ml_gates_kernel(u_ref, wg_ref, b_ref, o_ref, *, n_heads):
    pre = _dot_nt(wg_ref[...].astype(BF16), u_ref[...]) + b_ref[...]
    pre = GATE_SOFTCAP * jnp.tanh(pre / GATE_SOFTCAP)
    o_ref[0:n_heads, :] = pre[0:n_heads]
    o_ref[n_heads:, :] = -_softplus(-pre[n_heads:])


def ml_gates(u, wg_t, b_gate):
    m, d = u.shape
    g = wg_t.shape[0]
    tm = min(m, ROW_TILE)
    return pl.pallas_call(
        functools.partial(_ml_gates_kernel, n_heads=g // 2),
        grid=(m // tm,),
        in_specs=[pl.BlockSpec((tm, d), lambda i: (i, 0)),
                  pl.BlockSpec((g, d), lambda i: (0, 0)),
                  pl.BlockSpec((g, 1), lambda i: (0, 0))],
        out_specs=pl.BlockSpec((g, tm), lambda i: (0, i)),
        out_shape=jax.ShapeDtypeStruct((g, m), F32),
        compiler_params=_cparams("parallel"),
        name="ml_gates",
    )(u, wg_t, b_gate)


def mlstm_gate_layout(g, b, t, n_heads):
    hps = ML_HEADS_PER_STEP
    t_pad = -(-t // CHUNK) * CHUNK
    g = jnp.pad(g.reshape(2, n_heads // hps, hps, b, t), ((0, 0),) * 4 + ((0, t_pad - t),))
    g = g.reshape(2, n_heads // hps, hps, b, t_pad // CHUNK, CHUNK).transpose(3, 4, 1, 2, 0, 5)
    return g.reshape(b, t_pad // CHUNK, n_heads // hps, 2 * hps, CHUNK)


def _lane_cumsum(x):
    lane = lax.broadcasted_iota(jnp.int32, x.shape, 1)
    shift = 1
    while shift < x.shape[1]:
        x = x + jnp.where(lane >= shift, pltpu.roll(x, shift, axis=1), 0.0)
        shift *= 2
    return x


def _mlstm_kernel(g_ref, q_ref, k_ref, v_ref, og_ref, gn_ref, c0_ref, n0_ref, m0_ref,
                  y_ref, c_out, n_out, m_out, c_scr, n_scr, m_scr, *, n_valid, k_scale):
    ci = pl.program_id(2)
    L = g_ref.shape[1]
    t_in = q_ref.shape[0]
    hps = c_scr.shape[0]
    dk = c_scr.shape[1]
    dv = c_scr.shape[2]

    @pl.when(ci == 0)
    def _():
        c_scr[...] = c0_ref[...]
        n_scr[...] = n0_ref[...]
        m_scr[...] = m0_ref[...]

    g = g_ref[...]
    row8 = lax.broadcasted_iota(jnp.int32, (2 * hps, L), 0)
    if n_valid < L:
        valid = lax.broadcasted_iota(jnp.int32, (2 * hps, L), 1) < n_valid
        g = jnp.where(valid, g, jnp.where((row8 & 1) == 0, NEG_BIG, 0.0))
    bc = _lane_cumsum(g)
    rows = jnp.concatenate([jnp.where((row8 & 1) == 0, g, bc), jnp.zeros((L - 2 * hps, L), F32)], axis=0)
    cols = rows.T
    t_idx = lax.broadcasted_iota(jnp.int32, (L, L), 0)
    s_idx = lax.broadcasted_iota(jnp.int32, (L, L), 1)
    causal = s_idx <= t_idx

    for hh in range(hps):
        li_r = g[2 * hh:2 * hh + 1, :]
        bc_r = bc[2 * hh + 1:2 * hh + 2, :]
        li_c = cols[:, 2 * hh:2 * hh + 1]
        bc_c = cols[:, 2 * hh + 1:2 * hh + 2]
        m_prev = m_scr[hh, 0:1, 0:1]
        dmat = jnp.where(causal, bc_c + (li_r - bc_r), NEG_BIG)
        inter = bc_c + m_prev
        m_t = jnp.maximum(inter, jnp.max(dmat, axis=1, keepdims=True))

        qf = _pad_rows(q_ref[:, hh * dk:(hh + 1) * dk], L)
        q = qf.astype(BF16)
        ks_f = _pad_rows(k_ref[:, hh * dk:(hh + 1) * dk], L) * k_scale
        v = _pad_rows(v_ref[:, hh * dv:(hh + 1) * dv], L).astype(BF16)
        a = jnp.exp(dmat - m_t) * _dot_nt(q, ks_f.astype(BF16))
        sc = jnp.exp(inter - m_t)
        c = c_scr[hh]
        n_row = n_scr[hh]
        num = _dot(a.astype(BF16), v) + sc * _dot(q, c.astype(BF16))
        den = jnp.sum(a, axis=1, keepdims=True) + sc * jnp.sum(qf * n_row, axis=1, keepdims=True)
        hval = num / jnp.maximum(jnp.abs(den), jnp.exp(-m_t))
        gate = jax.nn.sigmoid(og_ref[:, hh * dv:(hh + 1) * dv])
        y_ref[:, hh * dv:(hh + 1) * dv] = (gate * _rms(hval[:t_in], gn_ref[...])).astype(y_ref.dtype)

        m_new = m_t[L - 1:L, :]
        bc_last = bc_c[L - 1:L, :]
        wl = jnp.exp(bc_last - bc_c + li_c - m_new)
        sl = jnp.exp(bc_last + m_prev - m_new)
        kw = ks_f * wl
        c_scr[hh] = sl * c + _dot(kw.T.astype(BF16), v)
        n_scr[hh] = sl * n_row + jnp.sum(kw, axis=0, keepdims=True)
        m_scr[hh] = jnp.broadcast_to(m_new, (1, LANES))

    @pl.when(ci == pl.num_programs(2) - 1)
    def _():
        c_out[...] = c_scr[...]
        n_out[...] = n_scr[...]
        m_out[...] = m_scr[...]


def mlstm_scan(proj, gates, out_gain, c0, n0, m0, n_heads, n_valid):
    b, t, _ = proj.shape
    L = min(t, CHUNK)
    dk = c0.shape[2]
    dv = c0.shape[3]
    hps = ML_HEADS_PER_STEP
    n_grp = n_heads // hps
    v_blk = 2 * n_heads * dk // (hps * dv)
    kern = functools.partial(_mlstm_kernel, n_valid=n_valid, k_scale=dk ** -0.5)
    state_specs = [pl.BlockSpec((None, hps, dk, dv), lambda bi, h, ci: (bi, h, 0, 0)),
                   pl.BlockSpec((None, hps, 1, dk), lambda bi, h, ci: (bi, h, 0, 0)),
                   pl.BlockSpec((None, hps, 1, LANES), lambda bi, h, ci: (bi, h, 0, 0))]
    return pl.pallas_call(
        kern,
        grid=(b, n_grp, t // L),
        in_specs=[pl.BlockSpec((None, None, None, 2 * hps, CHUNK), lambda bi, h, ci: (bi, ci, h, 0, 0)),
                  pl.BlockSpec((None, L, hps * dk), lambda bi, h, ci: (bi, ci, h)),
                  pl.BlockSpec((None, L, hps * dk), lambda bi, h, ci: (bi, ci, n_grp + h)),
                  pl.BlockSpec((None, L, hps * dv), lambda bi, h, ci: (bi, ci, v_blk + h)),
                  pl.BlockSpec((None, L, hps * dv), lambda bi, h, ci: (bi, ci, v_blk + n_grp + h)),
                  pl.BlockSpec((1, dv), lambda bi, h, ci: (0, 0))] + state_specs,
        out_specs=[pl.BlockSpec((None, L, hps * dv), lambda bi, h, ci: (bi, ci, h))] + state_specs,
        out_shape=[jax.ShapeDtypeStruct((b, t, n_heads * dv), BF16),
                   jax.ShapeDtypeStruct(c0.shape, F32),
                   jax.ShapeDtypeStruct(n0.shape, F32),
                   jax.ShapeDtypeStruct(m0.shape, F32)],
        scratch_shapes=[pltpu.VMEM((hps, dk, dv), F32), pltpu.VMEM((hps, 1, dk), F32),
                        pltpu.VMEM((hps, 1, LANES), F32)],
        compiler_params=_cparams("parallel", "parallel", "arbitrary"),
        name="mlstm_scan",
    )(gates, proj, proj, proj, proj, out_gain, c0, n0, m0)


def _hgrn_tables(c):
    n_lev = int(np.log2(c))
    lev = np.zeros(((n_lev + 1) * c, c), np.float32)
    lev[:c] = np.tril(np.ones((c, c), np.float32))
    for l in range(1, n_lev + 1):
        m = 1 << (l - 1)
        for t in range(c):
            mid = (t >> l << l) + m
            if t & m:
                lev[l * c + t, mid:t + 1] = 1.0
            else:
                lev[l * c + t, t + 1:mid] = 1.0
    t = np.arange(c)[:, None]
    s = np.arange(c)[None, :]
    x = t ^ s
    pair_level = np.where(s > t, -1, np.where(s == t, 0, np.floor(np.log2(np.maximum(x, 1))).astype(np.int32) + 1))
    return jnp.asarray(np.concatenate([lev, lev], axis=1), BF16), jnp.asarray(pair_level, jnp.int32)


def _hgrn_kernel(lev_ref, lvl_ref, q_ref, f_ref, i_ref, g_ref, lb_ref, gn_ref, s0_ref,
                 y_ref, s_out, st_scr, *, n_valid):
    ci = pl.program_id(2)
    c = lvl_ref.shape[0]
    t_in = q_ref.shape[0]
    n_lev = lev_ref.shape[0] // c - 1
    hps = st_scr.shape[0]
    dh = st_scr.shape[1]

    @pl.when(ci == 0)
    def _():
        for hh in range(hps):
            st_scr[hh] = s0_ref[hh].T

    fp = _pad_rows(f_ref[...], c)
    lb = lb_ref[...]
    log_sig = -_softplus(-fp)
    la = jnp.log(lb)
    lbb = jnp.log1p(-lb) + log_sig
    lf = jnp.maximum(la, lbb) + jnp.log(1.0 + jnp.exp(-jnp.abs(la - lbb)))
    kk = (1.0 - lb) * jax.nn.sigmoid(-fp)
    if n_valid < c:
        valid = lax.broadcasted_iota(jnp.int32, (c, 1), 0) < n_valid
        lf = jnp.where(valid, lf, 0.0)
        kk = jnp.where(valid, kk, 0.0)

    hi, lo = _split_bf16(lf)
    d_all = _dot(lev_ref[...], jnp.concatenate([hi, lo], axis=0))
    bcum = d_all[0:c]
    q = _pad_rows(q_ref[...], c)
    iv_f = _pad_rows(i_ref[...], c)
    lvl = lvl_ref[...]
    heads = [slice(hh * dh, (hh + 1) * dh) for hh in range(hps)]
    qb = q.astype(BF16)
    kb = kk.astype(BF16)
    a = [jnp.where(lvl == 0, _dot_nt(qb[:, sl], kb[:, sl]), 0.0) for sl in heads]
    for l in range(1, n_lev + 1):
        e = jnp.exp(d_all[l * c:(l + 1) * c])
        qe = (q * e).astype(BF16)
        ke = (kk * e).astype(BF16)
        a = [a[hh] + jnp.where(lvl == l, _dot_nt(qe[:, sl], ke[:, sl]), 0.0) for hh, sl in enumerate(heads)]
    blast = bcum[c - 1:c, :]
    q_in = (q * jnp.exp(bcum)).astype(BF16)
    k_out = (kk * jnp.exp(blast - bcum)).astype(BF16)
    decay = jnp.exp(blast)
    gate = jax.nn.sigmoid(g_ref[...])
    for hh, sl in enumerate(heads):
        st = st_scr[hh]
        o = _dot(a[hh].astype(BF16), iv_f[:, sl].astype(BF16)) + _dot_nt(q_in[:, sl], st.astype(BF16))
        y_ref[:, sl] = (gate[:, sl] * _rms(o[:t_in], gn_ref[...])).astype(y_ref.dtype)
        st_scr[hh] = decay[:, sl] * st + _dot(iv_f[:, sl].T.astype(BF16), k_out[:, sl])

    @pl.when(ci == pl.num_programs(2) - 1)
    def _():
        for hh in range(hps):
            s_out[hh] = st_scr[hh].T


def hgrn_scan(proj, lower_bound, out_gain, s0, n_heads, n_valid):
    b, t, _ = proj.shape
    c = min(t, CHUNK)
    dh = s0.shape[2]
    hps = HG_HEADS_PER_STEP
    n_grp = n_heads // hps
    lev, pair_level = _hgrn_tables(CHUNK)
    kern = functools.partial(_hgrn_kernel, n_valid=n_valid)

    def part(p):
        return pl.BlockSpec((None, c, hps * dh), lambda bi, h, ci: (bi, ci, p * n_grp + h))

    state_spec = pl.BlockSpec((None, hps, dh, dh), lambda bi, h, ci: (bi, h, 0, 0))
    return pl.pallas_call(
        kern,
        grid=(b, n_grp, t // c),
        in_specs=[pl.BlockSpec(lev.shape, lambda bi, h, ci: (0, 0)),
                  pl.BlockSpec(pair_level.shape, lambda bi, h, ci: (0, 0)),
                  part(0), part(1), part(2), part(3),
                  pl.BlockSpec((1, hps * dh), lambda bi, h, ci: (0, h)),
                  pl.BlockSpec((1, dh), lambda bi, h, ci: (0, 0)),
                  state_spec],
        out_specs=[pl.BlockSpec((None, c, hps * dh), lambda bi, h, ci: (bi, ci, h)), state_spec],
        out_shape=[jax.ShapeDtypeStruct((b, t, n_heads * dh), BF16),
                   jax.ShapeDtypeStruct(s0.shape, F32)],
        scratch_shapes=[pltpu.VMEM((hps, dh, dh), F32)],
        compiler_params=_cparams("parallel", "parallel", "arbitrary"),
        name="hgrn_scan",
    )(lev, pair_level, proj, proj, proj, proj, lower_bound, out_gain, s0)


def _pad_tokens(x, t_pad):
    return jnp.pad(x, ((0, 0), (0, t_pad - x.shape[1])) + ((0, 0),) * (x.ndim - 2))


def kernel(x_prompt, x_sample, cache_k, cache_v, state_mlstm_c, state_mlstm_n, state_mlstm_m,
           state_hgrn, page_table, p_prompt, p_sample, norm_mix, norm_ffn, norm_ple,
           sb_w_qkv, sb_q_norm, sb_k_norm, sb_logit_bias, sb_w_out, ml_w_in, ml_b_gate,
           ml_out_norm, ml_w_out, hg_w_in, hg_lb_logits, hg_out_norm, hg_w_out,
           ffn_w_up, ffn_w_down, ple_w_proj, ple_w_gate):
    depth, d = norm_mix.shape
    bp, tp, _ = x_prompt.shape
    bs, ts, _ = x_sample.shape
    sb_heads = sb_logit_bias.shape[1]
    ml_heads = state_mlstm_c.shape[2]
    ml_dk = state_mlstm_c.shape[3]
    ml_dv = state_mlstm_c.shape[4]
    hg_heads = state_hgrn.shape[2]
    hg_dh = state_hgrn.shape[3]
    n_attn, n_pool, page = cache_k.shape[:3]
    ml_qk_w = ml_heads * ml_dk

    g_mix = norm_mix.reshape(depth, 1, d)
    g_ffn = norm_ffn.reshape(depth, 1, d)
    g_ple = norm_ple.reshape(depth, 1, d)
    lb_soft = jax.nn.softmax(hg_lb_logits.astype(F32), axis=0)
    lower_bounds = jnp.cumsum(lb_soft, axis=0) - lb_soft

    hp = x_prompt.reshape(bp * tp, d)
    hs = x_sample.reshape(bs * ts, d)
    pp_all = p_prompt.reshape(depth, bp * tp, -1)
    ps_all = p_sample.reshape(depth, bs * ts, -1)

    outs = {k: [] for k in ("ks", "vs", "mcp", "mnp", "mmp", "mcs", "mns", "mms", "hgp", "hgs")}
    kp_all = vp_all = None
    i_attn = i_ml = i_hg = 0
    for layer in range(depth):
        kind = layer % N_MIXERS
        up = rmsnorm_bf16(hp, g_mix, layer)
        us = rmsnorm_bf16(hs, g_mix, layer)
        if kind == 0:
            j = i_attn
            i_attn += 1
            gq = sb_q_norm[j].reshape(1, LANES)
            gk = sb_k_norm[j].reshape(1, LANES)
            q_p, q_s = linear("headnorm", up, sb_w_qkv, j, 0, d, shared=(gq,), side=(us, ()))
            kp_all, k_s = linear("headnorm", up, sb_w_qkv, j, d, d, shared=(gk,), side=(us, ()),
                                 slab=(n_attn, j, kp_all))
            vp_all, v_s = linear("plain", up, sb_w_qkv, j, 2 * d, d, side=(us, ()), slab=(n_attn, j, vp_all))
            q_s, k_s, v_s = (a.reshape(bs, ts, sb_heads, LANES) for a in (q_s, k_s, v_s))
            outs["ks"].append(k_s)
            outs["vs"].append(v_s)
            kv = (kp_all.reshape(n_attn, bp, tp, d), vp_all.reshape(n_attn, bp, tp, d))
            ap = sb_attend_prompt(q_p.reshape(bp, tp, d), kv, j, sb_logit_bias[j]).reshape(bp * tp, d)
            a_s = sb_attend_sample(q_s, k_s, v_s, cache_k, cache_v, page_table,
                                   sb_logit_bias[j], j).reshape(bs * ts, d)
            hp, hs = linear("residual", ap, sb_w_out, j, 0, d, per=(hp,), side=(a_s, (hs,)))
        elif kind == 1:
            j = i_ml
            i_ml += 1
            wg_t = ml_w_in[j][:, 2 * ml_qk_w + 2 * d:].T
            b_gate = ml_b_gate[j].reshape(-1, 1)
            gn = ml_out_norm[j].reshape(1, ml_dv)
            n_main = 2 * ml_qk_w + 2 * d
            proj_p, proj_s = linear("plain", up, ml_w_in, j, 0, n_main, side=(us, ()))
            proj_p = proj_p.reshape(bp, tp, n_main)
            proj_s = proj_s.reshape(bs, ts, n_main)
            gates_p = mlstm_gate_layout(ml_gates(up, wg_t, b_gate), bp, tp, ml_heads)
            gates_s = mlstm_gate_layout(ml_gates(us, wg_t, b_gate), bs, ts, ml_heads)
            zc = jnp.zeros((bp, ml_heads, ml_dk, ml_dv), F32)
            zn = jnp.zeros((bp, ml_heads, 1, ml_dk), F32)
            zm = jnp.zeros((bp, ml_heads, 1, LANES), F32)
            yp, cp, npv, mp = mlstm_scan(proj_p, gates_p, gn, zc, zn, zm, ml_heads, CHUNK)
            m0 = jnp.broadcast_to(state_mlstm_m[j][:, :, None, None], (bs, ml_heads, 1, LANES))
            ys, cs, nsv, ms = mlstm_scan(_pad_tokens(proj_s, SUBLANES), gates_s, gn, state_mlstm_c[j],
                                         state_mlstm_n[j][:, :, None, :], m0, ml_heads, ts)
            outs["mcp"].append(cp)
            outs["mnp"].append(npv[:, :, 0, :])
            outs["mmp"].append(mp[:, :, 0, 0])
            outs["mcs"].append(cs)
            outs["mns"].append(nsv[:, :, 0, :])
            outs["mms"].append(ms[:, :, 0, 0])
            hp, hs = linear("residual", yp.reshape(bp * tp, d), ml_w_out, j, 0, d, per=(hp,),
                            side=(ys[:, :ts].reshape(bs * ts, d), (hs,)))
        else:
            j = i_hg
            i_hg += 1
            lb = lower_bounds[layer].reshape(1, d)
            gn = hg_out_norm[j].reshape(1, hg_dh)
            proj_p, proj_s = linear("plain", up, hg_w_in, j, 0, 4 * d, side=(us, ()))
            proj_p = proj_p.reshape(bp, tp, 4 * d)
            proj_s = proj_s.reshape(bs, ts, 4 * d)
            zs = jnp.zeros((bp, hg_heads, hg_dh, hg_dh), F32)
            yp, sp = hgrn_scan(proj_p, lb, gn, zs, hg_heads, CHUNK)
            ys, ss = hgrn_scan(_pad_tokens(proj_s, SUBLANES), lb, gn, state_hgrn[j], hg_heads, ts)
            outs["hgp"].append(sp)
            outs["hgs"].append(ss)
            hp, hs = linear("residual", yp.reshape(bp * tp, d), hg_w_out, j, 0, d, per=(hp,),
                            side=(ys[:, :ts].reshape(bs * ts, d), (hs,)))
        a_p, a_s = linear("relu2", rmsnorm_bf16(hp, g_ffn, layer), ffn_w_up, layer, 0, ffn_w_up.shape[2],
                          side=(rmsnorm_bf16(hs, g_ffn, layer), ()), out_dtype=BF16, col_tile=FFN_COL_TILE)
        hp, hs = linear_residual_acc(a_p, ffn_w_down, layer, hp, a_s, hs)
        hp, hs = linear("ple", hp, ple_w_gate, layer, 0, d, shared=(ple_w_proj,), per=(pp_all, hp),
                        side=(hs, (ps_all, hs)), col_tile=PLE_COL_TILE, norm=(g_ple, layer))

    st = {k: jnp.stack(v) for k, v in outs.items()}
    kv_shape = (n_attn, bp, tp, sb_heads, LANES)
    return (hp.reshape(bp, tp, d), hs.reshape(bs, ts, d), kp_all.reshape(kv_shape), vp_all.reshape(kv_shape),
            st["ks"], st["vs"],
            st["mcp"], st["mnp"], st["mmp"], st["mcs"], st["mns"], st["mms"], st["hgp"], st["hgs"])
```

```python
import functools

import numpy as np
import jax
import jax.numpy as jnp
from jax import lax
from jax.experimental import pallas as pl
from jax.experimental.pallas import tpu as pltpu

F32 = jnp.float32
BF16 = jnp.bfloat16

NORM_EPS = 1e-6
N_MIXERS = 3
GATE_SOFTCAP = 15.0
LANES = 128
SUBLANES = 8
VMEM_LIMIT_BYTES = 56 * 1024 * 1024
ROW_TILE = 2048
COL_TILE = 512
PLE_COL_TILE = 256
FFN_COL_TILE = 1024
ACC_ROW_TILE = 2048
ACC_COL_TILE = 512
ACC_K_TILE = 2048
SB_Q_TILE = 512
SB_K_TILE = 512
SB_HEADS_PER_STEP = 4
SB_PAGES_PER_STEP = 8
CHUNK = 128
ML_HEADS_PER_STEP = 8
HG_HEADS_PER_STEP = 16
NEG_BIG = -1e30


def _cparams(*sem):
    return pltpu.CompilerParams(dimension_semantics=sem, vmem_limit_bytes=VMEM_LIMIT_BYTES)


def _dot(a, b):
    return jnp.dot(a, b, preferred_element_type=F32)


def _dot_nt(a, b):
    return lax.dot_general(a, b, (((1,), (1,)), ((), ())), preferred_element_type=F32)


def _split_bf16(x):
    hi = x.astype(BF16)
    lo = (x - hi.astype(F32)).astype(BF16)
    return hi, lo


def _softplus(z):
    return jnp.maximum(z, 0.0) + jnp.log(1.0 + jnp.exp(-jnp.abs(z)))


def _pad_rows(x, rows):
    if x.shape[0] == rows:
        return x
    return jnp.concatenate([x, jnp.zeros((rows - x.shape[0], x.shape[1]), x.dtype)], axis=0)


def _rms(x, gain):
    ms = jnp.mean(x * x, axis=-1, keepdims=True)
    return x * lax.rsqrt(ms + NORM_EPS) * gain


def _rmsnorm_kernel(x_ref, g_ref, o_ref):
    o_ref[...] = _rms(x_ref[...], g_ref[...]).astype(o_ref.dtype)


def rmsnorm_bf16(x, gains, layer):
    m, d = x.shape
    tm = min(m, 512)
    return pl.pallas_call(
        _rmsnorm_kernel,
        grid=(m // tm,),
        in_specs=[pl.BlockSpec((tm, d), lambda i: (i, 0)),
                  pl.BlockSpec((None, 1, d), lambda i: (layer, 0, 0))],
        out_specs=pl.BlockSpec((tm, d), lambda i: (i, 0)),
        out_shape=jax.ShapeDtypeStruct((m, d), BF16),
        compiler_params=_cparams("parallel"),
        name="rmsnorm",
    )(x, gains)


_LINEAR_INPUTS = {"plain": (0, 0), "headnorm": (1, 0), "residual": (0, 1), "relu2": (0, 0), "ple": (1, 2)}


def _linear_epilogue(mode, acc, shared, wp, per, cols, dtype):
    if mode == "plain":
        return acc
    if mode == "headnorm":
        return jnp.concatenate([_rms(acc[:, g * LANES:(g + 1) * LANES], shared[0][...])
                                for g in range(acc.shape[1] // LANES)], axis=1)
    if mode == "residual":
        return per[0][:, cols] + acc
    if mode == "relu2":
        a = jnp.maximum(acc, 0.0)
        return (a * a).astype(dtype)
    return per[1][:, cols] + jax.nn.sigmoid(acc) * _dot(per[0][...].astype(BF16), wp)


def _linear_kernel(*refs, mode, has_side, has_norm, first_slab=None):
    n_shared, n_per = _LINEAR_INPUTS[mode]
    n_groups = 2 if has_side else 1
    n_scratch = 1 if has_norm else 0
    w = refs[0][...].astype(BF16)
    shared = refs[1:1 + n_shared]
    pos = 1 + n_shared
    if has_norm:
        gain_ref = refs[pos]
        pos += 1
    groups = [refs[pos + g * (1 + n_per):pos + (g + 1) * (1 + n_per)] for g in range(n_groups)]
    outs = refs[len(refs) - n_groups - n_scratch:len(refs) - n_scratch]
    wp = shared[0][...].astype(BF16) if mode == "ple" else None
    tn = w.shape[1]

    x_ref, *per = groups[0]
    if has_norm:
        u_scr = refs[-1]

        @pl.when(pl.program_id(1) == 0)
        def _():
            u_scr[...] = _rms(x_ref[...], gain_ref[...]).astype(BF16)

        xv = u_scr[...]
    else:
        xv = x_ref[...].astype(BF16)
    acc = _dot(xv, w)
    res = _linear_epilogue(mode, acc, shared, wp, per, slice(None), outs[0].dtype)
    if first_slab is None:
        outs[0][...] = res
    else:
        for s in range(outs[0].shape[0]):
            outs[0][s] = res if s == first_slab else jnp.zeros_like(res)

    if has_side:
        @pl.when(pl.program_id(0) == 0)
        def _():
            xs_ref, *per_s = groups[1]
            cols = pl.ds(pl.multiple_of(pl.program_id(1) * tn, tn), tn)
            xs = _rms(xs_ref[...], gain_ref[...]) if has_norm else xs_ref[...]
            acc_s = _dot(xs.astype(BF16), w)
            outs[1][:, cols] = _linear_epilogue(mode, acc_s, shared, wp, per_s, cols, outs[1].dtype)


def linear(mode, x, w_all, layer, col_start, n_cols, shared=(), per=(), side=None, out_dtype=F32,
           col_tile=COL_TILE, slab=None, norm=None):
    m, k = x.shape
    tn = col_tile
    off = col_start // tn
    tm = min(m, ROW_TILE)
    in_specs = [pl.BlockSpec((None, k, tn), lambda i, j: (layer, 0, off + j))]
    if mode == "headnorm":
        in_specs.append(pl.BlockSpec((1, LANES), lambda i, j: (0, 0)))
    elif mode == "ple":
        in_specs.append(pl.BlockSpec((None, shared[0].shape[1], tn), lambda i, j: (layer, 0, j)))
    args = [w_all, *shared]
    scratch = []
    if norm is None:
        in_specs.append(pl.BlockSpec((tm, k), lambda i, j: (i, 0)))
    else:
        gains, g_idx = norm
        args.append(gains)
        in_specs.append(pl.BlockSpec((None, 1, k), lambda i, j: (g_idx, 0, 0)))
        in_specs.append(pl.BlockSpec((tm, k), lambda i, j: (i, 0), pipeline_mode=pl.Buffered(1)))
        scratch.append(pltpu.VMEM((tm, k), BF16))
    args += [x, *per]
    if mode == "residual":
        in_specs.append(pl.BlockSpec((tm, tn), lambda i, j: (i, j)))
    elif mode == "ple":
        in_specs.append(pl.BlockSpec((None, tm, per[0].shape[2]), lambda i, j: (layer, i, 0)))
        in_specs.append(pl.BlockSpec((tm, tn), lambda i, j: (i, j)))
    out_specs = [pl.BlockSpec((tm, tn), lambda i, j: (i, j))]
    out_shape = [jax.ShapeDtypeStruct((m, n_cols), out_dtype)]
    if side is not None:
        x_s, per_s = side
        ms = x_s.shape[0]
        args += [x_s, *per_s]
        in_specs.append(pl.BlockSpec((ms, k), lambda i, j: (0, 0)))
        if mode == "residual":
            in_specs.append(pl.BlockSpec((ms, n_cols), lambda i, j: (0, 0)))
        elif mode == "ple":
            in_specs.append(pl.BlockSpec((None, ms, per_s[0].shape[2]), lambda i, j: (layer, 0, 0)))
            in_specs.append(pl.BlockSpec((ms, n_cols), lambda i, j: (0, 0)))
        out_specs.append(pl.BlockSpec((ms, n_cols), lambda i, j: (0, 0)))
        out_shape.append(jax.ShapeDtypeStruct((ms, n_cols), out_dtype))
    aliases = {}
    first_slab = None
    if slab is not None:
        n_slabs, index, prev = slab
        out_shape[0] = jax.ShapeDtypeStruct((n_slabs, m, n_cols), out_dtype)
        if prev is None:
            first_slab = index
            out_specs[0] = pl.BlockSpec((n_slabs, tm, tn), lambda i, j: (0, i, j))
        else:
            out_specs[0] = pl.BlockSpec((None, tm, tn), lambda i, j: (index, i, j))
            aliases = {len(args): 0}
            args.append(prev)
            in_specs.append(pl.BlockSpec(memory_space=pl.ANY))
    outs = pl.pallas_call(
        functools.partial(_linear_kernel, mode=mode, has_side=side is not None, has_norm=norm is not None,
                          first_slab=first_slab),
        grid=(m // tm, n_cols // tn),
        in_specs=in_specs,
        out_specs=out_specs,
        out_shape=out_shape,
        scratch_shapes=scratch,
        input_output_aliases=aliases,
        compiler_params=_cparams("arbitrary", "arbitrary"),
        name="linear_" + mode,
    )(*args)
    return outs[0] if side is None else outs


def _linear_acc_kernel(w_ref, x_ref, r_ref, xs_ref, rs_ref, o_ref, os_ref):
    kk = pl.program_id(2)
    w = w_ref[...].astype(BF16)
    tk, tn = w.shape

    @pl.when(kk == 0)
    def _():
        o_ref[...] = r_ref[...]

    o_ref[...] += _dot(x_ref[...], w)

    @pl.when(pl.program_id(0) == 0)
    def _():
        cols = pl.ds(pl.multiple_of(pl.program_id(1) * tn, tn), tn)

        @pl.when(kk == 0)
        def _():
            os_ref[:, cols] = rs_ref[:, cols]

        os_ref[:, cols] += _dot(xs_ref[:, pl.ds(pl.multiple_of(kk * tk, tk), tk)], w)


def linear_residual_acc(x, w_all, layer, res, x_side, res_side):
    m, k = x.shape
    n = res.shape[1]
    ms = x_side.shape[0]
    tm = min(m, ACC_ROW_TILE)
    tn = ACC_COL_TILE
    tk = ACC_K_TILE
    return pl.pallas_call(
        _linear_acc_kernel,
        grid=(m // tm, n // tn, k // tk),
        in_specs=[pl.BlockSpec((None, tk, tn), lambda i, j, kk: (layer, kk, j)),
                  pl.BlockSpec((tm, tk), lambda i, j, kk: (i, kk)),
                  pl.BlockSpec((tm, tn), lambda i, j, kk: (i, j)),
                  pl.BlockSpec((ms, k), lambda i, j, kk: (0, 0)),
                  pl.BlockSpec((ms, n), lambda i, j, kk: (0, 0))],
        out_specs=[pl.BlockSpec((tm, tn), lambda i, j, kk: (i, j)),
                   pl.BlockSpec((ms, n), lambda i, j, kk: (0, 0))],
        out_shape=[jax.ShapeDtypeStruct((m, n), F32), jax.ShapeDtypeStruct((ms, n), F32)],
        compiler_params=_cparams("arbitrary", "arbitrary", "arbitrary"),
        name="linear_acc",
    )(w_all, x, res, x_side, res_side)


def _cumsum_weights():
    row = lax.broadcasted_iota(jnp.int32, (LANES, 2 * LANES), 0)
    col = lax.broadcasted_iota(jnp.int32, (LANES, 2 * LANES), 1)
    return jnp.where((col >= LANES) | (row > col), 1.0, 0.0).astype(BF16)


def _log_sigmoid_pair(z):
    neg_abs = lax.bitcast_convert_type(lax.bitcast_convert_type(z, jnp.uint32) | jnp.uint32(0x80000000), F32)
    log_sig = jnp.minimum(z, 0.0) - jnp.log(1.0 + jnp.exp(neg_abs))
    return log_sig, log_sig - z


def _sb_prompt_kernel(bias_ref, q_ref, k_ref, v_ref, o_ref, acc_scr, carry_scr, *, scale, tq, tk):
    qi = pl.program_id(2)
    hps = acc_scr.shape[0]
    dh = acc_scr.shape[2]
    uo = _cumsum_weights()
    acc_scr[...] = jnp.zeros_like(acc_scr)
    carry_scr[...] = jnp.zeros_like(carry_scr)
    n_diag = tq // tk
    n_blk = tk // LANES

    def tile(ks, masked):
        for hh in range(hps):
            head_tile(hh, ks, masked)

    def head_tile(hh, ks, masked):
        cols = slice(hh * dh, (hh + 1) * dh)
        q = q_ref[:, cols].astype(BF16)
        bias = bias_ref[pl.program_id(1) * hps + hh]
        kb = k_ref[pl.ds(ks, tk), cols].astype(BF16)
        vb = v_ref[pl.ds(ks, tk), cols].astype(BF16)
        z = _dot_nt(q, kb) * scale + bias
        log_sig, log_keep = _log_sigmoid_pair(z)
        if masked:
            q_pos = qi * tq + lax.broadcasted_iota(jnp.int32, (tq, tk), 0)
            k_pos = ks + lax.broadcasted_iota(jnp.int32, (tq, tk), 1)
            mask = k_pos < q_pos
            log_keep = jnp.where(mask, log_keep, 0.0)
        log_keep = log_keep.astype(BF16)
        carry = carry_scr[hh]
        tails = [None] * n_blk
        for c in reversed(range(n_blk)):
            cu = _dot(log_keep[:, c * LANES:(c + 1) * LANES], uo)
            tails[c] = cu[:, :LANES] + carry
            carry = carry + cu[:, LANES:]
        w = jnp.exp(log_sig + jnp.concatenate(tails, axis=1))
        if masked:
            w = jnp.where(mask, w, 0.0)
        acc_scr[hh] += _dot(w.astype(BF16), vb)
        carry_scr[hh] = carry

    for j in range(n_diag):
        tile(pl.multiple_of((qi * n_diag + n_diag - 1 - j) * tk, tk), True)

    def body(it, _):
        tile(pl.multiple_of((qi * n_diag - 1 - it) * tk, tk), False)
        return 0

    lax.fori_loop(0, qi * n_diag, body, 0)
    for hh in range(hps):
        o_ref[:, hh * dh:(hh + 1) * dh] = acc_scr[hh].astype(o_ref.dtype)


def sb_attend_prompt(q, kv, layer, bias):
    b, t, _ = q.shape
    dh = LANES
    n_heads = bias.shape[0]
    hps = SB_HEADS_PER_STEP
    tq = min(t, SB_Q_TILE)
    tk = min(tq, SB_K_TILE)
    kern = functools.partial(_sb_prompt_kernel, scale=dh ** -0.5, tq=tq, tk=tk)
    kv_spec = pl.BlockSpec((None, None, t, hps * dh), lambda bi, h, qi: (layer, bi, 0, h))
    return pl.pallas_call(
        kern,
        grid=(b, n_heads // hps, t // tq),
        in_specs=[pl.BlockSpec(memory_space=pltpu.SMEM),
                  pl.BlockSpec((None, tq, hps * dh), lambda bi, h, qi: (bi, qi, h)),
                  kv_spec, kv_spec],
        out_specs=pl.BlockSpec((None, tq, hps * dh), lambda bi, h, qi: (bi, qi, h)),
        out_shape=jax.ShapeDtypeStruct((b, t, n_heads * dh), BF16),
        scratch_shapes=[pltpu.VMEM((hps, tq, dh), F32), pltpu.VMEM((hps, tq, LANES), F32)],
        compiler_params=_cparams("parallel", "parallel", "arbitrary"),
        name="sb_attend_prompt",
    )(bias, q, kv[0], kv[1])


def _suffix_sum_strided(x, stride):
    w = x.shape[1]
    lane = lax.broadcasted_iota(jnp.int32, x.shape, 1)
    s = stride
    while s < w:
        if s % LANES == 0:
            shifted = jnp.concatenate([x[:, s:], jnp.zeros((x.shape[0], s), F32)], axis=1)
        else:
            shifted = jnp.where(lane < w - s, pltpu.roll(x, w - s, axis=1), 0.0)
        x = x + shifted
        s *= 2
    return x


def _sb_sample_kernel(pt_ref, q_ref, bias_ref, kn_ref, vn_ref, *rest, scale, n_heads, n_new, pages_per_step):
    k_refs = rest[:pages_per_step]
    v_refs = rest[pages_per_step:2 * pages_per_step]
    o_ref, z_scr, w_scr, acc_scr, carry_scr = rest[2 * pages_per_step:]
    s = pl.program_id(1)
    width = kn_ref.shape[0]
    rows = n_new * n_heads
    lane_h = lax.broadcasted_iota(jnp.int32, (n_heads, width), 1) & (n_heads - 1)
    same_head = lane_h == lax.broadcasted_iota(jnp.int32, (n_heads, width), 0)
    row8 = lax.broadcasted_iota(jnp.int32, (SUBLANES, LANES), 0)
    q = q_ref[...].astype(BF16)

    def process(blocks, mask, carry_in):
        r_all = max(SUBLANES, len(blocks) * n_new)
        for p, (k_blk, _) in enumerate(blocks):
            zfull = _dot_nt(q, k_blk.astype(BF16))
            for i in range(n_new):
                part = zfull[i * n_heads:(i + 1) * n_heads, :]
                r = p * n_new + i
                z_scr[r:r + 1, :] = jnp.sum(jnp.where(same_head, part, 0.0), axis=0, keepdims=True)
        z = z_scr[0:r_all, :] * scale + bias_ref[...]
        log_sig, log_keep = _log_sigmoid_pair(z)
        if mask is not None:
            log_keep = jnp.where(mask, log_keep, 0.0)
        incl = _suffix_sum_strided(log_keep, n_heads)
        row128 = lax.broadcasted_iota(jnp.int32, (r_all, LANES), 0)
        lane128 = lax.broadcasted_iota(jnp.int32, (r_all, LANES), 1)
        tot = jnp.where(lane128 < n_heads, incl[:, :LANES], 0.0)
        shift = n_heads
        while shift < LANES:
            tot = tot + pltpu.roll(tot, shift, axis=1)
            shift *= 2
        pre = tot
        shift = n_new
        while shift < r_all:
            pre = pre + jnp.where(row128 >= shift, pltpu.roll(pre, shift, axis=0), 0.0)
            shift *= 2
        carry = jnp.concatenate([carry_in] * (r_all // SUBLANES), axis=0) + (pre - tot)
        w = jnp.exp(log_sig + (incl - log_keep) + jnp.concatenate([carry] * (width // LANES), axis=1))
        if mask is not None:
            w = jnp.where(mask, w, 0.0)
        pv = None
        for p, (_, v_blk) in enumerate(blocks):
            for i in range(n_new):
                r = p * n_new + i
                w_scr[p * rows + i * n_heads:p * rows + (i + 1) * n_heads, :] = jnp.where(
                    same_head, jnp.broadcast_to(w[r:r + 1, :], (n_heads, width)), 0.0).astype(BF16)
            d = _dot(w_scr[p * rows:(p + 1) * rows, :], v_blk.astype(BF16))
            pv = d if pv is None else pv + d
        last = pre[r_all - SUBLANES:, :]
        total = jnp.where(row8 >= n_new, last, pltpu.roll(last, n_new, axis=0))
        return pv, carry_in + total

    @pl.when(s == 0)
    def _():
        z_scr[...] = jnp.zeros_like(z_scr)
        row = lax.broadcasted_iota(jnp.int32, (SUBLANES, width), 0)
        lane = lax.broadcasted_iota(jnp.int32, (SUBLANES, width), 1)
        pv, carry = process([(kn_ref[...], vn_ref[...])], (row < n_new) & (lane < row * n_heads),
                            jnp.zeros(carry_scr.shape, F32))
        acc_scr[...] = pv
        carry_scr[...] = carry

    pv, carry = process([(k_refs[p][...], v_refs[p][...]) for p in range(pages_per_step)], None, carry_scr[...])
    acc_scr[...] += pv
    carry_scr[...] = carry

    @pl.when(s == pl.num_programs(1) - 1)
    def _():
        o_ref[...] = acc_scr[...]


def sb_attend_sample(q, k_new, v_new, cache_k, cache_v, page_table, bias, layer):
    n_seq, n_new, n_heads, dh = q.shape
    assert 2 * n_new == SUBLANES, "two pages of n_new query rows share one 8-sublane array"
    n_pages = page_table.shape[1]
    n_attn, n_pool, page = cache_k.shape[:3]
    width = page * n_heads
    pps = SB_PAGES_PER_STEP
    rows = n_new * n_heads
    pad = ((0, 0), (0, page - n_new), (0, 0), (0, 0))
    k_pad = jnp.pad(k_new, pad).reshape(n_seq, width, dh)
    v_pad = jnp.pad(v_new, pad).reshape(n_seq, width, dh)
    ck = cache_k.reshape(n_attn, n_pool, width, dh)
    cv = cache_v.reshape(n_attn, n_pool, width, dh)
    bias_lanes = jnp.tile(bias.astype(F32), page).reshape(1, width)

    def page_spec(j):
        return pl.BlockSpec((None, None, width, dh),
                            lambda b, s, pt: (layer, pt[b, n_pages - 1 - (s * pps + j)], 0, 0))

    seq_spec = pl.BlockSpec((None, rows, dh), lambda b, s, pt: (b, 0, 0))
    new_spec = pl.BlockSpec((None, width, dh), lambda b, s, pt: (b, 0, 0))
    kern = functools.partial(_sb_sample_kernel, scale=dh ** -0.5, n_heads=n_heads, n_new=n_new,
                             pages_per_step=pps)
    grid_spec = pltpu.PrefetchScalarGridSpec(
        num_scalar_prefetch=1,
        grid=(n_seq, n_pages // pps),
        in_specs=[seq_spec, pl.BlockSpec((1, width), lambda b, s, pt: (0, 0)), new_spec, new_spec]
                 + [page_spec(j) for j in range(pps)] * 2,
        out_specs=seq_spec,
        scratch_shapes=[pltpu.VMEM((pps * n_new, width), F32), pltpu.VMEM((pps * rows, width), BF16),
                        pltpu.VMEM((rows, dh), F32), pltpu.VMEM((SUBLANES, LANES), F32)],
    )
    out = pl.pallas_call(
        kern,
        grid_spec=grid_spec,
        out_shape=jax.ShapeDtypeStruct((n_seq, rows, dh), F32),
        compiler_params=_cparams("parallel", "arbitrary"),
        name="sb_attend_sample",
    )(page_table, q.reshape(n_seq, rows, dh), bias_lanes, k_pad, v_pad, *([ck] * pps), *([cv] * pps))
    return out.reshape(n_seq, n_new, n_heads * dh)


def _ml_gates_kernel(u_ref, wg_ref, b_ref, o_ref, *, n_heads):
    pre = _dot_nt(wg_ref[...].astype(BF16), u_ref[...]) + b_ref[...]
    pre = GATE_SOFTCAP * jnp.tanh(pre / GATE_SOFTCAP)
    o_ref[0:n_heads, :] = pre[0:n_heads]
    o_ref[n_heads:, :] = -_softplus(-pre[n_heads:])


def ml_gates(u, wg_t, b_gate):
    m, d = u.shape
    g = wg_t.shape[0]
    tm = min(m, ROW_TILE)
    return pl.pallas_call(
        functools.partial(_ml_gates_kernel, n_heads=g // 2),
        grid=(m // tm,),
        in_specs=[pl.BlockSpec((tm, d), lambda i: (i, 0)),
                  pl.BlockSpec((g, d), lambda i: (0, 0)),
                  pl.BlockSpec((g, 1), lambda i: (0, 0))],
        out_specs=pl.BlockSpec((g, tm), lambda i: (0, i)),
        out_shape=jax.ShapeDtypeStruct((g, m), F32),
        compiler_params=_cparams("parallel"),
        name="ml_gates",
    )(u, wg_t, b_gate)


def mlstm_gate_layout(g, b, t, n_heads):
    hps = ML_HEADS_PER_STEP
    t_pad = -(-t // CHUNK) * CHUNK
    g = jnp.pad(g.reshape(2, n_heads // hps, hps, b, t), ((0, 0),) * 4 + ((0, t_pad - t),))
    g = g.reshape(2, n_heads // hps, hps, b, t_pad // CHUNK, CHUNK).transpose(3, 4, 1, 2, 0, 5)
    return g.reshape(b, t_pad // CHUNK, n_heads // hps, 2 * hps, CHUNK)


def _lane_cumsum(x):
    lane = lax.broadcasted_iota(jnp.int32, x.shape, 1)
    shift = 1
    while shift < x.shape[1]:
        x = x + jnp.where(lane >= shift, pltpu.roll(x, shift, axis=1), 0.0)
        shift *= 2
    return x


def _mlstm_kernel(g_ref, q_ref, k_ref, v_ref, og_ref, gn_ref, c0_ref, n0_ref, m0_ref,
                  y_ref, c_out, n_out, m_out, c_scr, n_scr, m_scr, *, n_valid, k_scale):
    ci = pl.program_id(2)
    L = g_ref.shape[1]
    t_in = q_ref.shape[0]
    hps = c_scr.shape[0]
    dk = c_scr.shape[1]
    dv = c_scr.shape[2]

    @pl.when(ci == 0)
    def _():
        c_scr[...] = c0_ref[...]
        n_scr[...] = n0_ref[...]
        m_scr[...] = m0_ref[...]

    g = g_ref[...]
    row8 = lax.broadcasted_iota(jnp.int32, (2 * hps, L), 0)
    if n_valid < L:
        valid = lax.broadcasted_iota(jnp.int32, (2 * hps, L), 1) < n_valid
        g = jnp.where(valid, g, jnp.where((row8 & 1) == 0, NEG_BIG, 0.0))
    bc = _lane_cumsum(g)
    rows = jnp.concatenate([jnp.where((row8 & 1) == 0, g, bc), jnp.zeros((L - 2 * hps, L), F32)], axis=0)
    cols = rows.T
    t_idx = lax.broadcasted_iota(jnp.int32, (L, L), 0)
    s_idx = lax.broadcasted_iota(jnp.int32, (L, L), 1)
    causal = s_idx <= t_idx

    for hh in range(hps):
        li_r = g[2 * hh:2 * hh + 1, :]
        bc_r = bc[2 * hh + 1:2 * hh + 2, :]
        li_c = cols[:, 2 * hh:2 * hh + 1]
        bc_c = cols[:, 2 * hh + 1:2 * hh + 2]
        m_prev = m_scr[hh, 0:1, 0:1]
        dmat = jnp.where(causal, bc_c + (li_r - bc_r), NEG_BIG)
        inter = bc_c + m_prev
        m_t = jnp.maximum(inter, jnp.max(dmat, axis=1, keepdims=True))

        qf = _pad_rows(q_ref[:, hh * dk:(hh + 1) * dk], L)
        q = qf.astype(BF16)
        ks_f = _pad_rows(k_ref[:, hh * dk:(hh + 1) * dk], L) * k_scale
        v = _pad_rows(v_ref[:, hh * dv:(hh + 1) * dv], L).astype(BF16)
        a = jnp.exp(dmat - m_t) * _dot_nt(q, ks_f.astype(BF16))
        sc = jnp.exp(inter - m_t)
        c = c_scr[hh]
        n_row = n_scr[hh]
        num = _dot(a.astype(BF16), v) + sc * _dot(q, c.astype(BF16))
        den = jnp.sum(a, axis=1, keepdims=True) + sc * jnp.sum(qf * n_row, axis=1, keepdims=True)
        hval = num / jnp.maximum(jnp.abs(den), jnp.exp(-m_t))
        gate = jax.nn.sigmoid(og_ref[:, hh * dv:(hh + 1) * dv])
        y_ref[:, hh * dv:(hh + 1) * dv] = (gate * _rms(hval[:t_in], gn_ref[...])).astype(y_ref.dtype)

        m_new = m_t[L - 1:L, :]
        bc_last = bc_c[L - 1:L, :]
        wl = jnp.exp(bc_last - bc_c + li_c - m_new)
        sl = jnp.exp(bc_last + m_prev - m_new)
        kw = ks_f * wl
        c_scr[hh] = sl * c + _dot(kw.T.astype(BF16), v)
        n_scr[hh] = sl * n_row + jnp.sum(kw, axis=0, keepdims=True)
        m_scr[hh] = jnp.broadcast_to(m_new, (1, LANES))

    @pl.when(ci == pl.num_programs(2) - 1)
    def _():
        c_out[...] = c_scr[...]
        n_out[...] = n_scr[...]
        m_out[...] = m_scr[...]


def mlstm_scan(proj, gates, out_gain, c0, n0, m0, n_heads, n_valid):
    b, t, _ = proj.shape
    L = min(t, CHUNK)
    dk = c0.shape[2]
    dv = c0.shape[3]
    hps = ML_HEADS_PER_STEP
    n_grp = n_heads // hps
    v_blk = 2 * n_heads * dk // (hps * dv)
    kern = functools.partial(_mlstm_kernel, n_valid=n_valid, k_scale=dk ** -0.5)
    state_specs = [pl.BlockSpec((None, hps, dk, dv), lambda bi, h, ci: (bi, h, 0, 0)),
                   pl.BlockSpec((None, hps, 1, dk), lambda bi, h, ci: (bi, h, 0, 0)),
                   pl.BlockSpec((None, hps, 1, LANES), lambda bi, h, ci: (bi, h, 0, 0))]
    return pl.pallas_call(
        kern,
        grid=(b, n_grp, t // L),
        in_specs=[pl.BlockSpec((None, None, None, 2 * hps, CHUNK), lambda bi, h, ci: (bi, ci, h, 0, 0)),
                  pl.BlockSpec((None, L, hps * dk), lambda bi, h, ci: (bi, ci, h)),
                  pl.BlockSpec((None, L, hps * dk), lambda bi, h, ci: (bi, ci, n_grp + h)),
                  pl.BlockSpec((None, L, hps * dv), lambda bi, h, ci: (bi, ci, v_blk + h)),
                  pl.BlockSpec((None, L, hps * dv), lambda bi, h, ci: (bi, ci, v_blk + n_grp + h)),
                  pl.BlockSpec((1, dv), lambda bi, h, ci: (0, 0))] + state_specs,
        out_specs=[pl.BlockSpec((None, L, hps * dv), lambda bi, h, ci: (bi, ci, h))] + state_specs,
        out_shape=[jax.ShapeDtypeStruct((b, t, n_heads * dv), BF16),
                   jax.ShapeDtypeStruct(c0.shape, F32),
                   jax.ShapeDtypeStruct(n0.shape, F32),
                   jax.ShapeDtypeStruct(m0.shape, F32)],
        scratch_shapes=[pltpu.VMEM((hps, dk, dv), F32), pltpu.VMEM((hps, 1, dk), F32),
                        pltpu.VMEM((hps, 1, LANES), F32)],
        compiler_params=_cparams("parallel", "parallel", "arbitrary"),
        name="mlstm_scan",
    )(gates, proj, proj, proj, proj, out_gain, c0, n0, m0)


def _hgrn_tables(c):
    n_lev = int(np.log2(c))
    lev = np.zeros(((n_lev + 1) * c, c), np.float32)
    lev[:c] = np.tril(np.ones((c, c), np.float32))
    for l in range(1, n_lev + 1):
        m = 1 << (l - 1)
        for t in range(c):
            mid = (t >> l << l) + m
            if t & m:
                lev[l * c + t, mid:t + 1] = 1.0
            else:
                lev[l * c + t, t + 1:mid] = 1.0
    t = np.arange(c)[:, None]
    s = np.arange(c)[None, :]
    x = t ^ s
    pair_level = np.where(s > t, -1, np.where(s == t, 0, np.floor(np.log2(np.maximum(x, 1))).astype(np.int32) + 1))
    return jnp.asarray(np.concatenate([lev, lev], axis=1), BF16), jnp.asarray(pair_level, jnp.int32)


def _hgrn_kernel(lev_ref, lvl_ref, q_ref, f_ref, i_ref, g_ref, lb_ref, gn_ref, s0_ref,
                 y_ref, s_out, st_scr, *, n_valid):
    ci = pl.program_id(2)
    c = lvl_ref.shape[0]
    t_in = q_ref.shape[0]
    n_lev = lev_ref.shape[0] // c - 1
    hps = st_scr.shape[0]
    dh = st_scr.shape[1]

    @pl.when(ci == 0)
    def _():
        for hh in range(hps):
            st_scr[hh] = s0_ref[hh].T

    fp = _pad_rows(f_ref[...], c)
    lb = lb_ref[...]
    log_sig = -_softplus(-fp)
    la = jnp.log(lb)
    lbb = jnp.log1p(-lb) + log_sig
    lf = jnp.maximum(la, lbb) + jnp.log(1.0 + jnp.exp(-jnp.abs(la - lbb)))
    kk = (1.0 - lb) * jax.nn.sigmoid(-fp)
    if n_valid < c:
        valid = lax.broadcasted_iota(jnp.int32, (c, 1), 0) < n_valid
        lf = jnp.where(valid, lf, 0.0)
        kk = jnp.where(valid, kk, 0.0)

    hi, lo = _split_bf16(lf)
    d_all = _dot(lev_ref[...], jnp.concatenate([hi, lo], axis=0))
    bcum = d_all[0:c]
    q = _pad_rows(q_ref[...], c)
    iv_f = _pad_rows(i_ref[...], c)
    lvl = lvl_ref[...]
    heads = [slice(hh * dh, (hh + 1) * dh) for hh in range(hps)]
    qb = q.astype(BF16)
    kb = kk.astype(BF16)
    a = [jnp.where(lvl == 0, _dot_nt(qb[:, sl], kb[:, sl]), 0.0) for sl in heads]
    for l in range(1, n_lev + 1):
        e = jnp.exp(d_all[l * c:(l + 1) * c])
        qe = (q * e).astype(BF16)
        ke = (kk * e).astype(BF16)
        a = [a[hh] + jnp.where(lvl == l, _dot_nt(qe[:, sl], ke[:, sl]), 0.0) for hh, sl in enumerate(heads)]
    blast = bcum[c - 1:c, :]
    q_in = (q * jnp.exp(bcum)).astype(BF16)
    k_out = (kk * jnp.exp(blast - bcum)).astype(BF16)
    decay = jnp.exp(blast)
    gate = jax.nn.sigmoid(g_ref[...])
    for hh, sl in enumerate(heads):
        st = st_scr[hh]
        o = _dot(a[hh].astype(BF16), iv_f[:, sl].astype(BF16)) + _dot_nt(q_in[:, sl], st.astype(BF16))
        y_ref[:, sl] = (gate[:, sl] * _rms(o[:t_in], gn_ref[...])).astype(y_ref.dtype)
        st_scr[hh] = decay[:, sl] * st + _dot(iv_f[:, sl].T.astype(BF16), k_out[:, sl])

    @pl.when(ci == pl.num_programs(2) - 1)
    def _():
        for hh in range(hps):
            s_out[hh] = st_scr[hh].T


def hgrn_scan(proj, lower_bound, out_gain, s0, n_heads, n_valid):
    b, t, _ = proj.shape
    c = min(t, CHUNK)
    dh = s0.shape[2]
    hps = HG_HEADS_PER_STEP
    n_grp = n_heads // hps
    lev, pair_level = _hgrn_tables(CHUNK)
    kern = functools.partial(_hgrn_kernel, n_valid=n_valid)

    def part(p):
        return pl.BlockSpec((None, c, hps * dh), lambda bi, h, ci: (bi, ci, p * n_grp + h))

    state_spec = pl.BlockSpec((None, hps, dh, dh), lambda bi, h, ci: (bi, h, 0, 0))
    return pl.pallas_call(
        kern,
        grid=(b, n_grp, t // c),
        in_specs=[pl.BlockSpec(lev.shape, lambda bi, h, ci: (0, 0)),
                  pl.BlockSpec(pair_level.shape, lambda bi, h, ci: (0, 0)),
                  part(0), part(1), part(2), part(3),
                  pl.BlockSpec((1, hps * dh), lambda bi, h, ci: (0, h)),
                  pl.BlockSpec((1, dh), lambda bi, h, ci: (0, 0)),
                  state_spec],
        out_specs=[pl.BlockSpec((None, c, hps * dh), lambda bi, h, ci: (bi, ci, h)), state_spec],
        out_shape=[jax.ShapeDtypeStruct((b, t, n_heads * dh), BF16),
                   jax.ShapeDtypeStruct(s0.shape, F32)],
        scratch_shapes=[pltpu.VMEM((hps, dh, dh), F32)],
        compiler_params=_cparams("parallel", "parallel", "arbitrary"),
        name="hgrn_scan",
    )(lev, pair_level, proj, proj, proj, proj, lower_bound, out_gain, s0)


def _pad_tokens(x, t_pad):
    return jnp.pad(x, ((0, 0), (0, t_pad - x.shape[1])) + ((0, 0),) * (x.ndim - 2))


def kernel(x_prompt, x_sample, cache_k, cache_v, state_mlstm_c, state_mlstm_n, state_mlstm_m,
           state_hgrn, page_table, p_prompt, p_sample, norm_mix, norm_ffn, norm_ple,
           sb_w_qkv, sb_q_norm, sb_k_norm, sb_logit_bias, sb_w_out, ml_w_in, ml_b_gate,
           ml_out_norm, ml_w_out, hg_w_in, hg_lb_logits, hg_out_norm, hg_w_out,
           ffn_w_up, ffn_w_down, ple_w_proj, ple_w_gate):
    depth, d = norm_mix.shape
    bp, tp, _ = x_prompt.shape
    bs, ts, _ = x_sample.shape
    sb_heads = sb_logit_bias.shape[1]
    ml_heads = state_mlstm_c.shape[2]
    ml_dk = state_mlstm_c.shape[3]
    ml_dv = state_mlstm_c.shape[4]
    hg_heads = state_hgrn.shape[2]
    hg_dh = state_hgrn.shape[3]
    n_attn, n_pool, page = cache_k.shape[:3]
    ml_qk_w = ml_heads * ml_dk

    g_mix = norm_mix.reshape(depth, 1, d)
    g_ffn = norm_ffn.reshape(depth, 1, d)
    g_ple = norm_ple.reshape(depth, 1, d)
    lb_soft = jax.nn.softmax(hg_lb_logits.astype(F32), axis=0)
    lower_bounds = jnp.cumsum(lb_soft, axis=0) - lb_soft

    hp = x_prompt.reshape(bp * tp, d)
    hs = x_sample.reshape(bs * ts, d)
    pp_all = p_prompt.reshape(depth, bp * tp, -1)
    ps_all = p_sample.reshape(depth, bs * ts, -1)

    outs = {k: [] for k in ("ks", "vs", "mcp", "mnp", "mmp", "mcs", "mns", "mms", "hgp", "hgs")}
    kp_all = vp_all = None
    i_attn = i_ml = i_hg = 0
    for layer in range(depth):
        kind = layer % N_MIXERS
        up = rmsnorm_bf16(hp, g_mix, layer)
        us = rmsnorm_bf16(hs, g_mix, layer)
        if kind == 0:
            j = i_attn
            i_attn += 1
            gq = sb_q_norm[j].reshape(1, LANES)
            gk = sb_k_norm[j].reshape(1, LANES)
            q_p, q_s = linear("headnorm", up, sb_w_qkv, j, 0, d, shared=(gq,), side=(us, ()))
            kp_all, k_s = linear("headnorm", up, sb_w_qkv, j, d, d, shared=(gk,), side=(us, ()),
                                 slab=(n_attn, j, kp_all))
            vp_all, v_s = linear("plain", up, sb_w_qkv, j, 2 * d, d, side=(us, ()), slab=(n_attn, j, vp_all))
            q_s, k_s, v_s = (a.reshape(bs, ts, sb_heads, LANES) for a in (q_s, k_s, v_s))
            outs["ks"].append(k_s)
            outs["vs"].append(v_s)
            kv = (kp_all.reshape(n_attn, bp, tp, d), vp_all.reshape(n_attn, bp, tp, d))
            ap = sb_attend_prompt(q_p.reshape(bp, tp, d), kv, j, sb_logit_bias[j]).reshape(bp * tp, d)
            a_s = sb_attend_sample(q_s, k_s, v_s, cache_k, cache_v, page_table,
                                   sb_logit_bias[j], j).reshape(bs * ts, d)
            hp, hs = linear("residual", ap, sb_w_out, j, 0, d, per=(hp,), side=(a_s, (hs,)))
        elif kind == 1:
            j = i_ml
            i_ml += 1
            wg_t = ml_w_in[j][:, 2 * ml_qk_w + 2 * d:].T
            b_gate = ml_b_gate[j].reshape(-1, 1)
            gn = ml_out_norm[j].reshape(1, ml_dv)
            n_main = 2 * ml_qk_w + 2 * d
            proj_p, proj_s = linear("plain", up, ml_w_in, j, 0, n_main, side=(us, ()))
            proj_p = proj_p.reshape(bp, tp, n_main)
            proj_s = proj_s.reshape(bs, ts, n_main)
            gates_p = mlstm_gate_layout(ml_gates(up, wg_t, b_gate), bp, tp, ml_heads)
            gates_s = mlstm_gate_layout(ml_gates(us, wg_t, b_gate), bs, ts, ml_heads)
            zc = jnp.zeros((bp, ml_heads, ml_dk, ml_dv), F32)
            zn = jnp.zeros((bp, ml_heads, 1, ml_dk), F32)
            zm = jnp.zeros((bp, ml_heads, 1, LANES), F32)
            yp, cp, npv, mp = mlstm_scan(proj_p, gates_p, gn, zc, zn, zm, ml_heads, CHUNK)
            m0 = jnp.broadcast_to(state_mlstm_m[j][:, :, None, None], (bs, ml_heads, 1, LANES))
            ys, cs, nsv, ms = mlstm_scan(_pad_tokens(proj_s, SUBLANES), gates_s, gn, state_mlstm_c[j],
                                         state_mlstm_n[j][:, :, None, :], m0, ml_heads, ts)
            outs["mcp"].append(cp)
            outs["mnp"].append(npv[:, :, 0, :])
            outs["mmp"].append(mp[:, :, 0, 0])
            outs["mcs"].append(cs)
            outs["mns"].append(nsv[:, :, 0, :])
            outs["mms"].append(ms[:, :, 0, 0])
            hp, hs = linear("residual", yp.reshape(bp * tp, d), ml_w_out, j, 0, d, per=(hp,),
                            side=(ys[:, :ts].reshape(bs * ts, d), (hs,)))
        else:
            j = i_hg
            i_hg += 1
            lb = lower_bounds[layer].reshape(1, d)
            gn = hg_out_norm[j].reshape(1, hg_dh)
            proj_p, proj_s = linear("plain", up, hg_w_in, j, 0, 4 * d, side=(us, ()))
            proj_p = proj_p.reshape(bp, tp, 4 * d)
            proj_s = proj_s.reshape(bs, ts, 4 * d)
            zs = jnp.zeros((bp, hg_heads, hg_dh, hg_dh), F32)
            yp, sp = hgrn_scan(proj_p, lb, gn, zs, hg_heads, CHUNK)
            ys, ss = hgrn_scan(_pad_tokens(proj_s, SUBLANES), lb, gn, state_hgrn[j], hg_heads, ts)
            outs["hgp"].append(sp)
            outs["hgs"].append(ss)
            hp, hs = linear("residual", yp.reshape(bp * tp, d), hg_w_out, j, 0, d, per=(hp,),
                            side=(ys[:, :ts].reshape(bs * ts, d), (hs,)))
        a_p, a_s = linear("relu2", rmsnorm_bf16(hp, g_ffn, layer), ffn_w_up, layer, 0, ffn_w_up.shape[2],
                          side=(rmsnorm_bf16(hs, g_ffn, layer), ()), out_dtype=BF16, col_tile=FFN_COL_TILE)
        hp, hs = linear_residual_acc(a_p, ffn_w_down, layer, hp, a_s, hs)
        hp, hs = linear("ple", hp, ple_w_gate, layer, 0, d, shared=(ple_w_proj,), per=(pp_all, hp),
                        side=(hs, (ps_all, hs)), col_tile=PLE_COL_TILE, norm=(g_ple, layer))

    st = {k: jnp.stack(v) for k, v in outs.items()}
    kv_shape = (n_attn, bp, tp, sb_heads, LANES)
    return (hp.reshape(bp, tp, d), hs.reshape(bs, ts, d), kp_all.reshape(kv_shape), vp_all.reshape(kv_shape),
            st["ks"], st["vs"],
            st["mcp"], st["mnp"], st["mmp"], st["mcs"], st["mns"], st["mms"], st["hgp"], st["hgs"])
```

```python
import functools

import numpy as np
import jax
import jax.numpy as jnp
from jax import lax
from jax.experimental import pallas as pl
from jax.experimental.pallas import tpu as pltpu

F32 = jnp.float32
BF16 = jnp.bfloat16

NORM_EPS = 1e-6
N_MIXERS = 3
GATE_SOFTCAP = 15.0
LANES = 128
SUBLANES = 8
VMEM_LIMIT_BYTES = 56 * 1024 * 1024
ROW_TILE = 2048
COL_TILE = 512
PLE_COL_TILE = 256
FFN_COL_TILE = 1024
ACC_ROW_TILE = 2048
ACC_COL_TILE = 512
ACC_K_TILE = 2048
SB_Q_TILE = 512
SB_K_TILE = 512
SB_HEADS_PER_STEP = 8
SB_PAGES_PER_STEP = 8
CHUNK = 128
ML_HEADS_PER_STEP = 8
HG_HEADS_PER_STEP = 16
NEG_BIG = -1e30


def _cparams(*sem):
    return pltpu.CompilerParams(dimension_semantics=sem, vmem_limit_bytes=VMEM_LIMIT_BYTES)


def _dot(a, b):
    return jnp.dot(a, b, preferred_element_type=F32)


def _dot_nt(a, b):
    return lax.dot_general(a, b, (((1,), (1,)), ((), ())), preferred_element_type=F32)


def _split_bf16(x):
    hi = x.astype(BF16)
    lo = (x - hi.astype(F32)).astype(BF16)
    return hi, lo


def _softplus(z):
    return jnp.maximum(z, 0.0) + jnp.log(1.0 + jnp.exp(-jnp.abs(z)))


def _pad_rows(x, rows):
    if x.shape[0] == rows:
        return x
    return jnp.concatenate([x, jnp.zeros((rows - x.shape[0], x.shape[1]), x.dtype)], axis=0)


def _rms(x, gain):
    ms = jnp.mean(x * x, axis=-1, keepdims=True)
    return x * lax.rsqrt(ms + NORM_EPS) * gain


def _rmsnorm_kernel(x_ref, g_ref, o_ref):
    o_ref[...] = _rms(x_ref[...], g_ref[...]).astype(o_ref.dtype)


def rmsnorm_bf16(x, gains, layer):
    m, d = x.shape
    tm = min(m, 512)
    return pl.pallas_call(
        _rmsnorm_kernel,
        grid=(m // tm,),
        in_specs=[pl.BlockSpec((tm, d), lambda i: (i, 0)),
                  pl.BlockSpec((None, 1, d), lambda i: (layer, 0, 0))],
        out_specs=pl.BlockSpec((tm, d), lambda i: (i, 0)),
        out_shape=jax.ShapeDtypeStruct((m, d), BF16),
        compiler_params=_cparams("parallel"),
        name="rmsnorm",
    )(x, gains)


_LINEAR_INPUTS = {"plain": (0, 0), "headnorm": (1, 0), "residual": (0, 1), "relu2": (0, 0), "ple": (1, 2)}


def _linear_epilogue(mode, acc, shared, wp, per, cols, dtype):
    if mode == "plain":
        return acc
    if mode == "headnorm":
        return jnp.concatenate([_rms(acc[:, g * LANES:(g + 1) * LANES], shared[0][...])
                                for g in range(acc.shape[1] // LANES)], axis=1)
    if mode == "residual":
        return per[0][:, cols] + acc
    if mode == "relu2":
        a = jnp.maximum(acc, 0.0)
        return (a * a).astype(dtype)
    return per[1][:, cols] + jax.nn.sigmoid(acc) * _dot(per[0][...].astype(BF16), wp)


def _linear_kernel(*refs, mode, has_side, has_norm, first_slab=None):
    n_shared, n_per = _LINEAR_INPUTS[mode]
    n_groups = 2 if has_side else 1
    n_scratch = 1 if has_norm else 0
    w = refs[0][...].astype(BF16)
    shared = refs[1:1 + n_shared]
    pos = 1 + n_shared
    if has_norm:
        gain_ref = refs[pos]
        pos += 1
    groups = [refs[pos + g * (1 + n_per):pos + (g + 1) * (1 + n_per)] for g in range(n_groups)]
    outs = refs[len(refs) - n_groups - n_scratch:len(refs) - n_scratch]
    wp = shared[0][...].astype(BF16) if mode == "ple" else None
    tn = w.shape[1]

    x_ref, *per = groups[0]
    if has_norm:
        u_scr = refs[-1]

        @pl.when(pl.program_id(1) == 0)
        def _():
            u_scr[...] = _rms(x_ref[...], gain_ref[...]).astype(BF16)

        xv = u_scr[...]
    else:
        xv = x_ref[...].astype(BF16)
    acc = _dot(xv, w)
    res = _linear_epilogue(mode, acc, shared, wp, per, slice(None), outs[0].dtype)
    if first_slab is None:
        outs[0][...] = res
    else:
        for s in range(outs[0].shape[0]):
            outs[0][s] = res if s == first_slab else jnp.zeros_like(res)

    if has_side:
        @pl.when(pl.program_id(0) == 0)
        def _():
            xs_ref, *per_s = groups[1]
            cols = pl.ds(pl.multiple_of(pl.program_id(1) * tn, tn), tn)
            xs = _rms(xs_ref[...], gain_ref[...]) if has_norm else xs_ref[...]
            acc_s = _dot(xs.astype(BF16), w)
            outs[1][:, cols] = _linear_epilogue(mode, acc_s, shared, wp, per_s, cols, outs[1].dtype)


def linear(mode, x, w_all, layer, col_start, n_cols, shared=(), per=(), side=None, out_dtype=F32,
           col_tile=COL_TILE, slab=None, norm=None):
    m, k = x.shape
    tn = col_tile
    off = col_start // tn
    tm = min(m, ROW_TILE)
    in_specs = [pl.BlockSpec((None, k, tn), lambda i, j: (layer, 0, off + j))]
    if mode == "headnorm":
        in_specs.append(pl.BlockSpec((1, LANES), lambda i, j: (0, 0)))
    elif mode == "ple":
        in_specs.append(pl.BlockSpec((None, shared[0].shape[1], tn), lambda i, j: (layer, 0, j)))
    args = [w_all, *shared]
    scratch = []
    if norm is None:
        in_specs.append(pl.BlockSpec((tm, k), lambda i, j: (i, 0)))
    else:
        gains, g_idx = norm
        args.append(gains)
        in_specs.append(pl.BlockSpec((None, 1, k), lambda i, j: (g_idx, 0, 0)))
        in_specs.append(pl.BlockSpec((tm, k), lambda i, j: (i, 0), pipeline_mode=pl.Buffered(1)))
        scratch.append(pltpu.VMEM((tm, k), BF16))
    args += [x, *per]
    if mode == "residual":
        in_specs.append(pl.BlockSpec((tm, tn), lambda i, j: (i, j)))
    elif mode == "ple":
        in_specs.append(pl.BlockSpec((None, tm, per[0].shape[2]), lambda i, j: (layer, i, 0)))
        in_specs.append(pl.BlockSpec((tm, tn), lambda i, j: (i, j)))
    out_specs = [pl.BlockSpec((tm, tn), lambda i, j: (i, j))]
    out_shape = [jax.ShapeDtypeStruct((m, n_cols), out_dtype)]
    if side is not None:
        x_s, per_s = side
        ms = x_s.shape[0]
        args += [x_s, *per_s]
        in_specs.append(pl.BlockSpec((ms, k), lambda i, j: (0, 0)))
        if mode == "residual":
            in_specs.append(pl.BlockSpec((ms, n_cols), lambda i, j: (0, 0)))
        elif mode == "ple":
            in_specs.append(pl.BlockSpec((None, ms, per_s[0].shape[2]), lambda i, j: (layer, 0, 0)))
            in_specs.append(pl.BlockSpec((ms, n_cols), lambda i, j: (0, 0)))
        out_specs.append(pl.BlockSpec((ms, n_cols), lambda i, j: (0, 0)))
        out_shape.append(jax.ShapeDtypeStruct((ms, n_cols), out_dtype))
    aliases = {}
    first_slab = None
    if slab is not None:
        n_slabs, index, prev = slab
        out_shape[0] = jax.ShapeDtypeStruct((n_slabs, m, n_cols), out_dtype)
        if prev is None:
            first_slab = index
            out_specs[0] = pl.BlockSpec((n_slabs, tm, tn), lambda i, j: (0, i, j))
        else:
            out_specs[0] = pl.BlockSpec((None, tm, tn), lambda i, j: (index, i, j))
            aliases = {len(args): 0}
            args.append(prev)
            in_specs.append(pl.BlockSpec(memory_space=pl.ANY))
    outs = pl.pallas_call(
        functools.partial(_linear_kernel, mode=mode, has_side=side is not None, has_norm=norm is not None,
                          first_slab=first_slab),
        grid=(m // tm, n_cols // tn),
        in_specs=in_specs,
        out_specs=out_specs,
        out_shape=out_shape,
        scratch_shapes=scratch,
        input_output_aliases=aliases,
        compiler_params=_cparams("arbitrary", "arbitrary"),
        name="linear_" + mode,
    )(*args)
    return outs[0] if side is None else outs


def _linear_acc_kernel(w_ref, x_ref, r_ref, xs_ref, rs_ref, o_ref, os_ref):
    kk = pl.program_id(2)
    w = w_ref[...].astype(BF16)
    tk, tn = w.shape

    @pl.when(kk == 0)
    def _():
        o_ref[...] = r_ref[...]

    o_ref[...] += _dot(x_ref[...], w)

    @pl.when(pl.program_id(0) == 0)
    def _():
        cols = pl.ds(pl.multiple_of(pl.program_id(1) * tn, tn), tn)

        @pl.when(kk == 0)
        def _():
            os_ref[:, cols] = rs_ref[:, cols]

        os_ref[:, cols] += _dot(xs_ref[:, pl.ds(pl.multiple_of(kk * tk, tk), tk)], w)


def linear_residual_acc(x, w_all, layer, res, x_side, res_side):
    m, k = x.shape
    n = res.shape[1]
    ms = x_side.shape[0]
    tm = min(m, ACC_ROW_TILE)
    tn = ACC_COL_TILE
    tk = ACC_K_TILE
    return pl.pallas_call(
        _linear_acc_kernel,
        grid=(m // tm, n // tn, k // tk),
        in_specs=[pl.BlockSpec((None, tk, tn), lambda i, j, kk: (layer, kk, j)),
                  pl.BlockSpec((tm, tk), lambda i, j, kk: (i, kk)),
                  pl.BlockSpec((tm, tn), lambda i, j, kk: (i, j)),
                  pl.BlockSpec((ms, k), lambda i, j, kk: (0, 0)),
                  pl.BlockSpec((ms, n), lambda i, j, kk: (0, 0))],
        out_specs=[pl.BlockSpec((tm, tn), lambda i, j, kk: (i, j)),
                   pl.BlockSpec((ms, n), lambda i, j, kk: (0, 0))],
        out_shape=[jax.ShapeDtypeStruct((m, n), F32), jax.ShapeDtypeStruct((ms, n), F32)],
        compiler_params=_cparams("arbitrary", "arbitrary", "arbitrary"),
        name="linear_acc",
    )(w_all, x, res, x_side, res_side)


def _cumsum_weights():
    row = lax.broadcasted_iota(jnp.int32, (LANES, 2 * LANES), 0)
    col = lax.broadcasted_iota(jnp.int32, (LANES, 2 * LANES), 1)
    return jnp.where((col >= LANES) | (row > col), 1.0, 0.0).astype(BF16)


def _log_sigmoid_pair(z):
    neg_abs = lax.bitcast_convert_type(lax.bitcast_convert_type(z, jnp.uint32) | jnp.uint32(0x80000000), F32)
    log_sig = jnp.minimum(z, 0.0) - jnp.log(1.0 + jnp.exp(neg_abs))
    return log_sig, log_sig - z


def _sb_prompt_kernel(bias_ref, q_ref, k_ref, v_ref, o_ref, acc_scr, carry_scr, *, scale, tq, tk):
    qi = pl.program_id(2)
    hps = acc_scr.shape[0]
    dh = acc_scr.shape[2]
    uo = _cumsum_weights()
    acc_scr[...] = jnp.zeros_like(acc_scr)
    carry_scr[...] = jnp.zeros_like(carry_scr)
    n_diag = tq // tk
    n_blk = tk // LANES

    def tile(ks, masked):
        for hh in range(hps):
            head_tile(hh, ks, masked)

    def head_tile(hh, ks, masked):
        cols = slice(hh * dh, (hh + 1) * dh)
        q = q_ref[:, cols].astype(BF16)
        bias = bias_ref[pl.program_id(1) * hps + hh]
        kb = k_ref[pl.ds(ks, tk), cols].astype(BF16)
        vb = v_ref[pl.ds(ks, tk), cols].astype(BF16)
        z = _dot_nt(q, kb) * scale + bias
        log_sig, log_keep = _log_sigmoid_pair(z)
        if masked:
            q_pos = qi * tq + lax.broadcasted_iota(jnp.int32, (tq, tk), 0)
            k_pos = ks + lax.broadcasted_iota(jnp.int32, (tq, tk), 1)
            mask = k_pos < q_pos
            log_keep = jnp.where(mask, log_keep, 0.0)
        log_keep = log_keep.astype(BF16)
        carry = carry_scr[hh]
        tails = [None] * n_blk
        for c in reversed(range(n_blk)):
            cu = _dot(log_keep[:, c * LANES:(c + 1) * LANES], uo)
            tails[c] = cu[:, :LANES] + carry
            carry = carry + cu[:, LANES:]
        w = jnp.exp(log_sig + jnp.concatenate(tails, axis=1))
        if masked:
            w = jnp.where(mask, w, 0.0)
        acc_scr[hh] += _dot(w.astype(BF16), vb)
        carry_scr[hh] = carry

    for j in range(n_diag):
        tile(pl.multiple_of((qi * n_diag + n_diag - 1 - j) * tk, tk), True)

    def body(it, _):
        tile(pl.multiple_of((qi * n_diag - 1 - it) * tk, tk), False)
        return 0

    lax.fori_loop(0, qi * n_diag, body, 0)
    for hh in range(hps):
        o_ref[:, hh * dh:(hh + 1) * dh] = acc_scr[hh].astype(o_ref.dtype)


def sb_attend_prompt(q, kv, layer, bias):
    b, t, _ = q.shape
    dh = LANES
    n_heads = bias.shape[0]
    hps = SB_HEADS_PER_STEP
    tq = min(t, SB_Q_TILE)
    tk = min(tq, SB_K_TILE)
    kern = functools.partial(_sb_prompt_kernel, scale=dh ** -0.5, tq=tq, tk=tk)
    kv_spec = pl.BlockSpec((None, None, t, hps * dh), lambda bi, h, qi: (layer, bi, 0, h))
    return pl.pallas_call(
        kern,
        grid=(b, n_heads // hps, t // tq),
        in_specs=[pl.BlockSpec(memory_space=pltpu.SMEM),
                  pl.BlockSpec((None, tq, hps * dh), lambda bi, h, qi: (bi, qi, h)),
                  kv_spec, kv_spec],
        out_specs=pl.BlockSpec((None, tq, hps * dh), lambda bi, h, qi: (bi, qi, h)),
        out_shape=jax.ShapeDtypeStruct((b, t, n_heads * dh), BF16),
        scratch_shapes=[pltpu.VMEM((hps, tq, dh), F32), pltpu.VMEM((hps, tq, LANES), F32)],
        compiler_params=_cparams("parallel", "parallel", "arbitrary"),
        name="sb_attend_prompt",
    )(bias, q, kv[0], kv[1])


def _suffix_sum_strided(x, stride):
    w = x.shape[1]
    lane = lax.broadcasted_iota(jnp.int32, x.shape, 1)
    s = stride
    while s < w:
        if s % LANES == 0:
            shifted = jnp.concatenate([x[:, s:], jnp.zeros((x.shape[0], s), F32)], axis=1)
        else:
            shifted = jnp.where(lane < w - s, pltpu.roll(x, w - s, axis=1), 0.0)
        x = x + shifted
        s *= 2
    return x


def _sb_sample_kernel(pt_ref, q_ref, bias_ref, kn_ref, vn_ref, *rest, scale, n_heads, n_new, pages_per_step):
    k_refs = rest[:pages_per_step]
    v_refs = rest[pages_per_step:2 * pages_per_step]
    o_ref, z_scr, w_scr, acc_scr, carry_scr = rest[2 * pages_per_step:]
    s = pl.program_id(1)
    width = kn_ref.shape[0]
    rows = n_new * n_heads
    lane_h = lax.broadcasted_iota(jnp.int32, (n_heads, width), 1) & (n_heads - 1)
    same_head = lane_h == lax.broadcasted_iota(jnp.int32, (n_heads, width), 0)
    row8 = lax.broadcasted_iota(jnp.int32, (SUBLANES, LANES), 0)
    q = q_ref[...].astype(BF16)

    def process(blocks, mask, carry_in):
        r_all = max(SUBLANES, len(blocks) * n_new)
        for p, (k_blk, _) in enumerate(blocks):
            zfull = _dot_nt(q, k_blk.astype(BF16))
            for i in range(n_new):
                part = zfull[i * n_heads:(i + 1) * n_heads, :]
                r = p * n_new + i
                z_scr[r:r + 1, :] = jnp.sum(jnp.where(same_head, part, 0.0), axis=0, keepdims=True)
        z = z_scr[0:r_all, :] * scale + bias_ref[...]
        log_sig, log_keep = _log_sigmoid_pair(z)
        if mask is not None:
            log_keep = jnp.where(mask, log_keep, 0.0)
        incl = _suffix_sum_strided(log_keep, n_heads)
        row128 = lax.broadcasted_iota(jnp.int32, (r_all, LANES), 0)
        lane128 = lax.broadcasted_iota(jnp.int32, (r_all, LANES), 1)
        tot = jnp.where(lane128 < n_heads, incl[:, :LANES], 0.0)
        shift = n_heads
        while shift < LANES:
            tot = tot + pltpu.roll(tot, shift, axis=1)
            shift *= 2
        pre = tot
        shift = n_new
        while shift < r_all:
            pre = pre + jnp.where(row128 >= shift, pltpu.roll(pre, shift, axis=0), 0.0)
            shift *= 2
        carry = jnp.concatenate([carry_in] * (r_all // SUBLANES), axis=0) + (pre - tot)
        w = jnp.exp(log_sig + (incl - log_keep) + jnp.concatenate([carry] * (width // LANES), axis=1))
        if mask is not None:
            w = jnp.where(mask, w, 0.0)
        pv = None
        for p, (_, v_blk) in enumerate(blocks):
            for i in range(n_new):
                r = p * n_new + i
                w_scr[p * rows + i * n_heads:p * rows + (i + 1) * n_heads, :] = jnp.where(
                    same_head, jnp.broadcast_to(w[r:r + 1, :], (n_heads, width)), 0.0).astype(BF16)
            d = _dot(w_scr[p * rows:(p + 1) * rows, :], v_blk.astype(BF16))
            pv = d if pv is None else pv + d
        last = pre[r_all - SUBLANES:, :]
        total = jnp.where(row8 >= n_new, last, pltpu.roll(last, n_new, axis=0))
        return pv, carry_in + total

    @pl.when(s == 0)
    def _():
        z_scr[...] = jnp.zeros_like(z_scr)
        row = lax.broadcasted_iota(jnp.int32, (SUBLANES, width), 0)
        lane = lax.broadcasted_iota(jnp.int32, (SUBLANES, width), 1)
        pv, carry = process([(kn_ref[...], vn_ref[...])], (row < n_new) & (lane < row * n_heads),
                            jnp.zeros(carry_scr.shape, F32))
        acc_scr[...] = pv
        carry_scr[...] = carry

    pv, carry = process([(k_refs[p][...], v_refs[p][...]) for p in range(pages_per_step)], None, carry_scr[...])
    acc_scr[...] += pv
    carry_scr[...] = carry

    @pl.when(s == pl.num_programs(1) - 1)
    def _():
        o_ref[...] = acc_scr[...]


def sb_attend_sample(q, k_new, v_new, cache_k, cache_v, page_table, bias, layer):
    n_seq, n_new, n_heads, dh = q.shape
    assert 2 * n_new == SUBLANES, "two pages of n_new query rows share one 8-sublane array"
    n_pages = page_table.shape[1]
    n_attn, n_pool, page = cache_k.shape[:3]
    width = page * n_heads
    pps = SB_PAGES_PER_STEP
    rows = n_new * n_heads
    pad = ((0, 0), (0, page - n_new), (0, 0), (0, 0))
    k_pad = jnp.pad(k_new, pad).reshape(n_seq, width, dh)
    v_pad = jnp.pad(v_new, pad).reshape(n_seq, width, dh)
    ck = cache_k.reshape(n_attn, n_pool, width, dh)
    cv = cache_v.reshape(n_attn, n_pool, width, dh)
    bias_lanes = jnp.tile(bias.astype(F32), page).reshape(1, width)

    def page_spec(j):
        return pl.BlockSpec((None, None, width, dh),
                            lambda b, s, pt: (layer, pt[b, n_pages - 1 - (s * pps + j)], 0, 0))

    seq_spec = pl.BlockSpec((None, rows, dh), lambda b, s, pt: (b, 0, 0))
    new_spec = pl.BlockSpec((None, width, dh), lambda b, s, pt: (b, 0, 0))
    kern = functools.partial(_sb_sample_kernel, scale=dh ** -0.5, n_heads=n_heads, n_new=n_new,
                             pages_per_step=pps)
    grid_spec = pltpu.PrefetchScalarGridSpec(
        num_scalar_prefetch=1,
        grid=(n_seq, n_pages // pps),
        in_specs=[seq_spec, pl.BlockSpec((1, width), lambda b, s, pt: (0, 0)), new_spec, new_spec]
                 + [page_spec(j) for j in range(pps)] * 2,
        out_specs=seq_spec,
        scratch_shapes=[pltpu.VMEM((pps * n_new, width), F32), pltpu.VMEM((pps * rows, width), BF16),
                        pltpu.VMEM((rows, dh), F32), pltpu.VMEM((SUBLANES, LANES), F32)],
    )
    out = pl.pallas_call(
        kern,
        grid_spec=grid_spec,
        out_shape=jax.ShapeDtypeStruct((n_seq, rows, dh), F32),
        compiler_params=_cparams("parallel", "arbitrary"),
        name="sb_attend_sample",
    )(page_table, q.reshape(n_seq, rows, dh), bias_lanes, k_pad, v_pad, *([ck] * pps), *([cv] * pps))
    return out.reshape(n_seq, n_new, n_heads * dh)


def _ml_gates_kernel(u_ref, wg_ref, b_ref, o_ref, *, n_heads):
    pre = _dot_nt(wg_ref[...].astype(BF16), u_ref[...]) + b_ref[...]
    pre = GATE_SOFTCAP * jnp.tanh(pre / GATE_SOFTCAP)
    o_ref[0:n_heads, :] = pre[0:n_heads]
    o_ref[n_heads:, :] = -_softplus(-pre[n_heads:])


def ml_gates(u, wg_t, b_gate):
    m, d = u.shape
    g = wg_t.shape[0]
    tm = min(m, ROW_TILE)
    return pl.pallas_call(
        functools.partial(_ml_gates_kernel, n_heads=g // 2),
        grid=(m // tm,),
        in_specs=[pl.BlockSpec((tm, d), lambda i: (i, 0)),
                  pl.BlockSpec((g, d), lambda i: (0, 0)),
                  pl.BlockSpec((g, 1), lambda i: (0, 0))],
        out_specs=pl.BlockSpec((g, tm), lambda i: (0, i)),
        out_shape=jax.ShapeDtypeStruct((g, m), F32),
        compiler_params=_cparams("parallel"),
        name="ml_gates",
    )(u, wg_t, b_gate)


def mlstm_gate_layout(g, b, t, n_heads):
    hps = ML_HEADS_PER_STEP
    t_pad = -(-t // CHUNK) * CHUNK
    g = jnp.pad(g.reshape(2, n_heads // hps, hps, b, t), ((0, 0),) * 4 + ((0, t_pad - t),))
    g = g.reshape(2, n_heads // hps, hps, b, t_pad // CHUNK, CHUNK).transpose(3, 4, 1, 2, 0, 5)
    return g.reshape(b, t_pad // CHUNK, n_heads // hps, 2 * hps, CHUNK)


def _lane_cumsum(x):
    lane = lax.broadcasted_iota(jnp.int32, x.shape, 1)
    shift = 1
    while shift < x.shape[1]:
        x = x + jnp.where(lane >= shift, pltpu.roll(x, shift, axis=1), 0.0)
        shift *= 2
    return x


def _mlstm_kernel(g_ref, q_ref, k_ref, v_ref, og_ref, gn_ref, c0_ref, n0_ref, m0_ref,
                  y_ref, c_out, n_out, m_out, c_scr, n_scr, m_scr, *, n_valid, k_scale):
    ci = pl.program_id(2)
    L = g_ref.shape[1]
    t_in = q_ref.shape[0]
    hps = c_scr.shape[0]
    dk = c_scr.shape[1]
    dv = c_scr.shape[2]

    @pl.when(ci == 0)
    def _():
        c_scr[...] = c0_ref[...]
        n_scr[...] = n0_ref[...]
        m_scr[...] = m0_ref[...]

    g = g_ref[...]
    row8 = lax.broadcasted_iota(jnp.int32, (2 * hps, L), 0)
    if n_valid < L:
        valid = lax.broadcasted_iota(jnp.int32, (2 * hps, L), 1) < n_valid
        g = jnp.where(valid, g, jnp.where((row8 & 1) == 0, NEG_BIG, 0.0))
    bc = _lane_cumsum(g)
    rows = jnp.concatenate([jnp.where((row8 & 1) == 0, g, bc), jnp.zeros((L - 2 * hps, L), F32)], axis=0)
    cols = rows.T
    t_idx = lax.broadcasted_iota(jnp.int32, (L, L), 0)
    s_idx = lax.broadcasted_iota(jnp.int32, (L, L), 1)
    causal = s_idx <= t_idx

    for hh in range(hps):
        li_r = g[2 * hh:2 * hh + 1, :]
        bc_r = bc[2 * hh + 1:2 * hh + 2, :]
        li_c = cols[:, 2 * hh:2 * hh + 1]
        bc_c = cols[:, 2 * hh + 1:2 * hh + 2]
        m_prev = m_scr[hh, 0:1, 0:1]
        dmat = jnp.where(causal, bc_c + (li_r - bc_r), NEG_BIG)
        inter = bc_c + m_prev
        m_t = jnp.maximum(inter, jnp.max(dmat, axis=1, keepdims=True))

        qf = _pad_rows(q_ref[:, hh * dk:(hh + 1) * dk], L)
        q = qf.astype(BF16)
        ks_f = _pad_rows(k_ref[:, hh * dk:(hh + 1) * dk], L) * k_scale
        v = _pad_rows(v_ref[:, hh * dv:(hh + 1) * dv], L).astype(BF16)
        a = jnp.exp(dmat - m_t) * _dot_nt(q, ks_f.astype(BF16))
        sc = jnp.exp(inter - m_t)
        c = c_scr[hh]
        n_row = n_scr[hh]
        num = _dot(a.astype(BF16), v) + sc * _dot(q, c.astype(BF16))
        den = jnp.sum(a, axis=1, keepdims=True) + sc * jnp.sum(qf * n_row, axis=1, keepdims=True)
        hval = num / jnp.maximum(jnp.abs(den), jnp.exp(-m_t))
        gate = jax.nn.sigmoid(og_ref[:, hh * dv:(hh + 1) * dv])
        y_ref[:, hh * dv:(hh + 1) * dv] = (gate * _rms(hval[:t_in], gn_ref[...])).astype(y_ref.dtype)

        m_new = m_t[L - 1:L, :]
        bc_last = bc_c[L - 1:L, :]
        wl = jnp.exp(bc_last - bc_c + li_c - m_new)
        sl = jnp.exp(bc_last + m_prev - m_new)
        kw = ks_f * wl
        c_scr[hh] = sl * c + _dot(kw.T.astype(BF16), v)
        n_scr[hh] = sl * n_row + jnp.sum(kw, axis=0, keepdims=True)
        m_scr[hh] = jnp.broadcast_to(m_new, (1, LANES))

    @pl.when(ci == pl.num_programs(2) - 1)
    def _():
        c_out[...] = c_scr[...]
        n_out[...] = n_scr[...]
        m_out[...] = m_scr[...]


def mlstm_scan(proj, gates, out_gain, c0, n0, m0, n_heads, n_valid):
    b, t, _ = proj.shape
    L = min(t, CHUNK)
    dk = c0.shape[2]
    dv = c0.shape[3]
    hps = ML_HEADS_PER_STEP
    n_grp = n_heads // hps
    v_blk = 2 * n_heads * dk // (hps * dv)
    kern = functools.partial(_mlstm_kernel, n_valid=n_valid, k_scale=dk ** -0.5)
    state_specs = [pl.BlockSpec((None, hps, dk, dv), lambda bi, h, ci: (bi, h, 0, 0)),
                   pl.BlockSpec((None, hps, 1, dk), lambda bi, h, ci: (bi, h, 0, 0)),
                   pl.BlockSpec((None, hps, 1, LANES), lambda bi, h, ci: (bi, h, 0, 0))]
    return pl.pallas_call(
        kern,
        grid=(b, n_grp, t // L),
        in_specs=[pl.BlockSpec((None, None, None, 2 * hps, CHUNK), lambda bi, h, ci: (bi, ci, h, 0, 0)),
                  pl.BlockSpec((None, L, hps * dk), lambda bi, h, ci: (bi, ci, h)),
                  pl.BlockSpec((None, L, hps * dk), lambda bi, h, ci: (bi, ci, n_grp + h)),
                  pl.BlockSpec((None, L, hps * dv), lambda bi, h, ci: (bi, ci, v_blk + h)),
                  pl.BlockSpec((None, L, hps * dv), lambda bi, h, ci: (bi, ci, v_blk + n_grp + h)),
                  pl.BlockSpec((1, dv), lambda bi, h, ci: (0, 0))] + state_specs,
        out_specs=[pl.BlockSpec((None, L, hps * dv), lambda bi, h, ci: (bi, ci, h))] + state_specs,
        out_shape=[jax.ShapeDtypeStruct((b, t, n_heads * dv), BF16),
                   jax.ShapeDtypeStruct(c0.shape, F32),
                   jax.ShapeDtypeStruct(n0.shape, F32),
                   jax.ShapeDtypeStruct(m0.shape, F32)],
        scratch_shapes=[pltpu.VMEM((hps, dk, dv), F32), pltpu.VMEM((hps, 1, dk), F32),
                        pltpu.VMEM((hps, 1, LANES), F32)],
        compiler_params=_cparams("parallel", "parallel", "arbitrary"),
        name="mlstm_scan",
    )(gates, proj, proj, proj, proj, out_gain, c0, n0, m0)


def _hgrn_tables(c):
    n_lev = int(np.log2(c))
    lev = np.zeros(((n_lev + 1) * c, c), np.float32)
    lev[:c] = np.tril(np.ones((c, c), np.float32))
    for l in range(1, n_lev + 1):
        m = 1 << (l - 1)
        for t in range(c):
            mid = (t >> l << l) + m
            if t & m:
                lev[l * c + t, mid:t + 1] = 1.0
            else:
                lev[l * c + t, t + 1:mid] = 1.0
    t = np.arange(c)[:, None]
    s = np.arange(c)[None, :]
    x = t ^ s
    pair_level = np.where(s > t, -1, np.where(s == t, 0, np.floor(np.log2(np.maximum(x, 1))).astype(np.int32) + 1))
    return jnp.asarray(np.concatenate([lev, lev], axis=1), BF16), jnp.asarray(pair_level, jnp.int32)


def _hgrn_kernel(lev_ref, lvl_ref, q_ref, f_ref, i_ref, g_ref, lb_ref, gn_ref, s0_ref,
                 y_ref, s_out, st_scr, *, n_valid):
    ci = pl.program_id(2)
    c = lvl_ref.shape[0]
    t_in = q_ref.shape[0]
    n_lev = lev_ref.shape[0] // c - 1
    hps = st_scr.shape[0]
    dh = st_scr.shape[1]

    @pl.when(ci == 0)
    def _():
        for hh in range(hps):
            st_scr[hh] = s0_ref[hh].T

    fp = _pad_rows(f_ref[...], c)
    lb = lb_ref[...]
    log_sig = -_softplus(-fp)
    la = jnp.log(lb)
    lbb = jnp.log1p(-lb) + log_sig
    lf = jnp.maximum(la, lbb) + jnp.log(1.0 + jnp.exp(-jnp.abs(la - lbb)))
    kk = (1.0 - lb) * jax.nn.sigmoid(-fp)
    if n_valid < c:
        valid = lax.broadcasted_iota(jnp.int32, (c, 1), 0) < n_valid
        lf = jnp.where(valid, lf, 0.0)
        kk = jnp.where(valid, kk, 0.0)

    hi, lo = _split_bf16(lf)
    d_all = _dot(lev_ref[...], jnp.concatenate([hi, lo], axis=0))
    bcum = d_all[0:c]
    q = _pad_rows(q_ref[...], c)
    iv_f = _pad_rows(i_ref[...], c)
    lvl = lvl_ref[...]
    heads = [slice(hh * dh, (hh + 1) * dh) for hh in range(hps)]
    qb = q.astype(BF16)
    kb = kk.astype(BF16)
    a = [jnp.where(lvl == 0, _dot_nt(qb[:, sl], kb[:, sl]), 0.0) for sl in heads]
    for l in range(1, n_lev + 1):
        e = jnp.exp(d_all[l * c:(l + 1) * c])
        qe = (q * e).astype(BF16)
        ke = (kk * e).astype(BF16)
        a = [a[hh] + jnp.where(lvl == l, _dot_nt(qe[:, sl], ke[:, sl]), 0.0) for hh, sl in enumerate(heads)]
    blast = bcum[c - 1:c, :]
    q_in = (q * jnp.exp(bcum)).astype(BF16)
    k_out = (kk * jnp.exp(blast - bcum)).astype(BF16)
    decay = jnp.exp(blast)
    gate = jax.nn.sigmoid(g_ref[...])
    for hh, sl in enumerate(heads):
        st = st_scr[hh]
        o = _dot(a[hh].astype(BF16), iv_f[:, sl].astype(BF16)) + _dot_nt(q_in[:, sl], st.astype(BF16))
        y_ref[:, sl] = (gate[:, sl] * _rms(o[:t_in], gn_ref[...])).astype(y_ref.dtype)
        st_scr[hh] = decay[:, sl] * st + _dot(iv_f[:, sl].T.astype(BF16), k_out[:, sl])

    @pl.when(ci == pl.num_programs(2) - 1)
    def _():
        for hh in range(hps):
            s_out[hh] = st_scr[hh].T


def hgrn_scan(proj, lower_bound, out_gain, s0, n_heads, n_valid):
    b, t, _ = proj.shape
    c = min(t, CHUNK)
    dh = s0.shape[2]
    hps = HG_HEADS_PER_STEP
    n_grp = n_heads // hps
    lev, pair_level = _hgrn_tables(CHUNK)
    kern = functools.partial(_hgrn_kernel, n_valid=n_valid)

    def part(p):
        return pl.BlockSpec((None, c, hps * dh), lambda bi, h, ci: (bi, ci, p * n_grp + h))

    state_spec = pl.BlockSpec((None, hps, dh, dh), lambda bi, h, ci: (bi, h, 0, 0))
    return pl.pallas_call(
        kern,
        grid=(b, n_grp, t // c),
        in_specs=[pl.BlockSpec(lev.shape, lambda bi, h, ci: (0, 0)),
                  pl.BlockSpec(pair_level.shape, lambda bi, h, ci: (0, 0)),
                  part(0), part(1), part(2), part(3),
                  pl.BlockSpec((1, hps * dh), lambda bi, h, ci: (0, h)),
                  pl.BlockSpec((1, dh), lambda bi, h, ci: (0, 0)),
                  state_spec],
        out_specs=[pl.BlockSpec((None, c, hps * dh), lambda bi, h, ci: (bi, ci, h)), state_spec],
        out_shape=[jax.ShapeDtypeStruct((b, t, n_heads * dh), BF16),
                   jax.ShapeDtypeStruct(s0.shape, F32)],
        scratch_shapes=[pltpu.VMEM((hps, dh, dh), F32)],
        compiler_params=_cparams("parallel", "parallel", "arbitrary"),
        name="hgrn_scan",
    )(lev, pair_level, proj, proj, proj, proj, lower_bound, out_gain, s0)


def _pad_tokens(x, t_pad):
    return jnp.pad(x, ((0, 0), (0, t_pad - x.shape[1])) + ((0, 0),) * (x.ndim - 2))


def kernel(x_prompt, x_sample, cache_k, cache_v, state_mlstm_c, state_mlstm_n, state_mlstm_m,
           state_hgrn, page_table, p_prompt, p_sample, norm_mix, norm_ffn, norm_ple,
           sb_w_qkv, sb_q_norm, sb_k_norm, sb_logit_bias, sb_w_out, ml_w_in, ml_b_gate,
           ml_out_norm, ml_w_out, hg_w_in, hg_lb_logits, hg_out_norm, hg_w_out,
           ffn_w_up, ffn_w_down, ple_w_proj, ple_w_gate):
    depth, d = norm_mix.shape
    bp, tp, _ = x_prompt.shape
    bs, ts, _ = x_sample.shape
    sb_heads = sb_logit_bias.shape[1]
    ml_heads = state_mlstm_c.shape[2]
    ml_dk = state_mlstm_c.shape[3]
    ml_dv = state_mlstm_c.shape[4]
    hg_heads = state_hgrn.shape[2]
    hg_dh = state_hgrn.shape[3]
    n_attn, n_pool, page = cache_k.shape[:3]
    ml_qk_w = ml_heads * ml_dk

    g_mix = norm_mix.reshape(depth, 1, d)
    g_ffn = norm_ffn.reshape(depth, 1, d)
    g_ple = norm_ple.reshape(depth, 1, d)
    lb_soft = jax.nn.softmax(hg_lb_logits.astype(F32), axis=0)
    lower_bounds = jnp.cumsum(lb_soft, axis=0) - lb_soft

    hp = x_prompt.reshape(bp * tp, d)
    hs = x_sample.reshape(bs * ts, d)
    pp_all = p_prompt.reshape(depth, bp * tp, -1)
    ps_all = p_sample.reshape(depth, bs * ts, -1)

    outs = {k: [] for k in ("ks", "vs", "mcp", "mnp", "mmp", "mcs", "mns", "mms", "hgp", "hgs")}
    kp_all = vp_all = None
    i_attn = i_ml = i_hg = 0
    for layer in range(depth):
        kind = layer % N_MIXERS
        up = rmsnorm_bf16(hp, g_mix, layer)
        us = rmsnorm_bf16(hs, g_mix, layer)
        if kind == 0:
            j = i_attn
            i_attn += 1
            gq = sb_q_norm[j].reshape(1, LANES)
            gk = sb_k_norm[j].reshape(1, LANES)
            q_p, q_s = linear("headnorm", up, sb_w_qkv, j, 0, d, shared=(gq,), side=(us, ()))
            kp_all, k_s = linear("headnorm", up, sb_w_qkv, j, d, d, shared=(gk,), side=(us, ()),
                                 slab=(n_attn, j, kp_all))
            vp_all, v_s = linear("plain", up, sb_w_qkv, j, 2 * d, d, side=(us, ()), slab=(n_attn, j, vp_all))
            q_s, k_s, v_s = (a.reshape(bs, ts, sb_heads, LANES) for a in (q_s, k_s, v_s))
            outs["ks"].append(k_s)
            outs["vs"].append(v_s)
            kv = (kp_all.reshape(n_attn, bp, tp, d), vp_all.reshape(n_attn, bp, tp, d))
            ap = sb_attend_prompt(q_p.reshape(bp, tp, d), kv, j, sb_logit_bias[j]).reshape(bp * tp, d)
            a_s = sb_attend_sample(q_s, k_s, v_s, cache_k, cache_v, page_table,
                                   sb_logit_bias[j], j).reshape(bs * ts, d)
            hp, hs = linear("residual", ap, sb_w_out, j, 0, d, per=(hp,), side=(a_s, (hs,)))
        elif kind == 1:
            j = i_ml
            i_ml += 1
            wg_t = ml_w_in[j][:, 2 * ml_qk_w + 2 * d:].T
            b_gate = ml_b_gate[j].reshape(-1, 1)
            gn = ml_out_norm[j].reshape(1, ml_dv)
            n_main = 2 * ml_qk_w + 2 * d
            proj_p, proj_s = linear("plain", up, ml_w_in, j, 0, n_main, side=(us, ()))
            proj_p = proj_p.reshape(bp, tp, n_main)
            proj_s = proj_s.reshape(bs, ts, n_main)
            gates_p = mlstm_gate_layout(ml_gates(up, wg_t, b_gate), bp, tp, ml_heads)
            gates_s = mlstm_gate_layout(ml_gates(us, wg_t, b_gate), bs, ts, ml_heads)
            zc = jnp.zeros((bp, ml_heads, ml_dk, ml_dv), F32)
            zn = jnp.zeros((bp, ml_heads, 1, ml_dk), F32)
            zm = jnp.zeros((bp, ml_heads, 1, LANES), F32)
            yp, cp, npv, mp = mlstm_scan(proj_p, gates_p, gn, zc, zn, zm, ml_heads, CHUNK)
            m0 = jnp.broadcast_to(state_mlstm_m[j][:, :, None, None], (bs, ml_heads, 1, LANES))
            ys, cs, nsv, ms = mlstm_scan(_pad_tokens(proj_s, SUBLANES), gates_s, gn, state_mlstm_c[j],
                                         state_mlstm_n[j][:, :, None, :], m0, ml_heads, ts)
            outs["mcp"].append(cp)
            outs["mnp"].append(npv[:, :, 0, :])
            outs["mmp"].append(mp[:, :, 0, 0])
            outs["mcs"].append(cs)
            outs["mns"].append(nsv[:, :, 0, :])
            outs["mms"].append(ms[:, :, 0, 0])
            hp, hs = linear("residual", yp.reshape(bp * tp, d), ml_w_out, j, 0, d, per=(hp,),
                            side=(ys[:, :ts].reshape(bs * ts, d), (hs,)))
        else:
            j = i_hg
            i_hg += 1
            lb = lower_bounds[layer].reshape(1, d)
            gn = hg_out_norm[j].reshape(1, hg_dh)
            proj_p, proj_s = linear("plain", up, hg_w_in, j, 0, 4 * d, side=(us, ()))
            proj_p = proj_p.reshape(bp, tp, 4 * d)
            proj_s = proj_s.reshape(bs, ts, 4 * d)
            zs = jnp.zeros((bp, hg_heads, hg_dh, hg_dh), F32)
            yp, sp = hgrn_scan(proj_p, lb, gn, zs, hg_heads, CHUNK)
            ys, ss = hgrn_scan(_pad_tokens(proj_s, SUBLANES), lb, gn, state_hgrn[j], hg_heads, ts)
            outs["hgp"].append(sp)
            outs["hgs"].append(ss)
            hp, hs = linear("residual", yp.reshape(bp * tp, d), hg_w_out, j, 0, d, per=(hp,),
                            side=(ys[:, :ts].reshape(bs * ts, d), (hs,)))
        a_p, a_s = linear("relu2", rmsnorm_bf16(hp, g_ffn, layer), ffn_w_up, layer, 0, ffn_w_up.shape[2],
                          side=(rmsnorm_bf16(hs, g_ffn, layer), ()), out_dtype=BF16, col_tile=FFN_COL_TILE)
        hp, hs = linear_residual_acc(a_p, ffn_w_down, layer, hp, a_s, hs)
        hp, hs = linear("ple", hp, ple_w_gate, layer, 0, d, shared=(ple_w_proj,), per=(pp_all, hp),
                        side=(hs, (ps_all, hs)), col_tile=PLE_COL_TILE, norm=(g_ple, layer))

    st = {k: jnp.stack(v) for k, v in outs.items()}
    kv_shape = (n_attn, bp, tp, sb_heads, LANES)
    return (hp.reshape(bp, tp, d), hs.reshape(bs, ts, d), kp_all.reshape(kv_shape), vp_all.reshape(kv_shape),
            st["ks"], st["vs"],
            st["mcp"], st["mnp"], st["mmp"], st["mcs"], st["mns"], st["mms"], st["hgp"], st["hgs"])
```
